```python
import math
import jax
import jax.numpy as jnp
from jax import lax
import numpy as np

D_MODEL = 1024
BATCH = 8
SEQ = 2048
DEPTH = 4

GRID_W = 64
CTX_LEN = 256
N_MIXERS = 3
EPS = 1e-6

CHUNK = 128
SG_WIDTH = D_MODEL
SG_GROUPS = 8
SG_GROUP_DIM = SG_WIDTH // SG_GROUPS

S5_GROUP_CH = 16
S5_WIDTH = D_MODEL
S5_GROUPS = S5_WIDTH // S5_GROUP_CH
S5_STATE = 64
S5_DT_MIN = 1e-3
S5_DT_MAX = 1e-1

MLA_HEADS = 8
MLA_NOPE = 128
MLA_ROPE = 64
MLA_QK = MLA_NOPE + MLA_ROPE
MLA_V = 128
MLA_Q_RANK = 384
MLA_KV_RANK = 256
ROPE_THETA = 10000.0
Q_BLOCK = 128

N_EXPERTS = 32
TOP_K = 4
D_EXPERT = D_MODEL
SWIGLU_LIMIT = 7.0
SWIGLU_ALPHA = 1.702
MOE_BLOCK = 256

kernel_name = 'hybrid_gmlp_s5_mla_moe_diffusion'


def rmsnorm(x, g):
    xf = x.astype(jnp.float32)
    y = xf * lax.rsqrt(jnp.mean(xf * xf, axis=-1, keepdims=True) + EPS)
    return (y * g.astype(jnp.float32)).astype(x.dtype)


def adaln(h, g, shift, scale):
    return rmsnorm(h, g) * (1 + scale) + shift


def chunk_gmlp(h, w_in, g_v, w_s, b_s, w_out):
    bsz, length, _ = h.shape
    u, v = jnp.split(jax.nn.gelu(h @ w_in, approximate=False), 2, axis=-1)
    v = rmsnorm(v, g_v).reshape(bsz, length // CHUNK, CHUNK, SG_GROUPS, SG_GROUP_DIM)
    sv = jnp.einsum('gpq,bnqgc->bnpgc', w_s, v) + b_s.T[None, None, :, :, None]
    return (u * sv.reshape(bsz, length, SG_WIDTH)) @ w_out


def s5_discretize(lam_re, lam_im, log_dt, b_re, b_im):
    f32 = jnp.float32
    lam_re = lam_re.astype(f32)
    lam_im = lam_im.astype(f32)
    b_re = b_re.astype(f32)
    b_im = b_im.astype(f32)
    dt = jnp.exp(log_dt.astype(f32))[:, None]
    mag = jnp.exp(lam_re * dt)
    a_re = mag * jnp.cos(lam_im * dt)
    a_im = mag * jnp.sin(lam_im * dt)
    den = lam_re * lam_re + lam_im * lam_im
    n_re = a_re - 1.0
    f_re = (n_re * lam_re + a_im * lam_im) / den
    f_im = (a_im * lam_re - n_re * lam_im) / den
    bb_re = f_re[..., None] * b_re - f_im[..., None] * b_im
    bb_im = f_re[..., None] * b_im + f_im[..., None] * b_re
    return a_re, a_im, bb_re, bb_im


def _ssm_combine(left, right):
    a1r, a1i, b1r, b1i = left
    a2r, a2i, b2r, b2i = right
    return (a2r * a1r - a2i * a1i, a2r * a1i + a2i * a1r,
            a2r * b1r - a2i * b1i + b2r, a2r * b1i + a2i * b1r + b2i)


def s5_scan(xs, a_re, a_im, bb_re, bb_im, h0_re, h0_im, reverse):
    u_re = jnp.einsum('blgc,gpc->blgp', xs, bb_re)
    u_im = jnp.einsum('blgc,gpc->blgp', xs, bb_im)
    first = -1 if reverse else 0
    u_re = u_re.at[:, first].add(a_re * h0_re - a_im * h0_im)
    u_im = u_im.at[:, first].add(a_re * h0_im + a_im * h0_re)
    ar = jnp.broadcast_to(a_re, u_re.shape)
    ai = jnp.broadcast_to(a_im, u_re.shape)
    _, _, h_re, h_im = lax.associative_scan(_ssm_combine, (ar, ai, u_re, u_im), axis=1, reverse=reverse)
    return h_re, h_im


def s5_readout(h_re, h_im, c_re, c_im):
    return jnp.einsum('blgp,gcp->blgc', h_re, c_re) - jnp.einsum('blgp,gcp->blgc', h_im, c_im)


def s5_mixer(n_lat, n_ctx, w_in, lam_re, lam_im, log_dt, b_re, b_im, c_re, c_im, d_skip, w_glu, ctx_out):
    f32 = jnp.float32
    bsz, length, _ = n_lat.shape
    xs_l = (n_lat @ w_in).astype(f32).reshape(bsz, length, S5_GROUPS, S5_GROUP_CH)
    xs_c = (n_ctx @ w_in).astype(f32).reshape(bsz, n_ctx.shape[1], S5_GROUPS, S5_GROUP_CH)
    d = d_skip.astype(f32).reshape(S5_GROUPS, S5_GROUP_CH)
    y_l = d * xs_l
    y_c = d * xs_c if ctx_out else None
    h0 = jnp.zeros((bsz, S5_GROUPS, S5_STATE), f32)
    for direction, rev in enumerate((False, True)):
        a_re, a_im, bb_re, bb_im = s5_discretize(lam_re[direction], lam_im[direction], log_dt[direction],
                                                 b_re[direction], b_im[direction])
        cr = c_re[direction].astype(f32)
        ci = c_im[direction].astype(f32)
        hc_re, hc_im = s5_scan(xs_c, a_re, a_im, bb_re, bb_im, h0, h0, rev)
        end = 0 if rev else -1
        hl_re, hl_im = s5_scan(xs_l, a_re, a_im, bb_re, bb_im, hc_re[:, end], hc_im[:, end], rev)
        y_l = y_l + s5_readout(hl_re, hl_im, cr, ci)
        if ctx_out:
            y_c = y_c + s5_readout(hc_re, hc_im, cr, ci)

    def glu_out(y):
        y = jax.nn.gelu(y.reshape(y.shape[0], y.shape[1], S5_WIDTH).astype(n_lat.dtype), approximate=False)
        z1, z2 = jnp.split(y @ w_glu, 2, axis=-1)
        return z1 * jax.nn.sigmoid(z2)

    return glu_out(y_l), (glu_out(y_c) if ctx_out else None)


def axial_rope(x):
    length = x.shape[1]
    rows = length // GRID_W
    row = jnp.repeat(jnp.arange(rows), GRID_W).astype(jnp.float32)
    col = jnp.tile(jnp.arange(GRID_W), rows).astype(jnp.float32)
    quarter = MLA_ROPE // 4
    inv_freq = ROPE_THETA ** (-jnp.arange(quarter, dtype=jnp.float32) / quarter)
    ang = jnp.stack([row[:, None] * inv_freq, col[:, None] * inv_freq], axis=1)
    cos = jnp.cos(ang)[None, :, None]
    sin = jnp.sin(ang)[None, :, None]
    xr = x.astype(jnp.float32).reshape(x.shape[:-1] + (2, 2, quarter))
    x1 = xr[..., 0, :]
    x2 = xr[..., 1, :]
    out = jnp.stack([x1 * cos - x2 * sin, x2 * cos + x1 * sin], axis=-2)
    return out.reshape(x.shape).astype(x.dtype)


def block_attention(q, k, v):
    bsz, nh, lq, dk = q.shape
    scale = 1.0 / math.sqrt(dk)
    qb = q.reshape(bsz, nh, lq // Q_BLOCK, Q_BLOCK, dk).transpose(2, 0, 1, 3, 4)

    def one_block(qblk):
        s = jnp.einsum('bhqd,bhkd->bhqk', qblk, k).astype(jnp.float32) * scale
        p = jax.nn.softmax(s, axis=-1).astype(v.dtype)
        return jnp.einsum('bhqk,bhkd->bhqd', p, v)

    o = lax.map(one_block, qb)
    return o.transpose(1, 2, 0, 3, 4).reshape(bsz, nh, lq, v.shape[-1])


def mla_mixer(n_lat, n_ctx, w_in, g_q, g_kv, w_uq, w_ukv, g_qn, g_kn, w_out, ctx_out):
    def project(h, rotate):
        bsz, length, _ = h.shape
        z = h @ w_in
        q_lat = z[..., :MLA_Q_RANK]
        kv_lat = z[..., MLA_Q_RANK:MLA_Q_RANK + MLA_KV_RANK]
        k_pe = z[..., MLA_Q_RANK + MLA_KV_RANK:]
        q = (rmsnorm(q_lat, g_q) @ w_uq).reshape(bsz, length, MLA_HEADS, MLA_QK)
        kv = (rmsnorm(kv_lat, g_kv) @ w_ukv).reshape(bsz, length, MLA_HEADS, MLA_NOPE + MLA_V)
        k = jnp.concatenate([kv[..., :MLA_NOPE],
                             jnp.broadcast_to(k_pe[:, :, None, :], (bsz, length, MLA_HEADS, MLA_ROPE))], axis=-1)
        v = kv[..., MLA_NOPE:]
        q = rmsnorm(q, g_qn)
        k = rmsnorm(k, g_kn)
        if rotate:
            q = jnp.concatenate([q[..., :MLA_NOPE], axial_rope(q[..., MLA_NOPE:])], axis=-1)
            k = jnp.concatenate([k[..., :MLA_NOPE], axial_rope(k[..., MLA_NOPE:])], axis=-1)
        return q.transpose(0, 2, 1, 3), k.transpose(0, 2, 1, 3), v.transpose(0, 2, 1, 3)

    def merge(o):
        bsz, nh, length, dv = o.shape
        return o.transpose(0, 2, 1, 3).reshape(bsz, length, nh * dv) @ w_out

    q_c, k_c, v_c = project(n_ctx, False)
    q_l, k_l, v_l = project(n_lat, True)
    o_l = block_attention(q_l, jnp.concatenate([k_c, k_l], axis=2), jnp.concatenate([v_c, v_l], axis=2))
    o_c = merge(block_attention(q_c, k_c, v_c)) if ctx_out else None
    return merge(o_l), o_c


def moe_ffn(h, w_router, b_router, w_gate, b_gate, w_up, b_up, w_down, b_down):
    n_tok, d = h.shape
    logits = (h @ w_router).astype(jnp.float32) + b_router.astype(jnp.float32)
    top_val, top_idx = lax.top_k(logits, TOP_K)
    gates = jax.nn.softmax(top_val, axis=-1)
    n_assign = n_tok * TOP_K
    flat_e = top_idx.reshape(-1)
    order = jnp.argsort(flat_e)
    sorted_e = flat_e[order]
    sorted_tok = (order // TOP_K).astype(jnp.int32)
    sorted_gate = gates.reshape(-1)[order]
    counts = jnp.bincount(flat_e, length=N_EXPERTS)
    padded = (counts + MOE_BLOCK - 1) // MOE_BLOCK * MOE_BLOCK
    pad_end = jnp.cumsum(padded)
    pad_start = pad_end - padded
    start = jnp.cumsum(counts) - counts
    dest = pad_start[sorted_e] + jnp.arange(n_assign) - start[sorted_e]
    n_blocks = -(-n_assign // MOE_BLOCK) + N_EXPERTS
    buf = n_blocks * MOE_BLOCK
    tok_buf = jnp.full((buf,), n_tok, jnp.int32).at[dest].set(sorted_tok)
    gate_buf = jnp.zeros((buf,), jnp.float32).at[dest].set(sorted_gate)
    blk_expert = jnp.minimum(jnp.searchsorted(pad_end, jnp.arange(n_blocks) * MOE_BLOCK, side='right'),
                             N_EXPERTS - 1)
    h_pad = jnp.concatenate([h, jnp.zeros((1, d), h.dtype)], axis=0)
    xb = h_pad[tok_buf].reshape(n_blocks, MOE_BLOCK, d)

    def expert_block(args):
        xblk, e = args
        g = jnp.minimum(xblk @ w_gate[e] + b_gate[e], SWIGLU_LIMIT)
        u = jnp.clip(xblk @ w_up[e] + b_up[e], -SWIGLU_LIMIT, SWIGLU_LIMIT)
        a = g * jax.nn.sigmoid(SWIGLU_ALPHA * g) * (u + 1)
        return a @ w_down[e] + b_down[e]

    yb = lax.map(expert_block, (xb, blk_expert)).reshape(buf, d)
    yb = yb * gate_buf[:, None].astype(yb.dtype)
    out = jnp.zeros((n_tok + 1, d), yb.dtype).at[tok_buf].add(yb)
    return out[:n_tok]


def setup_inputs(seed: int = 0) -> dict:
    f32 = jnp.float32
    key = jax.random.key(seed)
    ks = list(jax.random.split(key, 48))

    def nrm(shape, scale):
        return jax.random.normal(ks.pop(), shape, f32) * scale

    def gain(shape):
        return 1.0 + nrm(shape, 0.02)

    n_a = len(range(0, DEPTH, N_MIXERS))
    n_b = len(range(1, DEPTH, N_MIXERS))
    n_c = len(range(2, DEPTH, N_MIXERS))
    return {
        'x': nrm((BATCH, SEQ, D_MODEL), 1.0),
        'c': nrm((BATCH, D_MODEL), 1.0),
        'ctx': nrm((BATCH, CTX_LEN, D_MODEL), 1.0),
        'c_ctx': nrm((D_MODEL,), 1.0),
        'w_ada': nrm((DEPTH, D_MODEL, 6 * D_MODEL), 0.5 * D_MODEL ** -0.5),
        'b_ada': nrm((DEPTH, 6 * D_MODEL), 0.02),
        'g_norm1': gain((DEPTH, D_MODEL)),
        'g_norm2': gain((DEPTH, D_MODEL)),
        'sg_w_in': nrm((n_a, D_MODEL, 2 * SG_WIDTH), D_MODEL ** -0.5),
        'sg_g_v': gain((n_a, SG_WIDTH)),
        'sg_w_s': nrm((n_a, SG_GROUPS, CHUNK, CHUNK), CHUNK ** -0.5),
        'sg_b_s': gain((n_a, SG_GROUPS, CHUNK)),
        'sg_w_out': nrm((n_a, SG_WIDTH, D_MODEL), SG_WIDTH ** -0.5),
        'ssm_w_in': nrm((n_b, D_MODEL, S5_WIDTH), D_MODEL ** -0.5),
        'ssm_lam_re': -0.5 + nrm((n_b, 2, S5_GROUPS, S5_STATE), 0.01),
        'ssm_lam_im': jnp.pi * jnp.arange(S5_STATE, dtype=f32) + nrm((n_b, 2, S5_GROUPS, S5_STATE), 0.01),
        'ssm_log_dt': jax.random.uniform(ks.pop(), (n_b, 2, S5_GROUPS), f32,
                                         math.log(S5_DT_MIN), math.log(S5_DT_MAX)),
        'ssm_b_re': nrm((n_b, 2, S5_GROUPS, S5_STATE, S5_GROUP_CH), (2 * S5_GROUP_CH) ** -0.5),
        'ssm_b_im': nrm((n_b, 2, S5_GROUPS, S5_STATE, S5_GROUP_CH), (2 * S5_GROUP_CH) ** -0.5),
        'ssm_c_re': nrm((n_b, 2, S5_GROUPS, S5_GROUP_CH, S5_STATE), S5_STATE ** -0.5),
        'ssm_c_im': nrm((n_b, 2, S5_GROUPS, S5_GROUP_CH, S5_STATE), S5_STATE ** -0.5),
        'ssm_d': nrm((n_b, S5_WIDTH), 1.0),
        'ssm_w_glu': nrm((n_b, S5_WIDTH, 2 * D_MODEL), S5_WIDTH ** -0.5),
        'mla_w_in': nrm((n_c, D_MODEL, MLA_Q_RANK + MLA_KV_RANK + MLA_ROPE), D_MODEL ** -0.5),
        'mla_g_q': gain((n_c, MLA_Q_RANK)),
        'mla_g_kv': gain((n_c, MLA_KV_RANK)),
        'mla_w_uq': nrm((n_c, MLA_Q_RANK, MLA_HEADS * MLA_QK), MLA_Q_RANK ** -0.5),
        'mla_w_ukv': nrm((n_c, MLA_KV_RANK, MLA_HEADS * (MLA_NOPE + MLA_V)), MLA_KV_RANK ** -0.5),
        'mla_g_qn': gain((n_c, MLA_QK)),
        'mla_g_kn': gain((n_c, MLA_QK)),
        'mla_w_out': nrm((n_c, MLA_HEADS * MLA_V, D_MODEL), (MLA_HEADS * MLA_V) ** -0.5),
        'moe_w_router': nrm((DEPTH, D_MODEL, N_EXPERTS), D_MODEL ** -0.5),
        'moe_b_router': nrm((DEPTH, N_EXPERTS), 0.01),
        'moe_w_gate': nrm((DEPTH, N_EXPERTS, D_MODEL, D_EXPERT), D_MODEL ** -0.5),
        'moe_b_gate': nrm((DEPTH, N_EXPERTS, D_EXPERT), 0.01),
        'moe_w_up': nrm((DEPTH, N_EXPERTS, D_MODEL, D_EXPERT), D_MODEL ** -0.5),
        'moe_b_up': nrm((DEPTH, N_EXPERTS, D_EXPERT), 0.01),
        'moe_w_down': nrm((DEPTH, N_EXPERTS, D_EXPERT, D_MODEL), D_EXPERT ** -0.5),
        'moe_b_down': nrm((DEPTH, N_EXPERTS, D_MODEL), 0.01),
    }


def reference(x, c, ctx, c_ctx, w_ada, b_ada, g_norm1, g_norm2,
              sg_w_in, sg_g_v, sg_w_s, sg_b_s, sg_w_out,
              ssm_w_in, ssm_lam_re, ssm_lam_im, ssm_log_dt, ssm_b_re, ssm_b_im, ssm_c_re, ssm_c_im,
              ssm_d, ssm_w_glu,
              mla_w_in, mla_g_q, mla_g_kv, mla_w_uq, mla_w_ukv, mla_g_qn, mla_g_kn, mla_w_out,
              moe_w_router, moe_b_router, moe_w_gate, moe_b_gate, moe_w_up, moe_b_up, moe_w_down, moe_b_down):
    h_lat, h_ctx = x, ctx
    s_lat = jax.nn.silu(c)
    s_ctx = jax.nn.silu(c_ctx)
    for i in range(DEPTH):
        mixer, slot = i % N_MIXERS, i // N_MIXERS
        ctx_out = i < DEPTH - 1
        ctx_in = ctx_out or mixer != 0
        mod_l = jnp.split((s_lat @ w_ada[i] + b_ada[i])[:, None, :], 6, axis=-1)
        mod_c = jnp.split(s_ctx @ w_ada[i] + b_ada[i], 6, axis=-1)
        n_lat = adaln(h_lat, g_norm1[i], mod_l[0], mod_l[1])
        n_ctx = adaln(h_ctx, g_norm1[i], mod_c[0], mod_c[1]) if ctx_in else None
        if mixer == 0:
            m_lat = chunk_gmlp(n_lat, sg_w_in[slot], sg_g_v[slot], sg_w_s[slot], sg_b_s[slot], sg_w_out[slot])
            m_ctx = (chunk_gmlp(n_ctx, sg_w_in[slot], sg_g_v[slot], sg_w_s[slot], sg_b_s[slot], sg_w_out[slot])
                     if ctx_out else None)
        elif mixer == 1:
            m_lat, m_ctx = s5_mixer(n_lat, n_ctx, ssm_w_in[slot], ssm_lam_re[slot], ssm_lam_im[slot],
                                    ssm_log_dt[slot], ssm_b_re[slot], ssm_b_im[slot], ssm_c_re[slot],
                                    ssm_c_im[slot], ssm_d[slot], ssm_w_glu[slot], ctx_out)
        else:
            m_lat, m_ctx = mla_mixer(n_lat, n_ctx, mla_w_in[slot], mla_g_q[slot], mla_g_kv[slot],
                                     mla_w_uq[slot], mla_w_ukv[slot], mla_g_qn[slot], mla_g_kn[slot],
                                     mla_w_out[slot], ctx_out)
        h_lat = h_lat + mod_l[2] * m_lat
        n_lat = adaln(h_lat, g_norm2[i], mod_l[3], mod_l[4])
        moe_args = (moe_w_router[i], moe_b_router[i], moe_w_gate[i], moe_b_gate[i],
                    moe_w_up[i], moe_b_up[i], moe_w_down[i], moe_b_down[i])
        if ctx_out:
            h_ctx = h_ctx + mod_c[2] * m_ctx
            n_ctx = adaln(h_ctx, g_norm2[i], mod_c[3], mod_c[4])
            n_l = n_lat.shape[0] * n_lat.shape[1]
            f = moe_ffn(jnp.concatenate([n_lat.reshape(-1, D_MODEL), n_ctx.reshape(-1, D_MODEL)], axis=0),
                        *moe_args)
            h_lat = h_lat + mod_l[5] * f[:n_l].reshape(h_lat.shape)
            h_ctx = h_ctx + mod_c[5] * f[n_l:].reshape(h_ctx.shape)
        else:
            h_lat = h_lat + mod_l[5] * moe_ffn(n_lat.reshape(-1, D_MODEL), *moe_args).reshape(h_lat.shape)
    return h_lat
```

```python
import functools
import math

import jax
import jax.numpy as jnp
import numpy as np
from jax import lax
from jax.experimental import pallas as pl
from jax.experimental.pallas import tpu as pltpu

F32 = jnp.float32
BF16 = jnp.bfloat16
I32 = jnp.int32
EPS = 1e-6

N_MIXERS = 3
GRID_W = 64
CHUNK = 128
SG_GROUPS = 8
S5_GROUP_CH = 16
S5_STATE = 64
MLA_HEADS = 8
MLA_NOPE = 128
MLA_ROPE = 64
MLA_QK = MLA_NOPE + MLA_ROPE
MLA_V = 128
MLA_Q_RANK = 384
MLA_KV_RANK = 256
ROPE_THETA = 10000.0
N_EXPERTS = 32
TOP_K = 4
SWIGLU_LIMIT = 7.0
SWIGLU_ALPHA = 1.702

LANES = 128
SUBLANES = 8
VMEM_LIMIT = 56 * 1024 * 1024

TM = 256
S5_LC = 16
MOE_BM = 256
MOE_RMAX = TM * TOP_K + N_EXPERTS * SUBLANES
PIECES = (256, 128, 64, 32, 16, 8)


def _cparams(sem):
    return pltpu.CompilerParams(dimension_semantics=sem, vmem_limit_bytes=VMEM_LIMIT)


def _rms(x, g):
    ms = jnp.mean(x * x, axis=-1, keepdims=True)
    return x * lax.rsqrt(ms + EPS) * g


def _adaln(x, g, shift, scale):
    return _rms(x, g) * (1.0 + scale) + shift


def _gelu(x):
    return 0.5 * x * (1.0 + lax.erf(x * (1.0 / math.sqrt(2.0))))


def _sigmoid(x):
    return 1.0 / (1.0 + jnp.exp(-x))


def _mod_spec(layer, d, tiles_per_seg):
    return pl.BlockSpec((1, 1, 6 * d), lambda i: (layer * 16 + i // tiles_per_seg, 0, 0))


def _mod_kernel(s_ref, w_ref, b_ref, o_ref):
    s = s_ref[...]
    s = s * _sigmoid(s)
    o_ref[0] = jnp.dot(s, w_ref[0], preferred_element_type=F32, precision=lax.Precision.HIGHEST) + b_ref[0]


def modulation(c, c_ctx, w_ada, b_ada):
    depth, d, d6 = w_ada.shape
    nb = c.shape[0]
    s = jnp.zeros((16, d), F32).at[:nb].set(c).at[nb].set(c_ctx)
    tn = 1024
    out = pl.pallas_call(
        _mod_kernel,
        grid=(depth, d6 // tn),
        in_specs=[pl.BlockSpec((16, d), lambda l, j: (0, 0)),
                  pl.BlockSpec((1, d, tn), lambda l, j: (l, 0, j)),
                  pl.BlockSpec((1, 1, tn), lambda l, j: (l, 0, j))],
        out_specs=pl.BlockSpec((1, 16, tn), lambda l, j: (l, 0, j)),
        out_shape=jax.ShapeDtypeStruct((depth, 16, d6), F32),
        compiler_params=_cparams(("arbitrary", "arbitrary")),
        name="adaln_modulation",
    )(s, w_ada, b_ada.reshape(depth, 1, d6))
    return out.reshape(depth * 16, 1, d6)


def _gmlp_kernel(h_ref, mod_ref, g1_ref, win_ref, gv_ref, ws_ref, bs_ref, wout_ref, o_ref, gated_ref):
    d = h_ref.shape[1]
    tm = h_ref.shape[0]
    h = h_ref[...]
    mod = mod_ref[0]
    n = _adaln(h, g1_ref[...], mod[:, 0:d], mod[:, d:2 * d]).astype(BF16)
    z = _gelu(jnp.dot(n, win_ref[...], preferred_element_type=F32))
    u = z[:, :d]
    v = _rms(z[:, d:], gv_ref[...]).astype(BF16)
    gd = d // SG_GROUPS
    for j in range(tm // CHUNK):
        rows = slice(j * CHUNK, (j + 1) * CHUNK)
        for g in range(SG_GROUPS):
            cols = slice(g * gd, (g + 1) * gd)
            sv = jnp.dot(ws_ref[g], v[rows, cols], preferred_element_type=F32) + bs_ref[g]
            gated_ref[rows, cols] = (u[rows, cols] * sv).astype(BF16)
    m = jnp.dot(gated_ref[...], wout_ref[...], preferred_element_type=F32)
    o_ref[...] = h + mod[:, 2 * d:3 * d] * m


def gmlp_layer(h, mod3, layer, g1, w_in, g_v, w_s, b_s, w_out, n_rows, seg):
    d = h.shape[1]
    gd = d // SG_GROUPS
    bs_b = jnp.broadcast_to(b_s[:, :, None], (SG_GROUPS, CHUNK, gd)).astype(F32)
    full = lambda *shape: pl.BlockSpec(shape, lambda i: (0,) * len(shape))
    return pl.pallas_call(
        _gmlp_kernel,
        grid=(n_rows // TM,),
        in_specs=[pl.BlockSpec((TM, d), lambda i: (i, 0)),
                  _mod_spec(layer, d, seg // TM),
                  full(1, d), full(d, 2 * d), full(1, d),
                  full(SG_GROUPS, CHUNK, CHUNK), full(SG_GROUPS, CHUNK, gd), full(d, d)],
        out_specs=pl.BlockSpec((TM, d), lambda i: (i, 0)),
        out_shape=jax.ShapeDtypeStruct((n_rows, d), F32),
        scratch_shapes=[pltpu.VMEM((TM, d), BF16)],
        compiler_params=_cparams(("arbitrary",)),
        name="gmlp_mixer",
    )(h, mod3, g1.reshape(1, d), w_in.astype(BF16), g_v.reshape(1, d), w_s.astype(BF16), bs_b,
      w_out.astype(BF16))


def _ln_proj_kernel(h_ref, mod_ref, g1_ref, w_ref, o_ref):
    d = h_ref.shape[1]
    mod = mod_ref[0]
    n = _adaln(h_ref[...], g1_ref[...], mod[:, 0:d], mod[:, d:2 * d]).astype(BF16)
    o_ref[...] = jnp.dot(n, w_ref[...], preferred_element_type=F32)


def ln_proj(h, mod3, layer, g1, w, seg):
    n_rows, d = h.shape
    dn = w.shape[1]
    return pl.pallas_call(
        _ln_proj_kernel,
        grid=(n_rows // TM,),
        in_specs=[pl.BlockSpec((TM, d), lambda i: (i, 0)),
                  _mod_spec(layer, d, seg // TM),
                  pl.BlockSpec((1, d), lambda i: (0, 0)),
                  pl.BlockSpec((d, dn), lambda i: (0, 0))],
        out_specs=pl.BlockSpec((TM, dn), lambda i: (i, 0)),
        out_shape=jax.ShapeDtypeStruct((n_rows, dn), F32),
        compiler_params=_cparams(("arbitrary",)),
        name="adaln_in_proj",
    )(h, mod3, g1.reshape(1, d), w.astype(BF16))


def _out_proj_kernel(y_ref, h_ref, mod_ref, w_ref, o_ref, *, glu):
    d = h_ref.shape[1]
    mod = mod_ref[0]
    z = jnp.dot(y_ref[...], w_ref[...], preferred_element_type=F32)
    if glu:
        z = z[:, :d] * _sigmoid(z[:, d:])
    o_ref[...] = h_ref[...] + mod[:, 2 * d:3 * d] * z


def out_proj(y, h, mod3, layer, w, seg, glu):
    n_rows, d = h.shape
    dk, dn = w.shape
    return pl.pallas_call(
        functools.partial(_out_proj_kernel, glu=glu),
        grid=(n_rows // TM,),
        in_specs=[pl.BlockSpec((TM, dk), lambda i: (i, 0)),
                  pl.BlockSpec((TM, d), lambda i: (i, 0)),
                  _mod_spec(layer, d, seg // TM),
                  pl.BlockSpec((dk, dn), lambda i: (0, 0))],
        out_specs=pl.BlockSpec((TM, d), lambda i: (i, 0)),
        out_shape=jax.ShapeDtypeStruct((n_rows, d), F32),
        compiler_params=_cparams(("arbitrary",)),
        name="mixer_out_proj",
    )(y, h, mod3, w.astype(BF16))


def _s5_tables(lam_re, lam_im, log_dt, b_re, b_im, c_re, c_im, reverse):
    hp = lax.Precision.HIGHEST
    lc = S5_LC
    dt = jnp.exp(log_dt.astype(F32))[:, None]
    steps = jnp.arange(lc + 1, dtype=F32)[:, None, None]
    mag = jnp.exp(lam_re * dt * steps)
    pw_re = mag * jnp.cos(lam_im * dt * steps)
    pw_im = mag * jnp.sin(lam_im * dt * steps)
    a_re, a_im = pw_re[1], pw_im[1]
    den = lam_re * lam_re + lam_im * lam_im
    n_re = a_re - 1.0
    f_re = (n_re * lam_re + a_im * lam_im) / den
    f_im = (a_im * lam_re - n_re * lam_im) / den
    bb_re = f_re[..., None] * b_re - f_im[..., None] * b_im
    bb_im = f_re[..., None] * b_im + f_im[..., None] * b_re
    ca_re = c_re[None] * pw_re[:lc, :, None, :] - c_im[None] * pw_im[:lc, :, None, :]
    ca_im = c_re[None] * pw_im[:lc, :, None, :] + c_im[None] * pw_re[:lc, :, None, :]
    kd = (jnp.einsum('dgop,gpi->dgoi', ca_re, bb_re, precision=hp)
          - jnp.einsum('dgop,gpi->dgoi', ca_im, bb_im, precision=hp))
    t = np.arange(lc)
    lag = (t[:, None] - t[None, :]) if reverse else (t[None, :] - t[:, None])
    valid = jnp.asarray(lag >= 0)
    m = kd[np.clip(lag, 0, lc - 1)]
    m = jnp.where(valid[:, :, None, None, None], m, 0.0)
    m = m.transpose(2, 0, 4, 1, 3).reshape(-1, lc * S5_GROUP_CH, lc * S5_GROUP_CH)
    left = t if reverse else (lc - 1 - t)
    into = (lc - t) if reverse else (t + 1)
    bs_re = (pw_re[left][:, :, :, None] * bb_re[None] - pw_im[left][:, :, :, None] * bb_im[None])
    bs_im = (pw_re[left][:, :, :, None] * bb_im[None] + pw_im[left][:, :, :, None] * bb_re[None])
    to_bs = lambda x: x.transpose(1, 0, 3, 2).reshape(x.shape[1], lc * S5_GROUP_CH, S5_STATE)
    cs_re = c_re[None] * pw_re[into][:, :, None, :] - c_im[None] * pw_im[into][:, :, None, :]
    cs_im = c_re[None] * pw_im[into][:, :, None, :] + c_im[None] * pw_re[into][:, :, None, :]
    to_cs = lambda x: x.transpose(1, 3, 0, 2).reshape(x.shape[1], S5_STATE, lc * S5_GROUP_CH)
    pad_p = LANES - S5_STATE
    bs_re, bs_im = (jnp.pad(to_bs(x), ((0, 0), (0, 0), (0, pad_p))) for x in (bs_re, bs_im))
    cs_re, cs_imn = (jnp.pad(to_cs(x), ((0, 0), (0, pad_p), (0, 0))) for x in (cs_re, -cs_im))
    alc = jnp.pad(jnp.stack([pw_re[lc], pw_im[lc]], axis=1), ((0, 0), (0, 0), (0, pad_p)))
    return m, bs_re, bs_im, cs_re, cs_imn, alc


def _s5_scan_kernel(x_ref, m_ref, bs_ref, cs_ref, a_ref, d_ref, o_ref, s_ref, hp_ref, *, n_ctx_chunks):
    x = x_ref[0]
    xb = x.astype(BF16)
    n_chunks = x.shape[0] // SUBLANES
    for i in range(4):
        s_ref[i] = jnp.dot(xb, bs_ref[0, i], preferred_element_type=F32)
    a = a_ref[0]
    af_re, af_im, ar_re, ar_im = a[0:1], a[1:2], a[2:3], a[3:4]

    def step(i, carry):
        hf_re, hf_im, hr_re, hr_im = carry
        cf = pl.multiple_of(i * SUBLANES, SUBLANES)
        cr_idx = jnp.where(i < n_ctx_chunks, n_ctx_chunks - 1 - i, n_chunks - 1 + n_ctx_chunks - i)
        cr = pl.multiple_of(cr_idx * SUBLANES, SUBLANES)
        hp_ref[0, pl.ds(cf, SUBLANES), :] = hf_re
        hp_ref[1, pl.ds(cf, SUBLANES), :] = hf_im
        hp_ref[2, pl.ds(cr, SUBLANES), :] = hr_re
        hp_ref[3, pl.ds(cr, SUBLANES), :] = hr_im
        sf_re = s_ref[0, pl.ds(cf, SUBLANES), :]
        sf_im = s_ref[1, pl.ds(cf, SUBLANES), :]
        sr_re = s_ref[2, pl.ds(cr, SUBLANES), :]
        sr_im = s_ref[3, pl.ds(cr, SUBLANES), :]
        return (af_re * hf_re - af_im * hf_im + sf_re, af_re * hf_im + af_im * hf_re + sf_im,
                ar_re * hr_re - ar_im * hr_im + sr_re, ar_re * hr_im + ar_im * hr_re + sr_im)

    z = jnp.zeros((SUBLANES, LANES), F32)
    lax.fori_loop(0, n_chunks, step, (z, z, z, z))
    y = d_ref[0] * x + jnp.dot(xb, m_ref[0], preferred_element_type=F32)
    for i in range(4):
        y = y + jnp.dot(hp_ref[i].astype(BF16), cs_ref[0, i], preferred_element_type=F32)
    o_ref[0] = _gelu(y).astype(BF16)


def s5_layer(h, mod3, layer, g1, p, n_lat, n_ctx, nb, seg):
    d = h.shape[1]
    assert nb == SUBLANES
    lc, gc = S5_LC, S5_GROUP_CH
    ng = d // gc
    xs = ln_proj(h, mod3, layer, g1, p['w_in'], seg)
    l_lat, l_ctx = n_lat // nb, n_ctx // nb
    xl = xs[:n_lat].reshape(nb, l_lat // lc, lc, ng, gc)
    xc = xs[n_lat:].reshape(nb, l_ctx // lc, lc, ng, gc)
    x = jnp.concatenate([xc, xl], axis=1)
    n_chunks = x.shape[1]
    x = x.transpose(3, 1, 0, 2, 4).reshape(ng, n_chunks * nb, lc * gc)
    tabs = [_s5_tables(p['lam_re'][k], p['lam_im'][k], p['log_dt'][k], p['b_re'][k], p['b_im'][k],
                       p['c_re'][k], p['c_im'][k], reverse=bool(k)) for k in range(2)]
    m = (tabs[0][0] + tabs[1][0]).astype(BF16)
    bs = jnp.stack([tabs[0][1], tabs[0][2], tabs[1][1], tabs[1][2]], axis=1).astype(BF16)
    cs = jnp.stack([tabs[0][3], tabs[0][4], tabs[1][3], tabs[1][4]], axis=1).astype(BF16)
    alc = jnp.concatenate([tabs[0][5], tabs[1][5], jnp.zeros((ng, 4, LANES), F32)], axis=1)
    dsk = jnp.tile(p['d'].reshape(ng, 1, gc), (1, lc, 1)).reshape(ng, 1, lc * gc)
    rows, w = n_chunks * nb, lc * gc
    y = pl.pallas_call(
        functools.partial(_s5_scan_kernel, n_ctx_chunks=l_ctx // lc),
        grid=(ng,),
        in_specs=[pl.BlockSpec((1, rows, w), lambda g: (g, 0, 0)),
                  pl.BlockSpec((1, w, w), lambda g: (g, 0, 0)),
                  pl.BlockSpec((1, 4, w, LANES), lambda g: (g, 0, 0, 0)),
                  pl.BlockSpec((1, 4, LANES, w), lambda g: (g, 0, 0, 0)),
                  pl.BlockSpec((1, SUBLANES, LANES), lambda g: (g, 0, 0)),
                  pl.BlockSpec((1, 1, w), lambda g: (g, 0, 0))],
        out_specs=pl.BlockSpec((1, rows, w), lambda g: (g, 0, 0)),
        out_shape=jax.ShapeDtypeStruct((ng, rows, w), BF16),
        scratch_shapes=[pltpu.VMEM((4, rows, LANES), F32), pltpu.VMEM((4, rows, LANES), F32)],
        compiler_params=_cparams(("arbitrary",)),
        name="s5_chunked_scan",
    )(x, m, bs, cs, alc, dsk)
    y = y.reshape(ng, n_chunks, nb, lc, gc).transpose(2, 1, 3, 0, 4)
    yc = y[:, :l_ctx // lc].reshape(n_ctx, d)
    yl = y[:, l_ctx // lc:].reshape(n_lat, d)
    return out_proj(jnp.concatenate([yl, yc], axis=0), h, mod3, layer, p['w_glu'], seg, glu=True)


def _mla_proj_kernel(h_ref, mod_ref, g1_ref, win_ref, gq_ref, gkv_ref, wuq_ref, wukv_ref,
                     gqn_ref, gkn_ref, cos_ref, sin_ref, q_ref, k_ref, v_ref):
    d = h_ref.shape[1]
    mod = mod_ref[0]
    n = _adaln(h_ref[...], g1_ref[...], mod[:, 0:d], mod[:, d:2 * d]).astype(BF16)
    z = jnp.dot(n, win_ref[...], preferred_element_type=F32)
    ql = _rms(z[:, :MLA_Q_RANK], gq_ref[...]).astype(BF16)
    r0 = MLA_Q_RANK + MLA_KV_RANK
    kvl = _rms(z[:, MLA_Q_RANK:r0], gkv_ref[...]).astype(BF16)
    pe, pe_sw = z[:, r0:r0 + LANES], z[:, r0 + LANES:r0 + 2 * LANES]
    qa = jnp.dot(ql, wuq_ref[...], preferred_element_type=F32)
    kv = jnp.dot(kvl, wukv_ref[...], preferred_element_type=F32)
    cos, sin = cos_ref[...], sin_ref[...]
    gqn, gkn = gqn_ref[...], gkn_ref[...]
    kr = (pe * gkn[1:2]) * cos + (pe_sw * gkn[2:3]) * sin
    pe_ss = jnp.sum(pe * pe, axis=-1, keepdims=True)
    hw = 2 * LANES
    qscale = 1.0 / math.sqrt(MLA_QK)
    for hd in range(MLA_HEADS):
        qn = qa[:, hd * hw:hd * hw + LANES]
        qr = qa[:, hd * hw + LANES:(hd + 1) * hw]
        qsw = qa[:, MLA_HEADS * hw + hd * LANES:MLA_HEADS * hw + (hd + 1) * LANES]
        ss = jnp.sum(qn * qn, axis=-1, keepdims=True) + jnp.sum(qr * qr, axis=-1, keepdims=True)
        rq = lax.rsqrt(ss * (1.0 / MLA_QK) + EPS) * qscale
        q_ref[:, hd * hw:hd * hw + LANES] = (qn * rq * gqn[0:1]).astype(BF16)
        q_ref[:, hd * hw + LANES:(hd + 1) * hw] = (rq * ((qr * gqn[1:2]) * cos + (qsw * gqn[2:3]) * sin)).astype(BF16)
        kn = kv[:, hd * LANES:(hd + 1) * LANES]
        rk = lax.rsqrt((jnp.sum(kn * kn, axis=-1, keepdims=True) + pe_ss) * (1.0 / MLA_QK) + EPS)
        k_ref[:, hd * hw:hd * hw + LANES] = (kn * rk * gkn[0:1]).astype(BF16)
        k_ref[:, hd * hw + LANES:(hd + 1) * hw] = (kr * rk).astype(BF16)
    v_ref[...] = kv[:, MLA_HEADS * LANES:].astype(BF16)


def _attn_kernel(q_ref, kc_ref, vc_ref, *rest, with_latent):
    if with_latent:
        kl_ref, vl_ref, o_ref = rest
    else:
        (o_ref,) = rest
    q = q_ref[...]
    nt = (((1,), (1,)), ((), ()))
    s1 = lax.dot_general(q, kc_ref[...], nt, preferred_element_type=F32)
    mx = jnp.max(s1, axis=-1, keepdims=True)
    if with_latent:
        s2 = lax.dot_general(q, kl_ref[...], nt, preferred_element_type=F32)
        mx = jnp.maximum(mx, jnp.max(s2, axis=-1, keepdims=True))
    p1 = jnp.exp(s1 - mx)
    den = jnp.sum(p1, axis=-1, keepdims=True)
    o = jnp.dot(p1.astype(BF16), vc_ref[...], preferred_element_type=F32)
    if with_latent:
        p2 = jnp.exp(s2 - mx)
        den = den + jnp.sum(p2, axis=-1, keepdims=True)
        o = o + jnp.dot(p2.astype(BF16), vl_ref[...], preferred_element_type=F32)
    o_ref[...] = (o / den).astype(BF16)


def _rope_tables(l_lat, n_rows_id):
    rows = l_lat // GRID_W
    row = jnp.repeat(jnp.arange(rows), GRID_W).astype(F32)
    col = jnp.tile(jnp.arange(GRID_W), rows).astype(F32)
    quarter = MLA_ROPE // 4
    inv_freq = ROPE_THETA ** (-jnp.arange(quarter, dtype=F32) / quarter)
    ang_r, ang_c = row[:, None] * inv_freq, col[:, None] * inv_freq
    cos = jnp.concatenate([jnp.cos(ang_r), jnp.cos(ang_r), jnp.cos(ang_c), jnp.cos(ang_c)], axis=1)
    sin = jnp.concatenate([-jnp.sin(ang_r), jnp.sin(ang_r), -jnp.sin(ang_c), jnp.sin(ang_c)], axis=1)
    pad = ((0, n_rows_id), (0, LANES - MLA_ROPE))
    cos = jnp.pad(cos, pad).at[l_lat:, :MLA_ROPE].set(1.0)
    return cos, jnp.pad(sin, pad)


def mla_layer(h, mod3, layer, g1, p, n_lat, n_ctx, nb, seg):
    n_rows, d = h.shape
    hw = 2 * LANES
    nh = MLA_HEADS
    l_lat, l_ctx = n_lat // nb, n_ctx // nb
    r0 = MLA_Q_RANK + MLA_KV_RANK
    swap = np.arange(MLA_ROPE)
    swap = np.where(swap % 32 < 16, swap + 16, swap - 16)
    lane_pad = lambda x: jnp.pad(x, ((0, 0),) * (x.ndim - 1) + ((0, LANES - x.shape[-1]),))
    w_pe = p['w_in'][:, r0:]
    w_in_ext = jnp.concatenate([p['w_in'][:, :r0], lane_pad(w_pe), lane_pad(w_pe[:, swap])], axis=1).astype(BF16)
    wq = p['w_uq'].reshape(MLA_Q_RANK, nh, MLA_QK)
    wq_main = jnp.pad(wq, ((0, 0), (0, 0), (0, hw - MLA_QK))).reshape(MLA_Q_RANK, nh * hw)
    wq_sw = lane_pad(wq[:, :, MLA_NOPE:][:, :, swap]).reshape(MLA_Q_RANK, nh * LANES)
    w_uq_ext = jnp.concatenate([wq_main, wq_sw], axis=1).astype(BF16)
    wkv = p['w_ukv'].reshape(MLA_KV_RANK, nh, 2, LANES).transpose(0, 2, 1, 3).reshape(MLA_KV_RANK, 2 * nh * LANES)
    gains = lambda g: jnp.zeros((SUBLANES, LANES), F32).at[0].set(g[:MLA_NOPE]).at[1, :MLA_ROPE].set(
        g[MLA_NOPE:]).at[2, :MLA_ROPE].set(g[MLA_NOPE:][swap])
    cos, sin = _rope_tables(l_lat, seg)
    t_lat = l_lat // TM
    tps = seg // TM
    rope_map = lambda i: (jnp.where(i < (n_lat // TM), i % t_lat, t_lat + i % tps), 0)
    full = lambda *shape: pl.BlockSpec(shape, lambda i: (0,) * len(shape))
    q, k, v = pl.pallas_call(
        _mla_proj_kernel,
        grid=(n_rows // TM,),
        in_specs=[pl.BlockSpec((TM, d), lambda i: (i, 0)), _mod_spec(layer, d, tps), full(1, d),
                  full(d, r0 + 2 * LANES), full(1, MLA_Q_RANK), full(1, MLA_KV_RANK),
                  full(MLA_Q_RANK, nh * (hw + LANES)), full(MLA_KV_RANK, 2 * nh * LANES),
                  full(SUBLANES, LANES), full(SUBLANES, LANES),
                  pl.BlockSpec((TM, LANES), rope_map), pl.BlockSpec((TM, LANES), rope_map)],
        out_specs=[pl.BlockSpec((TM, nh * hw), lambda i: (i, 0)), pl.BlockSpec((TM, nh * hw), lambda i: (i, 0)),
                   pl.BlockSpec((TM, nh * LANES), lambda i: (i, 0))],
        out_shape=[jax.ShapeDtypeStruct((n_rows, nh * hw), BF16), jax.ShapeDtypeStruct((n_rows, nh * hw), BF16),
                   jax.ShapeDtypeStruct((n_rows, nh * LANES), BF16)],
        compiler_params=_cparams(("arbitrary",)),
        name="mla_projection",
    )(h, mod3, g1.reshape(1, d), w_in_ext, p['g_q'].reshape(1, -1), p['g_kv'].reshape(1, -1), w_uq_ext,
      wkv.astype(BF16), gains(p['g_qn']), gains(p['g_kn']), cos, sin)

    tq = min(512, l_lat)
    cb = n_lat // l_ctx
    o_lat = pl.pallas_call(
        functools.partial(_attn_kernel, with_latent=True),
        grid=(nb, nh, l_lat // tq),
        in_specs=[pl.BlockSpec((tq, hw), lambda b, hd, i: (b * (l_lat // tq) + i, hd)),
                  pl.BlockSpec((l_ctx, hw), lambda b, hd, i: (cb + b, hd)),
                  pl.BlockSpec((l_ctx, LANES), lambda b, hd, i: (cb + b, hd)),
                  pl.BlockSpec((l_lat, hw), lambda b, hd, i: (b, hd)),
                  pl.BlockSpec((l_lat, LANES), lambda b, hd, i: (b, hd))],
        out_specs=pl.BlockSpec((tq, LANES), lambda b, hd, i: (b * (l_lat // tq) + i, hd)),
        out_shape=jax.ShapeDtypeStruct((n_lat, nh * LANES), BF16),
        compiler_params=_cparams(("arbitrary", "arbitrary", "arbitrary")),
        name="mla_attention_latent",
    )(q, k, v, k, v)
    o_ctx = pl.pallas_call(
        functools.partial(_attn_kernel, with_latent=False),
        grid=(nb, nh),
        in_specs=[pl.BlockSpec((l_ctx, hw), lambda b, hd: (cb + b, hd)),
                  pl.BlockSpec((l_ctx, hw), lambda b, hd: (cb + b, hd)),
                  pl.BlockSpec((l_ctx, LANES), lambda b, hd: (cb + b, hd))],
        out_specs=pl.BlockSpec((l_ctx, LANES), lambda b, hd: (b, hd)),
        out_shape=jax.ShapeDtypeStruct((n_ctx, nh * LANES), BF16),
        compiler_params=_cparams(("arbitrary", "arbitrary")),
        name="mla_attention_context",
    )(q, k, v)
    return out_proj(jnp.concatenate([o_lat, o_ctx], axis=0), h, mod3, layer, p['w_out'], seg, glu=False)


def _route_kernel(h_ref, mod_ref, g2_ref, wrt_ref, br_ref, n2_ref, pos_ref, gate_ref, cnt_ref):
    d = h_ref.shape[1]
    tm = h_ref.shape[0]
    ne = wrt_ref.shape[0]
    mod = mod_ref[0]
    n2 = _adaln(h_ref[...], g2_ref[...], mod[:, 3 * d:4 * d], mod[:, 4 * d:5 * d])
    n2_ref[...] = n2.astype(BF16)
    logits = lax.dot_general(wrt_ref[...], n2, (((1,), (1,)), ((), ())), preferred_element_type=F32,
                             precision=lax.Precision.HIGHEST) + br_ref[...]
    eidx = lax.broadcasted_iota(I32, (ne, tm), 0).astype(F32)
    work = logits
    vals, idxs = [], []
    for _ in range(TOP_K):
        mx = jnp.max(work, axis=0, keepdims=True)
        idx = jnp.min(jnp.where(work == mx, eidx, float(ne)), axis=0, keepdims=True)
        vals.append(mx)
        idxs.append(idx)
        work = jnp.where(eidx == idx, -jnp.inf, work)
    ex = [jnp.exp(v - vals[0]) for v in vals]
    den = ex[0] + ex[1] + ex[2] + ex[3]
    sel = (eidx == idxs[0]) | (eidx == idxs[1]) | (eidx == idxs[2]) | (eidx == idxs[3])
    onehot = jnp.where(sel, 1.0, 0.0).astype(BF16)
    r = lax.broadcasted_iota(I32, (tm, tm), 0)
    c = lax.broadcasted_iota(I32, (tm, tm), 1)
    before = jnp.where(r < c, 1.0, 0.0).astype(BF16)
    rank_all = jnp.dot(onehot, before, preferred_element_type=F32)
    cnt = jnp.dot(onehot, jnp.ones((tm, tm), BF16), preferred_element_type=F32)
    cnt8 = jnp.floor((cnt + (SUBLANES - 1)) * (1.0 / SUBLANES)) * SUBLANES
    er = lax.broadcasted_iota(I32, (ne, LANES), 0)
    ec = lax.broadcasted_iota(I32, (ne, LANES), 1)
    lower = jnp.where(ec < er, 1.0, 0.0).astype(BF16)
    cnt8_p = jnp.concatenate([cnt8, jnp.zeros((LANES - ne, tm), F32)], axis=0).astype(BF16)
    off = jnp.dot(lower, cnt8_p, preferred_element_type=F32)
    base = rank_all + off
    for k in range(TOP_K):
        pos = jnp.sum(jnp.where(eidx == idxs[k], base, 0.0), axis=0, keepdims=True)
        pos_ref[k:k + 1, :] = pos.astype(I32)
        gate_ref[k:k + 1, :] = ex[k] / den
    cnt_ref[0] = cnt8[:, :LANES].astype(I32)


def moe_route(h, mod3, layer, g2, w_router, b_router, n_rows, seg):
    d = h.shape[1]
    ne = w_router.shape[1]
    nt = n_rows // TM
    return pl.pallas_call(
        _route_kernel,
        grid=(nt,),
        in_specs=[pl.BlockSpec((TM, d), lambda i: (i, 0)), _mod_spec(layer, d, seg // TM),
                  pl.BlockSpec((1, d), lambda i: (0, 0)), pl.BlockSpec((ne, d), lambda i: (0, 0)),
                  pl.BlockSpec((ne, 1), lambda i: (0, 0))],
        out_specs=[pl.BlockSpec((TM, d), lambda i: (i, 0)), pl.BlockSpec((TOP_K, TM), lambda i: (0, i)),
                   pl.BlockSpec((TOP_K, TM), lambda i: (0, i)), pl.BlockSpec((1, ne, LANES), lambda i: (i, 0, 0))],
        out_shape=[jax.ShapeDtypeStruct((n_rows, d), BF16), jax.ShapeDtypeStruct((TOP_K, n_rows), I32),
                   jax.ShapeDtypeStruct((TOP_K, n_rows), F32), jax.ShapeDtypeStruct((nt, ne, LANES), I32)],
        compiler_params=_cparams(("arbitrary",)),
        name="moe_route",
    )(h, mod3, g2.reshape(1, d), w_router.T, b_router.reshape(ne, 1))


def _chunk_copies(tab_ref, tile, e, vm_ref, hbm_ref, sem, to_hbm):
    ne = N_EXPERTS
    stride = tab_ref.shape[0] // 3
    off = tab_ref[tile * ne + e]
    n8 = tab_ref[stride + tile * ne + e]
    dst = tab_ref[2 * stride + tile * ne + e]
    out = []
    for size in PIECES:
        u = size // SUBLANES
        done = (n8 // (2 * u)) * (2 * u) * SUBLANES
        lo = pl.multiple_of(off + done, SUBLANES)
        hi = pl.multiple_of(dst + done, SUBLANES)
        v, hb = vm_ref.at[pl.ds(lo, size), :], hbm_ref.at[pl.ds(hi, size), :]
        cp = pltpu.make_async_copy(v, hb, sem) if to_hbm else pltpu.make_async_copy(hb, v, sem)
        out.append(((n8 // u) % 2 == 1, cp))
    return out


def _for_each_copy(tab_ref, tile, vm_ref, hbm_ref, sem, to_hbm, action):
    def body(e, carry):
        for cond, cp in _chunk_copies(tab_ref, tile, e, vm_ref, hbm_ref, sem, to_hbm):
            @pl.when(cond)
            def _():
                action(cp)
        return carry
    lax.fori_loop(0, N_EXPERTS, body, 0)


def _dispatch_kernel(tab_ref, tail_ref, nb_ref, n2_ref, pos_ref, xb_ref, sorted_ref, zero_ref, sem):
    i = pl.program_id(0)
    tm = n2_ref.shape[0]
    rmax = sorted_ref.shape[0]
    ridx = lax.broadcasted_iota(I32, (rmax, tm), 0)
    hit = ridx == pos_ref[0:1, :]
    for k in range(1, TOP_K):
        hit = hit | (ridx == pos_ref[k:k + 1, :])
    perm = jnp.where(hit, 1.0, 0.0).astype(BF16)
    sorted_ref[...] = jnp.dot(perm, n2_ref[...], preferred_element_type=F32)
    _for_each_copy(tab_ref, i, sorted_ref, xb_ref, sem, True, lambda cp: cp.start())
    _for_each_copy(tab_ref, i, sorted_ref, xb_ref, sem, True, lambda cp: cp.wait())

    @pl.when(i == pl.num_programs(0) - 1)
    def _():
        zero_ref[...] = jnp.zeros_like(zero_ref)
        _for_each_copy(tail_ref, 0, zero_ref, xb_ref, sem, True, lambda cp: cp.start())
        _for_each_copy(tail_ref, 0, zero_ref, xb_ref, sem, True, lambda cp: cp.wait())

        def spare(b, carry):
            cp = pltpu.make_async_copy(zero_ref, xb_ref.at[pl.ds(pl.multiple_of(b * MOE_BM, MOE_BM), MOE_BM), :], sem)
            cp.start()
            cp.wait()
            return carry
        lax.fori_loop(nb_ref[0], xb_ref.shape[0] // MOE_BM, spare, 0)


def _expert_kernel(be_ref, first_ref, nb_ref, x_ref, wg_ref, bg_ref, wu_ref, bu_ref, wd_ref, bd_ref, y_ref,
                   wg_s, wu_s, wd_s):
    b = pl.program_id(0)

    @pl.when(b < nb_ref[0])
    def _():
        @pl.when(first_ref[b] == 1)
        def _():
            wg_s[...] = wg_ref[0].astype(BF16)
            wu_s[...] = wu_ref[0].astype(BF16)
            wd_s[...] = wd_ref[0].astype(BF16)

        x = x_ref[...].astype(BF16)
        g = jnp.minimum(jnp.dot(x, wg_s[...], preferred_element_type=F32) + bg_ref[0], SWIGLU_LIMIT)
        u = jnp.clip(jnp.dot(x, wu_s[...], preferred_element_type=F32) + bu_ref[0], -SWIGLU_LIMIT, SWIGLU_LIMIT)
        a = (g * _sigmoid(SWIGLU_ALPHA * g) * (u + 1.0)).astype(BF16)
        y_ref[...] = jnp.dot(a, wd_s[...], preferred_element_type=F32) + bd_ref[0]

    @pl.when(b >= nb_ref[0])
    def _():
        y_ref[...] = jnp.zeros_like(y_ref)


def _combine_kernel(tab_ref, pos_ref, gate_ref, h_ref, mod_ref, yb_ref, o_ref, ys_ref, sem):
    i = pl.program_id(0)
    d = h_ref.shape[1]
    tm = h_ref.shape[0]
    rmax = ys_ref.shape[0]

    @pl.when(i == 0)
    def _():
        ys_ref[...] = jnp.zeros_like(ys_ref)

    _for_each_copy(tab_ref, i, ys_ref, yb_ref, sem, False, lambda cp: cp.start())
    cidx = lax.broadcasted_iota(I32, (tm, rmax), 1)
    wt = jnp.where(cidx == pos_ref[:, 0:1], gate_ref[:, 0:1], 0.0)
    for k in range(1, TOP_K):
        wt = wt + jnp.where(cidx == pos_ref[:, k:k + 1], gate_ref[:, k:k + 1], 0.0)
    _for_each_copy(tab_ref, i, ys_ref, yb_ref, sem, False, lambda cp: cp.wait())
    f = jnp.dot(wt.astype(BF16), ys_ref[...].astype(BF16), preferred_element_type=F32)
    o_ref[...] = h_ref[...] + mod_ref[0][:, 5 * d:6 * d] * f


def _moe_plan(cnt8, n_blocks_max):
    nt, ne = cnt8.shape
    loc_off = jnp.cumsum(cnt8, axis=1) - cnt8
    used = jnp.sum(cnt8, axis=0)
    rows_e = (used + MOE_BM - 1) // MOE_BM * MOE_BM
    e_end = jnp.cumsum(rows_e)
    e_start = e_end - rows_e
    dst = e_start[None, :] + jnp.cumsum(cnt8, axis=0) - cnt8
    tab = jnp.stack([loc_off, cnt8 // SUBLANES, dst]).reshape(3 * nt * ne).astype(I32)
    tail = jnp.stack([jnp.zeros((ne,), I32), (rows_e - used) // SUBLANES, e_start + used]).reshape(3 * ne).astype(I32)
    nb = (e_end[-1] // MOE_BM).astype(I32)
    blk = jnp.arange(n_blocks_max, dtype=I32)
    blk_c = jnp.minimum(blk, nb - 1)
    blk_e = jnp.minimum(jnp.searchsorted(e_end, blk_c * MOE_BM, side='right'), ne - 1).astype(I32)
    first = (blk_c * MOE_BM == e_start[blk_e]).astype(I32)
    return tab, tail, blk_e, first, nb.reshape(1)


def moe_layer(h, mod3, layer, g2, p, n_rows, seg):
    d = h.shape[1]
    ne = N_EXPERTS
    nt = n_rows // TM
    n2, pos, gate, cnt = moe_route(h, mod3, layer, g2, p['w_router'], p['b_router'], n_rows, seg)
    max_rows = n_rows * TOP_K + nt * ne * (SUBLANES - 1) + ne * (MOE_BM - 1)
    nbm = -(-max_rows // MOE_BM)
    tab, tail, blk_e, first, nb = _moe_plan(cnt[:, :, 0], nbm)

    xb = pl.pallas_call(
        _dispatch_kernel,
        grid_spec=pltpu.PrefetchScalarGridSpec(
            num_scalar_prefetch=3, grid=(nt,),
            in_specs=[pl.BlockSpec((TM, d), lambda i, *_: (i, 0)),
                      pl.BlockSpec((TOP_K, TM), lambda i, *_: (0, i))],
            out_specs=pl.BlockSpec(memory_space=pl.ANY),
            scratch_shapes=[pltpu.VMEM((MOE_RMAX, d), F32), pltpu.VMEM((PIECES[0], d), F32),
                            pltpu.SemaphoreType.DMA(())]),
        out_shape=jax.ShapeDtypeStruct((nbm * MOE_BM, d), F32),
        compiler_params=_cparams(("arbitrary",)),
        name="moe_dispatch",
    )(tab, tail, nb, n2, pos)

    row_map = lambda b, be, fi, nbr: (jnp.minimum(b, nbr[0] - 1), 0)
    out_map = lambda b, be, fi, nbr: (b, 0)
    w_map = lambda b, be, fi, nbr: (be[b], 0, 0)
    yb = pl.pallas_call(
        _expert_kernel,
        grid_spec=pltpu.PrefetchScalarGridSpec(
            num_scalar_prefetch=3, grid=(nbm,),
            in_specs=[pl.BlockSpec((MOE_BM, d), row_map),
                      pl.BlockSpec((1, d, d), w_map), pl.BlockSpec((1, 1, d), w_map),
                      pl.BlockSpec((1, d, d), w_map), pl.BlockSpec((1, 1, d), w_map),
                      pl.BlockSpec((1, d, d), w_map), pl.BlockSpec((1, 1, d), w_map)],
            out_specs=pl.BlockSpec((MOE_BM, d), out_map),
            scratch_shapes=[pltpu.VMEM((d, d), BF16)] * 3),
        out_shape=jax.ShapeDtypeStruct((nbm * MOE_BM, d), F32),
        compiler_params=_cparams(("arbitrary",)),
        name="moe_experts",
    )(blk_e, first, nb, xb, p['w_gate'], p['b_gate'].reshape(ne, 1, d), p['w_up'], p['b_up'].reshape(ne, 1, d),
      p['w_down'], p['b_down'].reshape(ne, 1, d))

    return pl.pallas_call(
        _combine_kernel,
        grid_spec=pltpu.PrefetchScalarGridSpec(
            num_scalar_prefetch=1, grid=(nt,),
            in_specs=[pl.BlockSpec((TM, TOP_K), lambda i, *_: (i, 0)),
                      pl.BlockSpec((TM, TOP_K), lambda i, *_: (i, 0)),
                      pl.BlockSpec((TM, d), lambda i, *_: (i, 0)),
                      pl.BlockSpec((1, 1, 6 * d), lambda i, *_: (layer * 16 + i // (seg // TM), 0, 0)),
                      pl.BlockSpec(memory_space=pl.ANY)],
            out_specs=pl.BlockSpec((TM, d), lambda i, *_: (i, 0)),
            scratch_shapes=[pltpu.VMEM((MOE_RMAX, d), F32), pltpu.SemaphoreType.DMA(())]),
        out_shape=jax.ShapeDtypeStruct((n_rows, d), F32),
        compiler_params=_cparams(("arbitrary",)),
        name="moe_combine",
    )(tab, pos.T, gate.T, h, mod3, yb)


def kernel(x, c, ctx, c_ctx, w_ada, b_ada, g_norm1, g_norm2, sg_w_in, sg_g_v, sg_w_s, sg_b_s, sg_w_out,
           ssm_w_in, ssm_lam_re, ssm_lam_im, ssm_log_dt, ssm_b_re, ssm_b_im, ssm_c_re, ssm_c_im, ssm_d, ssm_w_glu,
           mla_w_in, mla_g_q, mla_g_kv, mla_w_uq, mla_w_ukv, mla_g_qn, mla_g_kn, mla_w_out,
           moe_w_router, moe_b_router, moe_w_gate, moe_b_gate, moe_w_up, moe_b_up, moe_w_down, moe_b_down):
    nb, l_lat, d = x.shape
    l_ctx = ctx.shape[1]
    depth = w_ada.shape[0]
    n_lat, n_ctx = nb * l_lat, nb * l_ctx
    seg = l_lat
    assert n_ctx <= seg and seg % TM == 0 and n_ctx % TM == 0
    mod3 = modulation(c, c_ctx, w_ada, b_ada)
    h = jnp.concatenate([x.reshape(n_lat, d), ctx.reshape(n_ctx, d)], axis=0)
    for i in range(depth):
        mixer, slot = i % N_MIXERS, i // N_MIXERS
        ctx_out = i < depth - 1
        n_rows = n_lat + n_ctx if ctx_out else n_lat
        if mixer == 0:
            hm = gmlp_layer(h, mod3, i, g_norm1[i], sg_w_in[slot], sg_g_v[slot], sg_w_s[slot], sg_b_s[slot],
                            sg_w_out[slot], n_rows, seg)
        elif mixer == 1:
            p = dict(w_in=ssm_w_in[slot], lam_re=ssm_lam_re[slot], lam_im=ssm_lam_im[slot], log_dt=ssm_log_dt[slot],
                     b_re=ssm_b_re[slot], b_im=ssm_b_im[slot], c_re=ssm_c_re[slot], c_im=ssm_c_im[slot],
                     d=ssm_d[slot], w_glu=ssm_w_glu[slot])
            hm = s5_layer(h, mod3, i, g_norm1[i], p, n_lat, n_ctx, nb, seg)[:n_rows]
        else:
            p = dict(w_in=mla_w_in[slot], g_q=mla_g_q[slot], g_kv=mla_g_kv[slot], w_uq=mla_w_uq[slot],
                     w_ukv=mla_w_ukv[slot], g_qn=mla_g_qn[slot], g_kn=mla_g_kn[slot], w_out=mla_w_out[slot])
            hm = mla_layer(h, mod3, i, g_norm1[i], p, n_lat, n_ctx, nb, seg)[:n_rows]
        pm = dict(w_router=moe_w_router[i], b_router=moe_b_router[i], w_gate=moe_w_gate[i], b_gate=moe_b_gate[i],
                  w_up=moe_w_up[i], b_up=moe_b_up[i], w_down=moe_w_down[i], b_down=moe_b_down[i])
        h = moe_layer(hm, mod3, i, g_norm2[i], pm, n_rows, seg)
    return h[:n_lat].reshape(nb, l_lat, d)
```

```python
import functools
import math

import jax
import jax.numpy as jnp
import numpy as np
from jax import lax
from jax.experimental import pallas as pl
from jax.experimental.pallas import tpu as pltpu

F32 = jnp.float32
BF16 = jnp.bfloat16
I32 = jnp.int32
EPS = 1e-6

N_MIXERS = 3
GRID_W = 64
CHUNK = 128
SG_GROUPS = 8
S5_GROUP_CH = 16
S5_STATE = 64
MLA_HEADS = 8
MLA_NOPE = 128
MLA_ROPE = 64
MLA_QK = MLA_NOPE + MLA_ROPE
MLA_V = 128
MLA_Q_RANK = 384
MLA_KV_RANK = 256
ROPE_THETA = 10000.0
N_EXPERTS = 32
TOP_K = 4
SWIGLU_LIMIT = 7.0
SWIGLU_ALPHA = 1.702

LANES = 128
SUBLANES = 8
VMEM_LIMIT = 56 * 1024 * 1024

TM = 256
S5_LC = 16
MOE_BM = 256
MOE_RMAX = TM * TOP_K + N_EXPERTS * SUBLANES
PIECES = (256, 128, 64, 32, 16, 8)


def _cparams(sem):
    return pltpu.CompilerParams(dimension_semantics=sem, vmem_limit_bytes=VMEM_LIMIT)


def _rms(x, g):
    ms = jnp.mean(x * x, axis=-1, keepdims=True)
    return x * lax.rsqrt(ms + EPS) * g


def _adaln(x, g, shift, scale):
    return _rms(x, g) * (1.0 + scale) + shift


def _gelu(x):
    return 0.5 * x * (1.0 + lax.erf(x * (1.0 / math.sqrt(2.0))))


def _sigmoid(x):
    return 1.0 / (1.0 + jnp.exp(-x))


def _mod_spec(layer, d, tiles_per_seg):
    return pl.BlockSpec((1, 1, 6 * d), lambda i: (layer * 16 + i // tiles_per_seg, 0, 0))


def _mod_kernel(s_ref, w_ref, b_ref, o_ref):
    s = s_ref[...]
    s = s * _sigmoid(s)
    o_ref[0] = jnp.dot(s, w_ref[0], preferred_element_type=F32, precision=lax.Precision.HIGHEST) + b_ref[0]


def modulation(c, c_ctx, w_ada, b_ada):
    depth, d, d6 = w_ada.shape
    nb = c.shape[0]
    s = jnp.zeros((16, d), F32).at[:nb].set(c).at[nb].set(c_ctx)
    tn = 1024
    out = pl.pallas_call(
        _mod_kernel,
        grid=(depth, d6 // tn),
        in_specs=[pl.BlockSpec((16, d), lambda l, j: (0, 0)),
                  pl.BlockSpec((1, d, tn), lambda l, j: (l, 0, j)),
                  pl.BlockSpec((1, 1, tn), lambda l, j: (l, 0, j))],
        out_specs=pl.BlockSpec((1, 16, tn), lambda l, j: (l, 0, j)),
        out_shape=jax.ShapeDtypeStruct((depth, 16, d6), F32),
        compiler_params=_cparams(("arbitrary", "arbitrary")),
        name="adaln_modulation",
    )(s, w_ada, b_ada.reshape(depth, 1, d6))
    return out.reshape(depth * 16, 1, d6)


def _gmlp_kernel(h_ref, mod_ref, g1_ref, win_ref, gv_ref, ws_ref, bs_ref, wout_ref, o_ref, gated_ref):
    d = h_ref.shape[1]
    tm = h_ref.shape[0]
    h = h_ref[...]
    mod = mod_ref[0]
    n = _adaln(h, g1_ref[...], mod[:, 0:d], mod[:, d:2 * d]).astype(BF16)
    z = _gelu(jnp.dot(n, win_ref[...], preferred_element_type=F32))
    u = z[:, :d]
    v = _rms(z[:, d:], gv_ref[...]).astype(BF16)
    gd = d // SG_GROUPS
    for j in range(tm // CHUNK):
        rows = slice(j * CHUNK, (j + 1) * CHUNK)
        for g in range(SG_GROUPS):
            cols = slice(g * gd, (g + 1) * gd)
            sv = jnp.dot(ws_ref[g], v[rows, cols], preferred_element_type=F32) + bs_ref[g]
            gated_ref[rows, cols] = (u[rows, cols] * sv).astype(BF16)
    m = jnp.dot(gated_ref[...], wout_ref[...], preferred_element_type=F32)
    o_ref[...] = h + mod[:, 2 * d:3 * d] * m


def gmlp_layer(h, mod3, layer, g1, w_in, g_v, w_s, b_s, w_out, n_rows, seg):
    d = h.shape[1]
    gd = d // SG_GROUPS
    bs_b = jnp.broadcast_to(b_s[:, :, None], (SG_GROUPS, CHUNK, gd)).astype(F32)
    full = lambda *shape: pl.BlockSpec(shape, lambda i: (0,) * len(shape))
    return pl.pallas_call(
        _gmlp_kernel,
        grid=(n_rows // TM,),
        in_specs=[pl.BlockSpec((TM, d), lambda i: (i, 0)),
                  _mod_spec(layer, d, seg // TM),
                  full(1, d), full(d, 2 * d), full(1, d),
                  full(SG_GROUPS, CHUNK, CHUNK), full(SG_GROUPS, CHUNK, gd), full(d, d)],
        out_specs=pl.BlockSpec((TM, d), lambda i: (i, 0)),
        out_shape=jax.ShapeDtypeStruct((n_rows, d), F32),
        scratch_shapes=[pltpu.VMEM((TM, d), BF16)],
        compiler_params=_cparams(("arbitrary",)),
        name="gmlp_mixer",
    )(h, mod3, g1.reshape(1, d), w_in.astype(BF16), g_v.reshape(1, d), w_s.astype(BF16), bs_b,
      w_out.astype(BF16))


def _ln_proj_kernel(h_ref, mod_ref, g1_ref, w_ref, o_ref):
    d = h_ref.shape[1]
    mod = mod_ref[0]
    n = _adaln(h_ref[...], g1_ref[...], mod[:, 0:d], mod[:, d:2 * d]).astype(BF16)
    o_ref[...] = jnp.dot(n, w_ref[...], preferred_element_type=F32)


def ln_proj(h, mod3, layer, g1, w, seg):
    n_rows, d = h.shape
    dn = w.shape[1]
    return pl.pallas_call(
        _ln_proj_kernel,
        grid=(n_rows // TM,),
        in_specs=[pl.BlockSpec((TM, d), lambda i: (i, 0)),
                  _mod_spec(layer, d, seg // TM),
                  pl.BlockSpec((1, d), lambda i: (0, 0)),
                  pl.BlockSpec((d, dn), lambda i: (0, 0))],
        out_specs=pl.BlockSpec((TM, dn), lambda i: (i, 0)),
        out_shape=jax.ShapeDtypeStruct((n_rows, dn), F32),
        compiler_params=_cparams(("arbitrary",)),
        name="adaln_in_proj",
    )(h, mod3, g1.reshape(1, d), w.astype(BF16))


def _out_proj_kernel(y_ref, h_ref, mod_ref, w_ref, o_ref, *, glu):
    d = h_ref.shape[1]
    mod = mod_ref[0]
    z = jnp.dot(y_ref[...], w_ref[...], preferred_element_type=F32)
    if glu:
        z = z[:, :d] * _sigmoid(z[:, d:])
    o_ref[...] = h_ref[...] + mod[:, 2 * d:3 * d] * z


def out_proj(y, h, mod3, layer, w, seg, glu):
    n_rows, d = h.shape
    dk, dn = w.shape
    return pl.pallas_call(
        functools.partial(_out_proj_kernel, glu=glu),
        grid=(n_rows // TM,),
        in_specs=[pl.BlockSpec((TM, dk), lambda i: (i, 0)),
                  pl.BlockSpec((TM, d), lambda i: (i, 0)),
                  _mod_spec(layer, d, seg // TM),
                  pl.BlockSpec((dk, dn), lambda i: (0, 0))],
        out_specs=pl.BlockSpec((TM, d), lambda i: (i, 0)),
        out_shape=jax.ShapeDtypeStruct((n_rows, d), F32),
        compiler_params=_cparams(("arbitrary",)),
        name="mixer_out_proj",
    )(y, h, mod3, w.astype(BF16))


def _s5_tables(lam_re, lam_im, log_dt, b_re, b_im, c_re, c_im, reverse):
    hp = lax.Precision.HIGHEST
    lc = S5_LC
    dt = jnp.exp(log_dt.astype(F32))[:, None]
    steps = jnp.arange(lc + 1, dtype=F32)[:, None, None]
    mag = jnp.exp(lam_re * dt * steps)
    pw_re = mag * jnp.cos(lam_im * dt * steps)
    pw_im = mag * jnp.sin(lam_im * dt * steps)
    a_re, a_im = pw_re[1], pw_im[1]
    den = lam_re * lam_re + lam_im * lam_im
    n_re = a_re - 1.0
    f_re = (n_re * lam_re + a_im * lam_im) / den
    f_im = (a_im * lam_re - n_re * lam_im) / den
    bb_re = f_re[..., None] * b_re - f_im[..., None] * b_im
    bb_im = f_re[..., None] * b_im + f_im[..., None] * b_re
    ca_re = c_re[None] * pw_re[:lc, :, None, :] - c_im[None] * pw_im[:lc, :, None, :]
    ca_im = c_re[None] * pw_im[:lc, :, None, :] + c_im[None] * pw_re[:lc, :, None, :]
    kd = (jnp.einsum('dgop,gpi->dgoi', ca_re, bb_re, precision=hp)
          - jnp.einsum('dgop,gpi->dgoi', ca_im, bb_im, precision=hp))
    t = np.arange(lc)
    lag = (t[:, None] - t[None, :]) if reverse else (t[None, :] - t[:, None])
    valid = jnp.asarray(lag >= 0)
    m = kd[np.clip(lag, 0, lc - 1)]
    m = jnp.where(valid[:, :, None, None, None], m, 0.0)
    m = m.transpose(2, 0, 4, 1, 3).reshape(-1, lc * S5_GROUP_CH, lc * S5_GROUP_CH)
    left = t if reverse else (lc - 1 - t)
    into = (lc - t) if reverse else (t + 1)
    bs_re = (pw_re[left][:, :, :, None] * bb_re[None] - pw_im[left][:, :, :, None] * bb_im[None])
    bs_im = (pw_re[left][:, :, :, None] * bb_im[None] + pw_im[left][:, :, :, None] * bb_re[None])
    to_bs = lambda x: x.transpose(1, 0, 3, 2).reshape(x.shape[1], lc * S5_GROUP_CH, S5_STATE)
    cs_re = c_re[None] * pw_re[into][:, :, None, :] - c_im[None] * pw_im[into][:, :, None, :]
    cs_im = c_re[None] * pw_im[into][:, :, None, :] + c_im[None] * pw_re[into][:, :, None, :]
    to_cs = lambda x: x.transpose(1, 3, 0, 2).reshape(x.shape[1], S5_STATE, lc * S5_GROUP_CH)
    pad_p = LANES - S5_STATE
    bs_re, bs_im = (jnp.pad(to_bs(x), ((0, 0), (0, 0), (0, pad_p))) for x in (bs_re, bs_im))
    cs_re, cs_imn = (jnp.pad(to_cs(x), ((0, 0), (0, pad_p), (0, 0))) for x in (cs_re, -cs_im))
    alc = jnp.pad(jnp.stack([pw_re[lc], pw_im[lc]], axis=1), ((0, 0), (0, 0), (0, pad_p)))
    return m, bs_re, bs_im, cs_re, cs_imn, alc


def _s5_scan_kernel(x_ref, m_ref, bs_ref, cs_ref, a_ref, d_ref, o_ref, s_ref, hp_ref, *, n_ctx_chunks):
    x = x_ref[0]
    xb = x.astype(BF16)
    n_chunks = x.shape[0] // SUBLANES
    for i in range(4):
        s_ref[i] = jnp.dot(xb, bs_ref[0, i], preferred_element_type=F32)
    a = a_ref[0]
    af_re, af_im, ar_re, ar_im = a[0:1], a[1:2], a[2:3], a[3:4]

    def step(i, carry):
        hf_re, hf_im, hr_re, hr_im = carry
        cf = pl.multiple_of(i * SUBLANES, SUBLANES)
        cr_idx = jnp.where(i < n_ctx_chunks, n_ctx_chunks - 1 - i, n_chunks - 1 + n_ctx_chunks - i)
        cr = pl.multiple_of(cr_idx * SUBLANES, SUBLANES)
        hp_ref[0, pl.ds(cf, SUBLANES), :] = hf_re
        hp_ref[1, pl.ds(cf, SUBLANES), :] = hf_im
        hp_ref[2, pl.ds(cr, SUBLANES), :] = hr_re
        hp_ref[3, pl.ds(cr, SUBLANES), :] = hr_im
        sf_re = s_ref[0, pl.ds(cf, SUBLANES), :]
        sf_im = s_ref[1, pl.ds(cf, SUBLANES), :]
        sr_re = s_ref[2, pl.ds(cr, SUBLANES), :]
        sr_im = s_ref[3, pl.ds(cr, SUBLANES), :]
        return (af_re * hf_re - af_im * hf_im + sf_re, af_re * hf_im + af_im * hf_re + sf_im,
                ar_re * hr_re - ar_im * hr_im + sr_re, ar_re * hr_im + ar_im * hr_re + sr_im)

    z = jnp.zeros((SUBLANES, LANES), F32)
    lax.fori_loop(0, n_chunks, step, (z, z, z, z))
    y = d_ref[0] * x + jnp.dot(xb, m_ref[0], preferred_element_type=F32)
    for i in range(4):
        y = y + jnp.dot(hp_ref[i].astype(BF16), cs_ref[0, i], preferred_element_type=F32)
    o_ref[0] = _gelu(y).astype(BF16)


def s5_layer(h, mod3, layer, g1, p, n_lat, n_ctx, nb, seg):
    d = h.shape[1]
    assert nb == SUBLANES
    lc, gc = S5_LC, S5_GROUP_CH
    ng = d // gc
    xs = ln_proj(h, mod3, layer, g1, p['w_in'], seg)
    l_lat, l_ctx = n_lat // nb, n_ctx // nb
    xl = xs[:n_lat].reshape(nb, l_lat // lc, lc, ng, gc)
    xc = xs[n_lat:].reshape(nb, l_ctx // lc, lc, ng, gc)
    x = jnp.concatenate([xc, xl], axis=1)
    n_chunks = x.shape[1]
    x = x.transpose(3, 1, 0, 2, 4).reshape(ng, n_chunks * nb, lc * gc)
    tabs = [_s5_tables(p['lam_re'][k], p['lam_im'][k], p['log_dt'][k], p['b_re'][k], p['b_im'][k],
                       p['c_re'][k], p['c_im'][k], reverse=bool(k)) for k in range(2)]
    m = (tabs[0][0] + tabs[1][0]).astype(BF16)
    bs = jnp.stack([tabs[0][1], tabs[0][2], tabs[1][1], tabs[1][2]], axis=1).astype(BF16)
    cs = jnp.stack([tabs[0][3], tabs[0][4], tabs[1][3], tabs[1][4]], axis=1).astype(BF16)
    alc = jnp.concatenate([tabs[0][5], tabs[1][5], jnp.zeros((ng, 4, LANES), F32)], axis=1)
    dsk = jnp.tile(p['d'].reshape(ng, 1, gc), (1, lc, 1)).reshape(ng, 1, lc * gc)
    rows, w = n_chunks * nb, lc * gc
    y = pl.pallas_call(
        functools.partial(_s5_scan_kernel, n_ctx_chunks=l_ctx // lc),
        grid=(ng,),
        in_specs=[pl.BlockSpec((1, rows, w), lambda g: (g, 0, 0)),
                  pl.BlockSpec((1, w, w), lambda g: (g, 0, 0)),
                  pl.BlockSpec((1, 4, w, LANES), lambda g: (g, 0, 0, 0)),
                  pl.BlockSpec((1, 4, LANES, w), lambda g: (g, 0, 0, 0)),
                  pl.BlockSpec((1, SUBLANES, LANES), lambda g: (g, 0, 0)),
                  pl.BlockSpec((1, 1, w), lambda g: (g, 0, 0))],
        out_specs=pl.BlockSpec((1, rows, w), lambda g: (g, 0, 0)),
        out_shape=jax.ShapeDtypeStruct((ng, rows, w), BF16),
        scratch_shapes=[pltpu.VMEM((4, rows, LANES), F32), pltpu.VMEM((4, rows, LANES), F32)],
        compiler_params=_cparams(("arbitrary",)),
        name="s5_chunked_scan",
    )(x, m, bs, cs, alc, dsk)
    y = y.reshape(ng, n_chunks, nb, lc, gc).transpose(2, 1, 3, 0, 4)
    yc = y[:, :l_ctx // lc].reshape(n_ctx, d)
    yl = y[:, l_ctx // lc:].reshape(n_lat, d)
    return out_proj(jnp.concatenate([yl, yc], axis=0), h, mod3, layer, p['w_glu'], seg, glu=True)


def _mla_proj_kernel(h_ref, mod_ref, g1_ref, win_ref, gq_ref, gkv_ref, wuq_ref, wukv_ref,
                     gqn_ref, gkn_ref, cos_ref, sin_ref, q_ref, k_ref, v_ref):
    d = h_ref.shape[1]
    mod = mod_ref[0]
    n = _adaln(h_ref[...], g1_ref[...], mod[:, 0:d], mod[:, d:2 * d]).astype(BF16)
    z = jnp.dot(n, win_ref[...], preferred_element_type=F32)
    ql = _rms(z[:, :MLA_Q_RANK], gq_ref[...]).astype(BF16)
    r0 = MLA_Q_RANK + MLA_KV_RANK
    kvl = _rms(z[:, MLA_Q_RANK:r0], gkv_ref[...]).astype(BF16)
    pe, pe_sw = z[:, r0:r0 + LANES], z[:, r0 + LANES:r0 + 2 * LANES]
    qa = jnp.dot(ql, wuq_ref[...], preferred_element_type=F32)
    kv = jnp.dot(kvl, wukv_ref[...], preferred_element_type=F32)
    cos, sin = cos_ref[...], sin_ref[...]
    gqn, gkn = gqn_ref[...], gkn_ref[...]
    kr = (pe * gkn[1:2]) * cos + (pe_sw * gkn[2:3]) * sin
    pe_ss = jnp.sum(pe * pe, axis=-1, keepdims=True)
    hw = 2 * LANES
    qscale = 1.0 / math.sqrt(MLA_QK)
    for hd in range(MLA_HEADS):
        qn = qa[:, hd * hw:hd * hw + LANES]
        qr = qa[:, hd * hw + LANES:(hd + 1) * hw]
        qsw = qa[:, MLA_HEADS * hw + hd * LANES:MLA_HEADS * hw + (hd + 1) * LANES]
        ss = jnp.sum(qn * qn, axis=-1, keepdims=True) + jnp.sum(qr * qr, axis=-1, keepdims=True)
        rq = lax.rsqrt(ss * (1.0 / MLA_QK) + EPS) * qscale
        q_ref[:, hd * hw:hd * hw + LANES] = (qn * rq * gqn[0:1]).astype(BF16)
        q_ref[:, hd * hw + LANES:(hd + 1) * hw] = (rq * ((qr * gqn[1:2]) * cos + (qsw * gqn[2:3]) * sin)).astype(BF16)
        kn = kv[:, hd * LANES:(hd + 1) * LANES]
        rk = lax.rsqrt((jnp.sum(kn * kn, axis=-1, keepdims=True) + pe_ss) * (1.0 / MLA_QK) + EPS)
        k_ref[:, hd * hw:hd * hw + LANES] = (kn * rk * gkn[0:1]).astype(BF16)
        k_ref[:, hd * hw + LANES:(hd + 1) * hw] = (kr * rk).astype(BF16)
    v_ref[...] = kv[:, MLA_HEADS * LANES:].astype(BF16)


def _attn_kernel(q_ref, kc_ref, vc_ref, *rest, with_latent):
    if with_latent:
        kl_ref, vl_ref, o_ref = rest
    else:
        (o_ref,) = rest
    q = q_ref[...]
    nt = (((1,), (1,)), ((), ()))
    s1 = lax.dot_general(q, kc_ref[...], nt, preferred_element_type=F32)
    mx = jnp.max(s1, axis=-1, keepdims=True)
    if with_latent:
        s2 = lax.dot_general(q, kl_ref[...], nt, preferred_element_type=F32)
        mx = jnp.maximum(mx, jnp.max(s2, axis=-1, keepdims=True))
    p1 = jnp.exp(s1 - mx)
    den = jnp.sum(p1, axis=-1, keepdims=True)
    o = jnp.dot(p1.astype(BF16), vc_ref[...], preferred_element_type=F32)
    if with_latent:
        p2 = jnp.exp(s2 - mx)
        den = den + jnp.sum(p2, axis=-1, keepdims=True)
        o = o + jnp.dot(p2.astype(BF16), vl_ref[...], preferred_element_type=F32)
    o_ref[...] = (o / den).astype(BF16)


def _rope_tables(l_lat, n_rows_id):
    rows = l_lat // GRID_W
    row = jnp.repeat(jnp.arange(rows), GRID_W).astype(F32)
    col = jnp.tile(jnp.arange(GRID_W), rows).astype(F32)
    quarter = MLA_ROPE // 4
    inv_freq = ROPE_THETA ** (-jnp.arange(quarter, dtype=F32) / quarter)
    ang_r, ang_c = row[:, None] * inv_freq, col[:, None] * inv_freq
    cos = jnp.concatenate([jnp.cos(ang_r), jnp.cos(ang_r), jnp.cos(ang_c), jnp.cos(ang_c)], axis=1)
    sin = jnp.concatenate([-jnp.sin(ang_r), jnp.sin(ang_r), -jnp.sin(ang_c), jnp.sin(ang_c)], axis=1)
    pad = ((0, n_rows_id), (0, LANES - MLA_ROPE))
    cos = jnp.pad(cos, pad).at[l_lat:, :MLA_ROPE].set(1.0)
    return cos, jnp.pad(sin, pad)


def mla_layer(h, mod3, layer, g1, p, n_lat, n_ctx, nb, seg):
    n_rows, d = h.shape
    hw = 2 * LANES
    nh = MLA_HEADS
    l_lat, l_ctx = n_lat // nb, n_ctx // nb
    r0 = MLA_Q_RANK + MLA_KV_RANK
    swap = np.arange(MLA_ROPE)
    swap = np.where(swap % 32 < 16, swap + 16, swap - 16)
    lane_pad = lambda x: jnp.pad(x, ((0, 0),) * (x.ndim - 1) + ((0, LANES - x.shape[-1]),))
    w_pe = p['w_in'][:, r0:]
    w_in_ext = jnp.concatenate([p['w_in'][:, :r0], lane_pad(w_pe), lane_pad(w_pe[:, swap])], axis=1).astype(BF16)
    wq = p['w_uq'].reshape(MLA_Q_RANK, nh, MLA_QK)
    wq_main = jnp.pad(wq, ((0, 0), (0, 0), (0, hw - MLA_QK))).reshape(MLA_Q_RANK, nh * hw)
    wq_sw = lane_pad(wq[:, :, MLA_NOPE:][:, :, swap]).reshape(MLA_Q_RANK, nh * LANES)
    w_uq_ext = jnp.concatenate([wq_main, wq_sw], axis=1).astype(BF16)
    wkv = p['w_ukv'].reshape(MLA_KV_RANK, nh, 2, LANES).transpose(0, 2, 1, 3).reshape(MLA_KV_RANK, 2 * nh * LANES)
    gains = lambda g: jnp.zeros((SUBLANES, LANES), F32).at[0].set(g[:MLA_NOPE]).at[1, :MLA_ROPE].set(
        g[MLA_NOPE:]).at[2, :MLA_ROPE].set(g[MLA_NOPE:][swap])
    cos, sin = _rope_tables(l_lat, seg)
    t_lat = l_lat // TM
    tps = seg // TM
    rope_map = lambda i: (jnp.where(i < (n_lat // TM), i % t_lat, t_lat + i % tps), 0)
    full = lambda *shape: pl.BlockSpec(shape, lambda i: (0,) * len(shape))
    q, k, v = pl.pallas_call(
        _mla_proj_kernel,
        grid=(n_rows // TM,),
        in_specs=[pl.BlockSpec((TM, d), lambda i: (i, 0)), _mod_spec(layer, d, tps), full(1, d),
                  full(d, r0 + 2 * LANES), full(1, MLA_Q_RANK), full(1, MLA_KV_RANK),
                  full(MLA_Q_RANK, nh * (hw + LANES)), full(MLA_KV_RANK, 2 * nh * LANES),
                  full(SUBLANES, LANES), full(SUBLANES, LANES),
                  pl.BlockSpec((TM, LANES), rope_map), pl.BlockSpec((TM, LANES), rope_map)],
        out_specs=[pl.BlockSpec((TM, nh * hw), lambda i: (i, 0)), pl.BlockSpec((TM, nh * hw), lambda i: (i, 0)),
                   pl.BlockSpec((TM, nh * LANES), lambda i: (i, 0))],
        out_shape=[jax.ShapeDtypeStruct((n_rows, nh * hw), BF16), jax.ShapeDtypeStruct((n_rows, nh * hw), BF16),
                   jax.ShapeDtypeStruct((n_rows, nh * LANES), BF16)],
        compiler_params=_cparams(("arbitrary",)),
        name="mla_projection",
    )(h, mod3, g1.reshape(1, d), w_in_ext, p['g_q'].reshape(1, -1), p['g_kv'].reshape(1, -1), w_uq_ext,
      wkv.astype(BF16), gains(p['g_qn']), gains(p['g_kn']), cos, sin)

    tq = min(512, l_lat)
    cb = n_lat // l_ctx
    o_lat = pl.pallas_call(
        functools.partial(_attn_kernel, with_latent=True),
        grid=(nb, nh, l_lat // tq),
        in_specs=[pl.BlockSpec((tq, hw), lambda b, hd, i: (b * (l_lat // tq) + i, hd)),
                  pl.BlockSpec((l_ctx, hw), lambda b, hd, i: (cb + b, hd)),
                  pl.BlockSpec((l_ctx, LANES), lambda b, hd, i: (cb + b, hd)),
                  pl.BlockSpec((l_lat, hw), lambda b, hd, i: (b, hd)),
                  pl.BlockSpec((l_lat, LANES), lambda b, hd, i: (b, hd))],
        out_specs=pl.BlockSpec((tq, LANES), lambda b, hd, i: (b * (l_lat // tq) + i, hd)),
        out_shape=jax.ShapeDtypeStruct((n_lat, nh * LANES), BF16),
        compiler_params=_cparams(("arbitrary", "arbitrary", "arbitrary")),
        name="mla_attention_latent",
    )(q, k, v, k, v)
    o_ctx = pl.pallas_call(
        functools.partial(_attn_kernel, with_latent=False),
        grid=(nb, nh),
        in_specs=[pl.BlockSpec((l_ctx, hw), lambda b, hd: (cb + b, hd)),
                  pl.BlockSpec((l_ctx, hw), lambda b, hd: (cb + b, hd)),
                  pl.BlockSpec((l_ctx, LANES), lambda b, hd: (cb + b, hd))],
        out_specs=pl.BlockSpec((l_ctx, LANES), lambda b, hd: (b, hd)),
        out_shape=jax.ShapeDtypeStruct((n_ctx, nh * LANES), BF16),
        compiler_params=_cparams(("arbitrary", "arbitrary")),
        name="mla_attention_context",
    )(q, k, v)
    return out_proj(jnp.concatenate([o_lat, o_ctx], axis=0), h, mod3, layer, p['w_out'], seg, glu=False)


def _route_kernel(h_ref, mod_ref, g2_ref, wrt_ref, br_ref, n2_ref, pos_ref, gate_ref, cnt_ref):
    d = h_ref.shape[1]
    tm = h_ref.shape[0]
    ne = wrt_ref.shape[0]
    mod = mod_ref[0]
    n2 = _adaln(h_ref[...], g2_ref[...], mod[:, 3 * d:4 * d], mod[:, 4 * d:5 * d])
    n2_ref[...] = n2.astype(BF16)
    logits = lax.dot_general(wrt_ref[...], n2, (((1,), (1,)), ((), ())), preferred_element_type=F32,
                             precision=lax.Precision.HIGHEST) + br_ref[...]
    eidx = lax.broadcasted_iota(I32, (ne, tm), 0).astype(F32)
    work = logits
    vals, idxs = [], []
    for _ in range(TOP_K):
        mx = jnp.max(work, axis=0, keepdims=True)
        idx = jnp.min(jnp.where(work == mx, eidx, float(ne)), axis=0, keepdims=True)
        vals.append(mx)
        idxs.append(idx)
        work = jnp.where(eidx == idx, -jnp.inf, work)
    ex = [jnp.exp(v - vals[0]) for v in vals]
    den = ex[0] + ex[1] + ex[2] + ex[3]
    sel = (eidx == idxs[0]) | (eidx == idxs[1]) | (eidx == idxs[2]) | (eidx == idxs[3])
    onehot = jnp.where(sel, 1.0, 0.0).astype(BF16)
    r = lax.broadcasted_iota(I32, (tm, tm), 0)
    c = lax.broadcasted_iota(I32, (tm, tm), 1)
    before = jnp.where(r < c, 1.0, 0.0).astype(BF16)
    rank_all = jnp.dot(onehot, before, preferred_element_type=F32)
    cnt = jnp.dot(onehot, jnp.ones((tm, tm), BF16), preferred_element_type=F32)
    cnt8 = jnp.floor((cnt + (SUBLANES - 1)) * (1.0 / SUBLANES)) * SUBLANES
    er = lax.broadcasted_iota(I32, (ne, LANES), 0)
    ec = lax.broadcasted_iota(I32, (ne, LANES), 1)
    lower = jnp.where(ec < er, 1.0, 0.0).astype(BF16)
    cnt8_p = jnp.concatenate([cnt8, jnp.zeros((LANES - ne, tm), F32)], axis=0).astype(BF16)
    off = jnp.dot(lower, cnt8_p, preferred_element_type=F32)
    base = rank_all + off
    for k in range(TOP_K):
        pos = jnp.sum(jnp.where(eidx == idxs[k], base, 0.0), axis=0, keepdims=True)
        pos_ref[k:k + 1, :] = pos.astype(I32)
        gate_ref[k:k + 1, :] = ex[k] / den
    cnt_ref[0] = cnt8[:, :LANES].astype(I32)


def moe_route(h, mod3, layer, g2, w_router, b_router, n_rows, seg):
    d = h.shape[1]
    ne = w_router.shape[1]
    nt = n_rows // TM
    return pl.pallas_call(
        _route_kernel,
        grid=(nt,),
        in_specs=[pl.BlockSpec((TM, d), lambda i: (i, 0)), _mod_spec(layer, d, seg // TM),
                  pl.BlockSpec((1, d), lambda i: (0, 0)), pl.BlockSpec((ne, d), lambda i: (0, 0)),
                  pl.BlockSpec((ne, 1), lambda i: (0, 0))],
        out_specs=[pl.BlockSpec((TM, d), lambda i: (i, 0)), pl.BlockSpec((TOP_K, TM), lambda i: (0, i)),
                   pl.BlockSpec((TOP_K, TM), lambda i: (0, i)), pl.BlockSpec((1, ne, LANES), lambda i: (i, 0, 0))],
        out_shape=[jax.ShapeDtypeStruct((n_rows, d), BF16), jax.ShapeDtypeStruct((TOP_K, n_rows), I32),
                   jax.ShapeDtypeStruct((TOP_K, n_rows), F32), jax.ShapeDtypeStruct((nt, ne, LANES), I32)],
        compiler_params=_cparams(("arbitrary",)),
        name="moe_route",
    )(h, mod3, g2.reshape(1, d), w_router.T, b_router.reshape(ne, 1))


def _chunk_copies(tab_ref, tile, e, vm_ref, hbm_ref, sem, to_hbm):
    ne = N_EXPERTS
    stride = tab_ref.shape[0] // 3
    off = tab_ref[tile * ne + e]
    n8 = tab_ref[stride + tile * ne + e]
    dst = tab_ref[2 * stride + tile * ne + e]
    out = []
    for size in PIECES:
        u = size // SUBLANES
        done = (n8 // (2 * u)) * (2 * u) * SUBLANES
        lo = pl.multiple_of(off + done, SUBLANES)
        hi = pl.multiple_of(dst + done, SUBLANES)
        v, hb = vm_ref.at[pl.ds(lo, size), :], hbm_ref.at[pl.ds(hi, size), :]
        cp = pltpu.make_async_copy(v, hb, sem) if to_hbm else pltpu.make_async_copy(hb, v, sem)
        out.append(((n8 // u) % 2 == 1, cp))
    return out


def _for_each_copy(tab_ref, tile, vm_ref, hbm_ref, sem, to_hbm, action):
    small = PIECES.index(32)

    def body(e, carry):
        copies = _chunk_copies(tab_ref, tile, e, vm_ref, hbm_ref, sem, to_hbm)

        def run(items):
            for cond, cp in items:
                @pl.when(cond)
                def _():
                    action(cp)
        run(copies[small:])

        @pl.when(tab_ref[tab_ref.shape[0] // 3 + tile * N_EXPERTS + e] >= 2 * PIECES[small] // SUBLANES)
        def _():
            run(copies[:small])
        return carry
    lax.fori_loop(0, N_EXPERTS, body, 0)


def _wait_rows(n_rows, vm_ref, hbm_ref, sem, to_hbm):
    n8 = n_rows // SUBLANES
    size = SUBLANES
    while size * 2 <= vm_ref.shape[0]:
        size *= 2
    while size >= SUBLANES:
        v, hb = vm_ref.at[pl.ds(0, size), :], hbm_ref.at[pl.ds(0, size), :]
        cp = pltpu.make_async_copy(v, hb, sem) if to_hbm else pltpu.make_async_copy(hb, v, sem)

        @pl.when((n8 // (size // SUBLANES)) % 2 == 1)
        def _():
            cp.wait()
        size //= 2


def _dispatch_kernel(tab_ref, rows_ref, tail_ref, nb_ref, n2_ref, pos_ref, xb_ref, sorted_ref, zero_ref, sem, zsem):
    i = pl.program_id(0)
    last = pl.num_programs(0) - 1
    slot = i % 2
    cur = sorted_ref.at[slot]
    tm = n2_ref.shape[0]
    rmax = sorted_ref.shape[1]
    ridx = lax.broadcasted_iota(I32, (rmax, tm), 0)
    hit = ridx == pos_ref[0:1, :]
    for k in range(1, TOP_K):
        hit = hit | (ridx == pos_ref[k:k + 1, :])
    perm = jnp.where(hit, 1.0, 0.0).astype(BF16)
    cur[...] = jnp.dot(perm, n2_ref[...], preferred_element_type=F32)
    _for_each_copy(tab_ref, i, cur, xb_ref, sem.at[slot], True, lambda cp: cp.start())

    @pl.when(i > 0)
    def _():
        _wait_rows(rows_ref[i - 1], sorted_ref.at[1 - slot], xb_ref, sem.at[1 - slot], True)

    @pl.when(i == last)
    def _():
        _wait_rows(rows_ref[i], cur, xb_ref, sem.at[slot], True)
        zero_ref[...] = jnp.zeros_like(zero_ref)
        _for_each_copy(tail_ref, 0, zero_ref, xb_ref, zsem, True, lambda cp: cp.start())
        _for_each_copy(tail_ref, 0, zero_ref, xb_ref, zsem, True, lambda cp: cp.wait())

        def spare(b, carry):
            cp = pltpu.make_async_copy(zero_ref, xb_ref.at[pl.ds(pl.multiple_of(b * MOE_BM, MOE_BM), MOE_BM), :], zsem)
            cp.start()
            cp.wait()
            return carry
        lax.fori_loop(nb_ref[0], xb_ref.shape[0] // MOE_BM, spare, 0)


def _expert_kernel(be_ref, first_ref, nb_ref, x_ref, wg_ref, bg_ref, wu_ref, bu_ref, wd_ref, bd_ref, y_ref,
                   wg_s, wu_s, wd_s):
    b = pl.program_id(0)

    @pl.when(b < nb_ref[0])
    def _():
        @pl.when(first_ref[b] == 1)
        def _():
            wg_s[...] = wg_ref[0, 0].astype(BF16)
            wu_s[...] = wu_ref[0, 0].astype(BF16)
            wd_s[...] = wd_ref[0, 0].astype(BF16)

        x = x_ref[...].astype(BF16)
        g = jnp.minimum(jnp.dot(x, wg_s[...], preferred_element_type=F32) + bg_ref[0], SWIGLU_LIMIT)
        u = jnp.clip(jnp.dot(x, wu_s[...], preferred_element_type=F32) + bu_ref[0], -SWIGLU_LIMIT, SWIGLU_LIMIT)
        a = (g * _sigmoid(SWIGLU_ALPHA * g) * (u + 1.0)).astype(BF16)
        y_ref[...] = jnp.dot(a, wd_s[...], preferred_element_type=F32) + bd_ref[0]

    @pl.when(b >= nb_ref[0])
    def _():
        y_ref[...] = jnp.zeros_like(y_ref)


def _combine_kernel(tab_ref, rows_ref, pos_ref, gate_ref, h_ref, mod_ref, yb_ref, o_ref, ys_ref, sem):
    i = pl.program_id(0)
    slot = i % 2
    d = h_ref.shape[1]
    tm = h_ref.shape[0]
    rmax = ys_ref.shape[1]

    @pl.when(i == 0)
    def _():
        ys_ref[...] = jnp.zeros_like(ys_ref)
        _for_each_copy(tab_ref, 0, ys_ref.at[0], yb_ref, sem.at[0], False, lambda cp: cp.start())

    @pl.when(i + 1 < pl.num_programs(0))
    def _():
        _for_each_copy(tab_ref, i + 1, ys_ref.at[1 - slot], yb_ref, sem.at[1 - slot], False, lambda cp: cp.start())

    cidx = lax.broadcasted_iota(I32, (tm, rmax), 1)
    wt = jnp.where(cidx == pos_ref[:, 0:1], gate_ref[:, 0:1], 0.0)
    for k in range(1, TOP_K):
        wt = wt + jnp.where(cidx == pos_ref[:, k:k + 1], gate_ref[:, k:k + 1], 0.0)
    _wait_rows(rows_ref[i], ys_ref.at[slot], yb_ref, sem.at[slot], False)
    f = jnp.dot(wt.astype(BF16), ys_ref[slot].astype(BF16), preferred_element_type=F32)
    o_ref[...] = h_ref[...] + mod_ref[0][:, 5 * d:6 * d] * f


def _moe_plan(cnt8, n_blocks_max):
    nt, ne = cnt8.shape
    loc_off = jnp.cumsum(cnt8, axis=1) - cnt8
    used = jnp.sum(cnt8, axis=0)
    rows_e = (used + MOE_BM - 1) // MOE_BM * MOE_BM
    e_end = jnp.cumsum(rows_e)
    e_start = e_end - rows_e
    dst = e_start[None, :] + jnp.cumsum(cnt8, axis=0) - cnt8
    tab = jnp.stack([loc_off, cnt8 // SUBLANES, dst]).reshape(3 * nt * ne).astype(I32)
    tail = jnp.stack([jnp.zeros((ne,), I32), (rows_e - used) // SUBLANES, e_start + used]).reshape(3 * ne).astype(I32)
    nb = (e_end[-1] // MOE_BM).astype(I32)
    blk = jnp.arange(n_blocks_max, dtype=I32)
    blk_c = jnp.minimum(blk, nb - 1)
    blk_e = jnp.sum((e_end[None, :] <= (blk_c * MOE_BM)[:, None]).astype(I32), axis=1)
    blk_e = jnp.minimum(blk_e, ne - 1)
    first = (blk_c * MOE_BM == e_start[blk_e]).astype(I32)
    tile_rows = jnp.sum(cnt8, axis=1).astype(I32)
    return tab, tile_rows, tail, blk_e, first, nb.reshape(1)


def moe_layer(h, mod3, layer, g2, p, n_rows, seg):
    d = h.shape[1]
    ne = N_EXPERTS
    nt = n_rows // TM
    n2, pos, gate, cnt = moe_route(h, mod3, layer, g2, p['w_router'], p['b_router'], n_rows, seg)
    max_rows = n_rows * TOP_K + nt * ne * (SUBLANES - 1) + ne * (MOE_BM - 1)
    nbm = -(-max_rows // MOE_BM)
    tab, tile_rows, tail, blk_e, first, nb = _moe_plan(cnt[:, :, 0], nbm)

    xb = pl.pallas_call(
        _dispatch_kernel,
        grid_spec=pltpu.PrefetchScalarGridSpec(
            num_scalar_prefetch=4, grid=(nt,),
            in_specs=[pl.BlockSpec((TM, d), lambda i, *_: (i, 0)),
                      pl.BlockSpec((TOP_K, TM), lambda i, *_: (0, i))],
            out_specs=pl.BlockSpec(memory_space=pl.ANY),
            scratch_shapes=[pltpu.VMEM((2, MOE_RMAX, d), F32), pltpu.VMEM((PIECES[0], d), F32),
                            pltpu.SemaphoreType.DMA((2,)), pltpu.SemaphoreType.DMA(())]),
        out_shape=jax.ShapeDtypeStruct((nbm * MOE_BM, d), F32),
        compiler_params=_cparams(("arbitrary",)),
        name="moe_dispatch",
    )(tab, tile_rows, tail, nb, n2, pos)

    row_map = lambda b, be, fi, nbr: (jnp.minimum(b, nbr[0] - 1), 0)
    out_map = lambda b, be, fi, nbr: (b, 0)
    w_map = lambda b, be, fi, nbr: (layer, be[b], 0, 0)
    b_map = lambda b, be, fi, nbr: (layer * ne + be[b], 0, 0)
    yb = pl.pallas_call(
        _expert_kernel,
        grid_spec=pltpu.PrefetchScalarGridSpec(
            num_scalar_prefetch=3, grid=(nbm,),
            in_specs=[pl.BlockSpec((MOE_BM, d), row_map),
                      pl.BlockSpec((1, 1, d, d), w_map), pl.BlockSpec((1, 1, d), b_map),
                      pl.BlockSpec((1, 1, d, d), w_map), pl.BlockSpec((1, 1, d), b_map),
                      pl.BlockSpec((1, 1, d, d), w_map), pl.BlockSpec((1, 1, d), b_map)],
            out_specs=pl.BlockSpec((MOE_BM, d), out_map),
            scratch_shapes=[pltpu.VMEM((d, d), BF16)] * 3),
        out_shape=jax.ShapeDtypeStruct((nbm * MOE_BM, d), F32),
        compiler_params=_cparams(("arbitrary",)),
        name="moe_experts",
    )(blk_e, first, nb, xb, p['w_gate'], p['b_gate'].reshape(-1, 1, d), p['w_up'], p['b_up'].reshape(-1, 1, d),
      p['w_down'], p['b_down'].reshape(-1, 1, d))

    return pl.pallas_call(
        _combine_kernel,
        grid_spec=pltpu.PrefetchScalarGridSpec(
            num_scalar_prefetch=2, grid=(nt,),
            in_specs=[pl.BlockSpec((TM, TOP_K), lambda i, *_: (i, 0)),
                      pl.BlockSpec((TM, TOP_K), lambda i, *_: (i, 0)),
                      pl.BlockSpec((TM, d), lambda i, *_: (i, 0)),
                      pl.BlockSpec((1, 1, 6 * d), lambda i, *_: (layer * 16 + i // (seg // TM), 0, 0)),
                      pl.BlockSpec(memory_space=pl.ANY)],
            out_specs=pl.BlockSpec((TM, d), lambda i, *_: (i, 0)),
            scratch_shapes=[pltpu.VMEM((2, MOE_RMAX, d), F32), pltpu.SemaphoreType.DMA((2,))]),
        out_shape=jax.ShapeDtypeStruct((n_rows, d), F32),
        compiler_params=_cparams(("arbitrary",)),
        name="moe_combine",
    )(tab, tile_rows, pos.T, gate.T, h, mod3, yb)


def kernel(x, c, ctx, c_ctx, w_ada, b_ada, g_norm1, g_norm2, sg_w_in, sg_g_v, sg_w_s, sg_b_s, sg_w_out,
           ssm_w_in, ssm_lam_re, ssm_lam_im, ssm_log_dt, ssm_b_re, ssm_b_im, ssm_c_re, ssm_c_im, ssm_d, ssm_w_glu,
           mla_w_in, mla_g_q, mla_g_kv, mla_w_uq, mla_w_ukv, mla_g_qn, mla_g_kn, mla_w_out,
           moe_w_router, moe_b_router, moe_w_gate, moe_b_gate, moe_w_up, moe_b_up, moe_w_down, moe_b_down):
    nb, l_lat, d = x.shape
    l_ctx = ctx.shape[1]
    depth = w_ada.shape[0]
    n_lat, n_ctx = nb * l_lat, nb * l_ctx
    seg = l_lat
    assert n_ctx <= seg and seg % TM == 0 and n_ctx % TM == 0
    mod3 = modulation(c, c_ctx, w_ada, b_ada)
    h = jnp.concatenate([x.reshape(n_lat, d), ctx.reshape(n_ctx, d)], axis=0)
    for i in range(depth):
        mixer, slot = i % N_MIXERS, i // N_MIXERS
        ctx_out = i < depth - 1
        n_rows = n_lat + n_ctx if ctx_out else n_lat
        if mixer == 0:
            hm = gmlp_layer(h, mod3, i, g_norm1[i], sg_w_in[slot], sg_g_v[slot], sg_w_s[slot], sg_b_s[slot],
                            sg_w_out[slot], n_rows, seg)
        elif mixer == 1:
            p = dict(w_in=ssm_w_in[slot], lam_re=ssm_lam_re[slot], lam_im=ssm_lam_im[slot], log_dt=ssm_log_dt[slot],
                     b_re=ssm_b_re[slot], b_im=ssm_b_im[slot], c_re=ssm_c_re[slot], c_im=ssm_c_im[slot],
                     d=ssm_d[slot], w_glu=ssm_w_glu[slot])
            hm = s5_layer(h, mod3, i, g_norm1[i], p, n_lat, n_ctx, nb, seg)[:n_rows]
        else:
            p = dict(w_in=mla_w_in[slot], g_q=mla_g_q[slot], g_kv=mla_g_kv[slot], w_uq=mla_w_uq[slot],
                     w_ukv=mla_w_ukv[slot], g_qn=mla_g_qn[slot], g_kn=mla_g_kn[slot], w_out=mla_w_out[slot])
            hm = mla_layer(h, mod3, i, g_norm1[i], p, n_lat, n_ctx, nb, seg)[:n_rows]
        pm = dict(w_router=moe_w_router[i], b_router=moe_b_router[i], w_gate=moe_w_gate, b_gate=moe_b_gate,
                  w_up=moe_w_up, b_up=moe_b_up, w_down=moe_w_down, b_down=moe_b_down)
        h = moe_layer(hm, mod3, i, g_norm2[i], pm, n_rows, seg)
    return h[:n_lat].reshape(nb, l_lat, d)
```

```python
import functools
import math

import jax
import jax.numpy as jnp
import numpy as np
from jax import lax
from jax.experimental import pallas as pl
from jax.experimental.pallas import tpu as pltpu

F32 = jnp.float32
BF16 = jnp.bfloat16
I32 = jnp.int32
EPS = 1e-6

N_MIXERS = 3
GRID_W = 64
CHUNK = 128
SG_GROUPS = 8
S5_GROUP_CH = 16
S5_STATE = 64
MLA_HEADS = 8
MLA_NOPE = 128
MLA_ROPE = 64
MLA_QK = MLA_NOPE + MLA_ROPE
MLA_V = 128
MLA_Q_RANK = 384
MLA_KV_RANK = 256
ROPE_THETA = 10000.0
N_EXPERTS = 32
TOP_K = 4
SWIGLU_LIMIT = 7.0
SWIGLU_ALPHA = 1.702

LANES = 128
SUBLANES = 8
VMEM_LIMIT = 56 * 1024 * 1024

TM = 256
S5_TB = 32
ATTN_KB = 512
ATTN_QS = 256
MOE_BM = 256
MOE_RMAX = TM * TOP_K + N_EXPERTS * SUBLANES
PIECES = (256, 128, 64, 32, 16, 8)


def _cparams(sem):
    return pltpu.CompilerParams(dimension_semantics=sem, vmem_limit_bytes=VMEM_LIMIT)


def _rms(x, g):
    ms = jnp.mean(x * x, axis=-1, keepdims=True)
    return x * lax.rsqrt(ms + EPS) * g


def _adaln(x, g, shift, scale):
    return _rms(x, g) * (1.0 + scale) + shift


def _gelu(x):
    return 0.5 * x * (1.0 + lax.erf(x * (1.0 / math.sqrt(2.0))))


def _sigmoid(x):
    return 1.0 / (1.0 + jnp.exp(-x))


def _mod_spec(layer, d, tiles_per_seg):
    return pl.BlockSpec((1, 1, 6 * d), lambda i: (layer * 16 + i // tiles_per_seg, 0, 0))


def _mod_kernel(s_ref, w_ref, b_ref, o_ref):
    s = s_ref[...]
    s = s * _sigmoid(s)
    o_ref[0] = jnp.dot(s, w_ref[0], preferred_element_type=F32, precision=lax.Precision.HIGHEST) + b_ref[0]


def modulation(c, c_ctx, w_ada, b_ada):
    depth, d, d6 = w_ada.shape
    nb = c.shape[0]
    s = jnp.zeros((16, d), F32).at[:nb].set(c).at[nb].set(c_ctx)
    tn = 1024
    out = pl.pallas_call(
        _mod_kernel,
        grid=(depth, d6 // tn),
        in_specs=[pl.BlockSpec((16, d), lambda l, j: (0, 0)),
                  pl.BlockSpec((1, d, tn), lambda l, j: (l, 0, j)),
                  pl.BlockSpec((1, 1, tn), lambda l, j: (l, 0, j))],
        out_specs=pl.BlockSpec((1, 16, tn), lambda l, j: (l, 0, j)),
        out_shape=jax.ShapeDtypeStruct((depth, 16, d6), F32),
        compiler_params=_cparams(("arbitrary", "arbitrary")),
        name="adaln_modulation",
    )(s, w_ada, b_ada.reshape(depth, 1, d6))
    return out.reshape(depth * 16, 1, d6)


def _gmlp_kernel(h_ref, mod_ref, g1_ref, win_ref, gv_ref, ws_ref, bs_ref, wout_ref, o_ref, gated_ref):
    d = h_ref.shape[1]
    tm = h_ref.shape[0]
    h = h_ref[...]
    mod = mod_ref[0]
    n = _adaln(h, g1_ref[...], mod[:, 0:d], mod[:, d:2 * d]).astype(BF16)
    z = _gelu(jnp.dot(n, win_ref[...], preferred_element_type=F32))
    u = z[:, :d]
    v = _rms(z[:, d:], gv_ref[...]).astype(BF16)
    gd = d // SG_GROUPS
    for j in range(tm // CHUNK):
        rows = slice(j * CHUNK, (j + 1) * CHUNK)
        for g in range(SG_GROUPS):
            cols = slice(g * gd, (g + 1) * gd)
            sv = jnp.dot(ws_ref[g], v[rows, cols], preferred_element_type=F32) + bs_ref[g]
            gated_ref[rows, cols] = (u[rows, cols] * sv).astype(BF16)
    m = jnp.dot(gated_ref[...], wout_ref[...], preferred_element_type=F32)
    o_ref[...] = h + mod[:, 2 * d:3 * d] * m


def gmlp_layer(h, mod3, layer, g1, w_in, g_v, w_s, b_s, w_out, n_rows, seg):
    d = h.shape[1]
    gd = d // SG_GROUPS
    bs_b = jnp.broadcast_to(b_s[:, :, None], (SG_GROUPS, CHUNK, gd)).astype(F32)
    full = lambda *shape: pl.BlockSpec(shape, lambda i: (0,) * len(shape))
    return pl.pallas_call(
        _gmlp_kernel,
        grid=(n_rows // TM,),
        in_specs=[pl.BlockSpec((TM, d), lambda i: (i, 0)),
                  _mod_spec(layer, d, seg // TM),
                  full(1, d), full(d, 2 * d), full(1, d),
                  full(SG_GROUPS, CHUNK, CHUNK), full(SG_GROUPS, CHUNK, gd), full(d, d)],
        out_specs=pl.BlockSpec((TM, d), lambda i: (i, 0)),
        out_shape=jax.ShapeDtypeStruct((n_rows, d), F32),
        scratch_shapes=[pltpu.VMEM((TM, d), BF16)],
        compiler_params=_cparams(("arbitrary",)),
        name="gmlp_mixer",
    )(h, mod3, g1.reshape(1, d), w_in.astype(BF16), g_v.reshape(1, d), w_s.astype(BF16), bs_b,
      w_out.astype(BF16))


def _ln_proj_kernel(h_ref, mod_ref, g1_ref, w_ref, o_ref):
    d = h_ref.shape[1]
    mod = mod_ref[0]
    n = _adaln(h_ref[...], g1_ref[...], mod[:, 0:d], mod[:, d:2 * d]).astype(BF16)
    o_ref[...] = jnp.dot(n, w_ref[...], preferred_element_type=F32)


def ln_proj(h, mod3, layer, g1, w, seg):
    n_rows, d = h.shape
    dn = w.shape[1]
    return pl.pallas_call(
        _ln_proj_kernel,
        grid=(n_rows // TM,),
        in_specs=[pl.BlockSpec((TM, d), lambda i: (i, 0)),
                  _mod_spec(layer, d, seg // TM),
                  pl.BlockSpec((1, d), lambda i: (0, 0)),
                  pl.BlockSpec((d, dn), lambda i: (0, 0))],
        out_specs=pl.BlockSpec((TM, dn), lambda i: (i, 0)),
        out_shape=jax.ShapeDtypeStruct((n_rows, dn), F32),
        compiler_params=_cparams(("arbitrary",)),
        name="adaln_in_proj",
    )(h, mod3, g1.reshape(1, d), w.astype(BF16))


def _out_proj_kernel(y_ref, h_ref, mod_ref, w_ref, o_ref, *, glu):
    d = h_ref.shape[1]
    mod = mod_ref[0]
    z = jnp.dot(y_ref[...], w_ref[...], preferred_element_type=F32)
    if glu:
        z = z[:, :d] * _sigmoid(z[:, d:])
    o_ref[...] = h_ref[...] + mod[:, 2 * d:3 * d] * z


def out_proj(y, h, mod3, layer, w, seg, glu):
    n_rows, d = h.shape
    dk, dn = w.shape
    return pl.pallas_call(
        functools.partial(_out_proj_kernel, glu=glu),
        grid=(n_rows // TM,),
        in_specs=[pl.BlockSpec((TM, dk), lambda i: (i, 0)),
                  pl.BlockSpec((TM, d), lambda i: (i, 0)),
                  _mod_spec(layer, d, seg // TM),
                  pl.BlockSpec((dk, dn), lambda i: (0, 0))],
        out_specs=pl.BlockSpec((TM, d), lambda i: (i, 0)),
        out_shape=jax.ShapeDtypeStruct((n_rows, d), F32),
        compiler_params=_cparams(("arbitrary",)),
        name="mixer_out_proj",
    )(y, h, mod3, w.astype(BF16))


def _s5_params(lam_re, lam_im, log_dt, b_re, b_im, c_re, c_im):
    ng, npst = lam_re.shape
    gpb = LANES // S5_GROUP_CH
    nj = ng // gpb
    dt = jnp.exp(log_dt.astype(F32))[:, None]
    mag = jnp.exp(lam_re * dt)
    a_re = mag * jnp.cos(lam_im * dt)
    a_im = mag * jnp.sin(lam_im * dt)
    den = lam_re * lam_re + lam_im * lam_im
    n_re = a_re - 1.0
    f_re = (n_re * lam_re + a_im * lam_im) / den
    f_im = (a_im * lam_re - n_re * lam_im) / den
    bb_re = f_re[..., None] * b_re - f_im[..., None] * b_im
    bb_im = f_re[..., None] * b_im + f_im[..., None] * b_re
    eye = jnp.eye(gpb, dtype=F32)

    def blockdiag_in(bb):
        x = bb.reshape(nj, gpb, npst, S5_GROUP_CH).transpose(0, 1, 3, 2)
        return (x[:, :, :, None, :] * eye[None, :, None, :, None]).reshape(nj, LANES, gpb * npst)

    def blockdiag_out(cc):
        x = cc.reshape(nj, gpb, S5_GROUP_CH, npst).transpose(0, 1, 3, 2)
        return (x[:, :, :, None, :] * eye[None, :, None, :, None]).reshape(nj, gpb * npst, LANES)

    b_in = jnp.concatenate([blockdiag_in(bb_re), blockdiag_in(bb_im)], axis=2)
    c_out = jnp.concatenate([blockdiag_out(c_re), blockdiag_out(-c_im)], axis=1)
    a = jnp.stack([a_re.reshape(nj, 1, gpb * npst), a_im.reshape(nj, 1, gpb * npst)], axis=1)
    return a, b_in.astype(BF16), c_out.astype(BF16)


def _s5_dir_kernel(x_ref, *rest, reverse, final):
    if final:
        yf_ref, d_ref, b_ref, c_ref, a_ref, o_ref, u_ref, h_ref = rest
    else:
        b_ref, c_ref, a_ref, o_ref, u_ref, h_ref = rest
    nj = b_ref.shape[0]
    sw = b_ref.shape[2] // 2
    steps = x_ref.shape[0] // SUBLANES

    @pl.when(pl.program_id(0) == 0)
    def _():
        h_ref[...] = jnp.zeros_like(h_ref)

    x = x_ref[...]
    xb = x.astype(BF16)
    for j in range(nj):
        u_ref[:, j * 2 * sw:(j + 1) * 2 * sw] = jnp.dot(xb[:, j * LANES:(j + 1) * LANES], b_ref[j],
                                                          preferred_element_type=F32)
    for j in range(nj):
        re_c = slice(j * 2 * sw, j * 2 * sw + sw)
        im_c = slice(j * 2 * sw + sw, (j + 1) * 2 * sw)
        a_re = jnp.broadcast_to(a_ref[j, 0], (SUBLANES, sw))
        a_im = jnp.broadcast_to(a_ref[j, 1], (SUBLANES, sw))

        def step(s, carry, re_c=re_c, im_c=im_c, a_re=a_re, a_im=a_im):
            h_re, h_im = carry
            t = (steps - 1 - s) if reverse else s
            rows = pl.ds(pl.multiple_of(t * SUBLANES, SUBLANES), SUBLANES)
            n_re = a_re * h_re - a_im * h_im + u_ref[rows, re_c]
            n_im = a_re * h_im + a_im * h_re + u_ref[rows, im_c]
            u_ref[rows, re_c] = n_re
            u_ref[rows, im_c] = n_im
            return n_re, n_im

        h_re, h_im = lax.fori_loop(0, steps, step, (h_ref[j, 0], h_ref[j, 1]), unroll=2)
        h_ref[j, 0] = h_re
        h_ref[j, 1] = h_im
    for j in range(nj):
        cols = slice(j * LANES, (j + 1) * LANES)
        y = jnp.dot(u_ref[:, j * 2 * sw:(j + 1) * 2 * sw].astype(BF16), c_ref[j], preferred_element_type=F32)
        if final:
            o_ref[:, cols] = _gelu(d_ref[:, cols] * x[:, cols] + yf_ref[:, cols] + y).astype(BF16)
        else:
            o_ref[:, cols] = y


def s5_layer(h, mod3, layer, g1, p, n_lat, n_ctx, nb, seg):
    d = h.shape[1]
    assert nb == SUBLANES
    xs = ln_proj(h, mod3, layer, g1, p['w_in'], seg)
    l_lat, l_ctx = n_lat // nb, n_ctx // nb
    to_tb = lambda x, l: x.reshape(nb, l, d).transpose(1, 0, 2)
    x_tb = jnp.concatenate([to_tb(xs[n_lat:], l_ctx), to_tb(xs[:n_lat], l_lat)], axis=0).reshape(-1, d)
    rows = S5_TB * nb
    n_blk, n_cblk = (l_ctx + l_lat) // S5_TB, l_ctx // S5_TB
    fwd = lambda i: (i, 0)
    rev = lambda i: (jnp.where(i < n_cblk, n_cblk - 1 - i, n_blk - 1 + n_cblk - i), 0)
    full = lambda a: pl.BlockSpec(a.shape, lambda i: (0,) * a.ndim)
    y = None
    for k, order in enumerate((fwd, rev)):
        a, b_in, c_out = _s5_params(p['lam_re'][k], p['lam_im'][k], p['log_dt'][k], p['b_re'][k], p['b_im'][k],
                                    p['c_re'][k], p['c_im'][k])
        final = k == 1
        blk = pl.BlockSpec((rows, d), order)
        args = [x_tb] + ([y, p['d'].reshape(1, d)] if final else []) + [b_in, c_out, a]
        specs = [blk] + ([blk, full(args[2])] if final else []) + [full(b_in), full(c_out), full(a)]
        y = pl.pallas_call(
            functools.partial(_s5_dir_kernel, reverse=bool(k), final=final),
            grid=(n_blk,),
            in_specs=specs,
            out_specs=blk,
            out_shape=jax.ShapeDtypeStruct(x_tb.shape, BF16 if final else F32),
            scratch_shapes=[pltpu.VMEM((rows, b_in.shape[0] * b_in.shape[2]), F32),
                            pltpu.VMEM((b_in.shape[0], 2, SUBLANES, b_in.shape[2] // 2), F32)],
            compiler_params=_cparams(("arbitrary",)),
            name="s5_scan_reverse" if final else "s5_scan_forward",
        )(*args)
    y = y.reshape(l_ctx + l_lat, nb, d)
    from_tb = lambda x: x.transpose(1, 0, 2).reshape(-1, d)
    y_rows = jnp.concatenate([from_tb(y[l_ctx:]), from_tb(y[:l_ctx])], axis=0)
    return out_proj(y_rows, h, mod3, layer, p['w_glu'], seg, glu=True)


def _mla_proj_kernel(h_ref, mod_ref, g1_ref, win_ref, gq_ref, gkv_ref, wuq_ref, wukv_ref,
                     gqn_ref, gkn_ref, cos_ref, sin_ref, q_ref, k_ref, v_ref):
    d = h_ref.shape[1]
    mod = mod_ref[0]
    n = _adaln(h_ref[...], g1_ref[...], mod[:, 0:d], mod[:, d:2 * d]).astype(BF16)
    z = jnp.dot(n, win_ref[...], preferred_element_type=F32)
    ql = _rms(z[:, :MLA_Q_RANK], gq_ref[...]).astype(BF16)
    r0 = MLA_Q_RANK + MLA_KV_RANK
    kvl = _rms(z[:, MLA_Q_RANK:r0], gkv_ref[...]).astype(BF16)
    pe, pe_sw = z[:, r0:r0 + LANES], z[:, r0 + LANES:r0 + 2 * LANES]
    qa = jnp.dot(ql, wuq_ref[...], preferred_element_type=F32)
    kv = jnp.dot(kvl, wukv_ref[...], preferred_element_type=F32)
    cos, sin = cos_ref[...], sin_ref[...]
    gqn, gkn = gqn_ref[...], gkn_ref[...]
    kr = (pe * gkn[1:2]) * cos + (pe_sw * gkn[2:3]) * sin
    pe_ss = jnp.sum(pe * pe, axis=-1, keepdims=True)
    hw = 2 * LANES
    qscale = 1.0 / math.sqrt(MLA_QK)
    for hd in range(MLA_HEADS):
        qn = qa[:, hd * hw:hd * hw + LANES]
        qr = qa[:, hd * hw + LANES:(hd + 1) * hw]
        qsw = qa[:, MLA_HEADS * hw + hd * LANES:MLA_HEADS * hw + (hd + 1) * LANES]
        ss = jnp.sum(qn * qn, axis=-1, keepdims=True) + jnp.sum(qr * qr, axis=-1, keepdims=True)
        rq = lax.rsqrt(ss * (1.0 / MLA_QK) + EPS) * qscale
        q_ref[:, hd * hw:hd * hw + LANES] = (qn * rq * gqn[0:1]).astype(BF16)
        q_ref[:, hd * hw + LANES:(hd + 1) * hw] = (rq * ((qr * gqn[1:2]) * cos + (qsw * gqn[2:3]) * sin)).astype(BF16)
        kn = kv[:, hd * LANES:(hd + 1) * LANES]
        rk = lax.rsqrt((jnp.sum(kn * kn, axis=-1, keepdims=True) + pe_ss) * (1.0 / MLA_QK) + EPS)
        k_ref[:, hd * hw:hd * hw + LANES] = (kn * rk * gkn[0:1]).astype(BF16)
        k_ref[:, hd * hw + LANES:(hd + 1) * hw] = (kr * rk).astype(BF16)
    v_ref[...] = kv[:, MLA_HEADS * LANES:].astype(BF16)


def _attn_kernel(q_ref, kc_ref, vc_ref, *rest, with_latent):
    if with_latent:
        kl_ref, vl_ref, o_ref = rest
    else:
        (o_ref,) = rest
    nt = (((1,), (1,)), ((), ()))
    blocks = [(kc_ref, vc_ref, 0, kc_ref.shape[0])]
    if with_latent:
        kb = min(ATTN_KB, kl_ref.shape[0])
        blocks += [(kl_ref, vl_ref, j * kb, kb) for j in range(kl_ref.shape[0] // kb)]
    tq = q_ref.shape[0]
    qs = min(ATTN_QS, tq)
    for qi in range(tq // qs):
        q = q_ref[qi * qs:(qi + 1) * qs, :]
        mx = den = acc = None
        for k_ref, v_ref, start, size in blocks:
            s = lax.dot_general(q, k_ref[start:start + size, :], nt, preferred_element_type=F32)
            bmx = jnp.max(s, axis=-1, keepdims=True)
            new_mx = bmx if mx is None else jnp.maximum(mx, bmx)
            p = jnp.exp(s - new_mx)
            psum = jnp.sum(p, axis=-1, keepdims=True)
            pv = jnp.dot(p.astype(BF16), v_ref[start:start + size, :], preferred_element_type=F32)
            if mx is None:
                den, acc = psum, pv
            else:
                alpha = jnp.exp(mx - new_mx)
                den, acc = alpha * den + psum, alpha * acc + pv
            mx = new_mx
        o_ref[qi * qs:(qi + 1) * qs, :] = (acc / den).astype(BF16)


def _rope_tables(l_lat, n_rows_id):
    rows = l_lat // GRID_W
    row = jnp.repeat(jnp.arange(rows), GRID_W).astype(F32)
    col = jnp.tile(jnp.arange(GRID_W), rows).astype(F32)
    quarter = MLA_ROPE // 4
    inv_freq = ROPE_THETA ** (-jnp.arange(quarter, dtype=F32) / quarter)
    ang_r, ang_c = row[:, None] * inv_freq, col[:, None] * inv_freq
    cos = jnp.concatenate([jnp.cos(ang_r), jnp.cos(ang_r), jnp.cos(ang_c), jnp.cos(ang_c)], axis=1)
    sin = jnp.concatenate([-jnp.sin(ang_r), jnp.sin(ang_r), -jnp.sin(ang_c), jnp.sin(ang_c)], axis=1)
    pad = ((0, n_rows_id), (0, LANES - MLA_ROPE))
    cos = jnp.pad(cos, pad).at[l_lat:, :MLA_ROPE].set(1.0)
    return cos, jnp.pad(sin, pad)


def mla_layer(h, mod3, layer, g1, p, n_lat, n_ctx, nb, seg):
    n_rows, d = h.shape
    hw = 2 * LANES
    nh = MLA_HEADS
    l_lat, l_ctx = n_lat // nb, n_ctx // nb
    r0 = MLA_Q_RANK + MLA_KV_RANK
    swap = np.arange(MLA_ROPE)
    swap = np.where(swap % 32 < 16, swap + 16, swap - 16)
    lane_pad = lambda x: jnp.pad(x, ((0, 0),) * (x.ndim - 1) + ((0, LANES - x.shape[-1]),))
    w_pe = p['w_in'][:, r0:]
    w_in_ext = jnp.concatenate([p['w_in'][:, :r0], lane_pad(w_pe), lane_pad(w_pe[:, swap])], axis=1).astype(BF16)
    wq = p['w_uq'].reshape(MLA_Q_RANK, nh, MLA_QK)
    wq_main = jnp.pad(wq, ((0, 0), (0, 0), (0, hw - MLA_QK))).reshape(MLA_Q_RANK, nh * hw)
    wq_sw = lane_pad(wq[:, :, MLA_NOPE:][:, :, swap]).reshape(MLA_Q_RANK, nh * LANES)
    w_uq_ext = jnp.concatenate([wq_main, wq_sw], axis=1).astype(BF16)
    wkv = p['w_ukv'].reshape(MLA_KV_RANK, nh, 2, LANES).transpose(0, 2, 1, 3).reshape(MLA_KV_RANK, 2 * nh * LANES)
    gains = lambda g: jnp.zeros((SUBLANES, LANES), F32).at[0].set(g[:MLA_NOPE]).at[1, :MLA_ROPE].set(
        g[MLA_NOPE:]).at[2, :MLA_ROPE].set(g[MLA_NOPE:][swap])
    cos, sin = _rope_tables(l_lat, seg)
    t_lat = l_lat // TM
    tps = seg // TM
    rope_map = lambda i: (jnp.where(i < (n_lat // TM), i % t_lat, t_lat + i % tps), 0)
    full = lambda *shape: pl.BlockSpec(shape, lambda i: (0,) * len(shape))
    q, k, v = pl.pallas_call(
        _mla_proj_kernel,
        grid=(n_rows // TM,),
        in_specs=[pl.BlockSpec((TM, d), lambda i: (i, 0)), _mod_spec(layer, d, tps), full(1, d),
                  full(d, r0 + 2 * LANES), full(1, MLA_Q_RANK), full(1, MLA_KV_RANK),
                  full(MLA_Q_RANK, nh * (hw + LANES)), full(MLA_KV_RANK, 2 * nh * LANES),
                  full(SUBLANES, LANES), full(SUBLANES, LANES),
                  pl.BlockSpec((TM, LANES), rope_map), pl.BlockSpec((TM, LANES), rope_map)],
        out_specs=[pl.BlockSpec((TM, nh * hw), lambda i: (i, 0)), pl.BlockSpec((TM, nh * hw), lambda i: (i, 0)),
                   pl.BlockSpec((TM, nh * LANES), lambda i: (i, 0))],
        out_shape=[jax.ShapeDtypeStruct((n_rows, nh * hw), BF16), jax.ShapeDtypeStruct((n_rows, nh * hw), BF16),
                   jax.ShapeDtypeStruct((n_rows, nh * LANES), BF16)],
        compiler_params=_cparams(("arbitrary",)),
        name="mla_projection",
    )(h, mod3, g1.reshape(1, d), w_in_ext, p['g_q'].reshape(1, -1), p['g_kv'].reshape(1, -1), w_uq_ext,
      wkv.astype(BF16), gains(p['g_qn']), gains(p['g_kn']), cos, sin)

    tq = min(512, l_lat)
    cb = n_lat // l_ctx
    o_lat = pl.pallas_call(
        functools.partial(_attn_kernel, with_latent=True),
        grid=(nb, nh, l_lat // tq),
        in_specs=[pl.BlockSpec((tq, hw), lambda b, hd, i: (b * (l_lat // tq) + i, hd)),
                  pl.BlockSpec((l_ctx, hw), lambda b, hd, i: (cb + b, hd)),
                  pl.BlockSpec((l_ctx, LANES), lambda b, hd, i: (cb + b, hd)),
                  pl.BlockSpec((l_lat, hw), lambda b, hd, i: (b, hd)),
                  pl.BlockSpec((l_lat, LANES), lambda b, hd, i: (b, hd))],
        out_specs=pl.BlockSpec((tq, LANES), lambda b, hd, i: (b * (l_lat // tq) + i, hd)),
        out_shape=jax.ShapeDtypeStruct((n_lat, nh * LANES), BF16),
        compiler_params=_cparams(("arbitrary", "arbitrary", "arbitrary")),
        name="mla_attention_latent",
    )(q, k, v, k, v)
    o_ctx = pl.pallas_call(
        functools.partial(_attn_kernel, with_latent=False),
        grid=(nb, nh),
        in_specs=[pl.BlockSpec((l_ctx, hw), lambda b, hd: (cb + b, hd)),
                  pl.BlockSpec((l_ctx, hw), lambda b, hd: (cb + b, hd)),
                  pl.BlockSpec((l_ctx, LANES), lambda b, hd: (cb + b, hd))],
        out_specs=pl.BlockSpec((l_ctx, LANES), lambda b, hd: (b, hd)),
        out_shape=jax.ShapeDtypeStruct((n_ctx, nh * LANES), BF16),
        compiler_params=_cparams(("arbitrary", "arbitrary")),
        name="mla_attention_context",
    )(q, k, v)
    return out_proj(jnp.concatenate([o_lat, o_ctx], axis=0), h, mod3, layer, p['w_out'], seg, glu=False)


def _route_kernel(h_ref, mod_ref, g2_ref, wrt_ref, br_ref, n2_ref, pos_ref, gate_ref, cnt_ref):
    d = h_ref.shape[1]
    tm = h_ref.shape[0]
    ne = wrt_ref.shape[0]
    mod = mod_ref[0]
    n2 = _adaln(h_ref[...], g2_ref[...], mod[:, 3 * d:4 * d], mod[:, 4 * d:5 * d])
    n2_ref[...] = n2.astype(BF16)
    logits = lax.dot_general(wrt_ref[...], n2, (((1,), (1,)), ((), ())), preferred_element_type=F32,
                             precision=lax.Precision.HIGHEST) + br_ref[...]
    eidx = lax.broadcasted_iota(I32, (ne, tm), 0).astype(F32)
    work = logits
    vals, idxs = [], []
    for _ in range(TOP_K):
        mx = jnp.max(work, axis=0, keepdims=True)
        idx = jnp.min(jnp.where(work == mx, eidx, float(ne)), axis=0, keepdims=True)
        vals.append(mx)
        idxs.append(idx)
        work = jnp.where(eidx == idx, -jnp.inf, work)
    ex = [jnp.exp(v - vals[0]) for v in vals]
    den = ex[0] + ex[1] + ex[2] + ex[3]
    sel = (eidx == idxs[0]) | (eidx == idxs[1]) | (eidx == idxs[2]) | (eidx == idxs[3])
    onehot = jnp.where(sel, 1.0, 0.0).astype(BF16)
    r = lax.broadcasted_iota(I32, (tm, tm), 0)
    c = lax.broadcasted_iota(I32, (tm, tm), 1)
    before = jnp.where(r < c, 1.0, 0.0).astype(BF16)
    rank_all = jnp.dot(onehot, before, preferred_element_type=F32)
    cnt = jnp.dot(onehot, jnp.ones((tm, tm), BF16), preferred_element_type=F32)
    cnt8 = jnp.floor((cnt + (SUBLANES - 1)) * (1.0 / SUBLANES)) * SUBLANES
    er = lax.broadcasted_iota(I32, (ne, LANES), 0)
    ec = lax.broadcasted_iota(I32, (ne, LANES), 1)
    lower = jnp.where(ec < er, 1.0, 0.0).astype(BF16)
    cnt8_p = jnp.concatenate([cnt8, jnp.zeros((LANES - ne, tm), F32)], axis=0).astype(BF16)
    off = jnp.dot(lower, cnt8_p, preferred_element_type=F32)
    base = rank_all + off
    for k in range(TOP_K):
        pos = jnp.sum(jnp.where(eidx == idxs[k], base, 0.0), axis=0, keepdims=True)
        pos_ref[k:k + 1, :] = pos.astype(I32)
        gate_ref[k:k + 1, :] = ex[k] / den
    cnt_ref[0] = cnt8[:, :LANES].astype(I32)


def moe_route(h, mod3, layer, g2, w_router, b_router, n_rows, seg):
    d = h.shape[1]
    ne = w_router.shape[1]
    nt = n_rows // TM
    return pl.pallas_call(
        _route_kernel,
        grid=(nt,),
        in_specs=[pl.BlockSpec((TM, d), lambda i: (i, 0)), _mod_spec(layer, d, seg // TM),
                  pl.BlockSpec((1, d), lambda i: (0, 0)), pl.BlockSpec((ne, d), lambda i: (0, 0)),
                  pl.BlockSpec((ne, 1), lambda i: (0, 0))],
        out_specs=[pl.BlockSpec((TM, d), lambda i: (i, 0)), pl.BlockSpec((TOP_K, TM), lambda i: (0, i)),
                   pl.BlockSpec((TOP_K, TM), lambda i: (0, i)), pl.BlockSpec((1, ne, LANES), lambda i: (i, 0, 0))],
        out_shape=[jax.ShapeDtypeStruct((n_rows, d), BF16), jax.ShapeDtypeStruct((TOP_K, n_rows), I32),
                   jax.ShapeDtypeStruct((TOP_K, n_rows), F32), jax.ShapeDtypeStruct((nt, ne, LANES), I32)],
        compiler_params=_cparams(("arbitrary",)),
        name="moe_route",
    )(h, mod3, g2.reshape(1, d), w_router.T, b_router.reshape(ne, 1))


def _chunk_copies(tab_ref, tile, e, vm_ref, hbm_ref, sem, to_hbm):
    ne = N_EXPERTS
    stride = tab_ref.shape[0] // 3
    off = tab_ref[tile * ne + e]
    n8 = tab_ref[stride + tile * ne + e]
    dst = tab_ref[2 * stride + tile * ne + e]
    out = []
    for size in PIECES:
        bit = (size // SUBLANES).bit_length() - 1
        done = ((n8 >> (bit + 1)) << (bit + 1)) * SUBLANES
        lo = pl.multiple_of(off + done, SUBLANES)
        hi = pl.multiple_of(dst + done, SUBLANES)
        v, hb = vm_ref.at[pl.ds(lo, size), :], hbm_ref.at[pl.ds(hi, size), :]
        cp = pltpu.make_async_copy(v, hb, sem) if to_hbm else pltpu.make_async_copy(hb, v, sem)
        out.append((((n8 >> bit) & 1) == 1, cp))
    return out


def _for_each_copy(tab_ref, tile, vm_ref, hbm_ref, sem, to_hbm, action):
    small = PIECES.index(32)

    def body(e, carry):
        copies = _chunk_copies(tab_ref, tile, e, vm_ref, hbm_ref, sem, to_hbm)

        def run(items):
            for cond, cp in items:
                @pl.when(cond)
                def _():
                    action(cp)
        run(copies[small:])

        @pl.when(tab_ref[tab_ref.shape[0] // 3 + tile * N_EXPERTS + e] >= 2 * PIECES[small] // SUBLANES)
        def _():
            run(copies[:small])
        return carry
    lax.fori_loop(0, N_EXPERTS, body, 0)


def _wait_rows(n_rows, vm_ref, hbm_ref, sem, to_hbm):
    n8 = n_rows >> (SUBLANES.bit_length() - 1)
    size = SUBLANES
    while size * 2 <= vm_ref.shape[0]:
        size *= 2
    while size >= SUBLANES:
        v, hb = vm_ref.at[pl.ds(0, size), :], hbm_ref.at[pl.ds(0, size), :]
        cp = pltpu.make_async_copy(v, hb, sem) if to_hbm else pltpu.make_async_copy(hb, v, sem)

        @pl.when(((n8 >> ((size // SUBLANES).bit_length() - 1)) & 1) == 1)
        def _():
            cp.wait()
        size //= 2


def _dispatch_kernel(tab_ref, rows_ref, tail_ref, nb_ref, n2_ref, pos_ref, xb_ref, sorted_ref, zero_ref, sem, zsem):
    i = pl.program_id(0)
    last = pl.num_programs(0) - 1
    slot = i & 1
    cur = sorted_ref.at[slot]
    tm = n2_ref.shape[0]
    rmax = sorted_ref.shape[1]
    ridx = lax.broadcasted_iota(I32, (rmax, tm), 0)
    hit = ridx == pos_ref[0:1, :]
    for k in range(1, TOP_K):
        hit = hit | (ridx == pos_ref[k:k + 1, :])
    perm = jnp.where(hit, 1.0, 0.0).astype(BF16)
    cur[...] = jnp.dot(perm, n2_ref[...], preferred_element_type=F32)
    _for_each_copy(tab_ref, i, cur, xb_ref, sem.at[slot], True, lambda cp: cp.start())

    @pl.when(i > 0)
    def _():
        _wait_rows(rows_ref[i - 1], sorted_ref.at[1 - slot], xb_ref, sem.at[1 - slot], True)

    @pl.when(i == last)
    def _():
        _wait_rows(rows_ref[i], cur, xb_ref, sem.at[slot], True)
        zero_ref[...] = jnp.zeros_like(zero_ref)
        _for_each_copy(tail_ref, 0, zero_ref, xb_ref, zsem, True, lambda cp: cp.start())
        _for_each_copy(tail_ref, 0, zero_ref, xb_ref, zsem, True, lambda cp: cp.wait())

        def spare(b, carry):
            cp = pltpu.make_async_copy(zero_ref, xb_ref.at[pl.ds(pl.multiple_of(b * MOE_BM, MOE_BM), MOE_BM), :], zsem)
            cp.start()
            cp.wait()
            return carry
        lax.fori_loop(nb_ref[0], xb_ref.shape[0] // MOE_BM, spare, 0)


def _expert_kernel(be_ref, first_ref, nb_ref, x_ref, wg_ref, bg_ref, wu_ref, bu_ref, wd_ref, bd_ref, y_ref,
                   wg_s, wu_s, wd_s):
    b = pl.program_id(0)

    @pl.when(b < nb_ref[0])
    def _():
        @pl.when(first_ref[b] == 1)
        def _():
            wg_s[...] = wg_ref[0, 0].astype(BF16)
            wu_s[...] = wu_ref[0, 0].astype(BF16)
            wd_s[...] = wd_ref[0, 0].astype(BF16)

        x = x_ref[...].astype(BF16)
        g = jnp.minimum(jnp.dot(x, wg_s[...], preferred_element_type=F32) + bg_ref[0], SWIGLU_LIMIT)
        u = jnp.clip(jnp.dot(x, wu_s[...], preferred_element_type=F32) + bu_ref[0], -SWIGLU_LIMIT, SWIGLU_LIMIT)
        a = (g * _sigmoid(SWIGLU_ALPHA * g) * (u + 1.0)).astype(BF16)
        y_ref[...] = jnp.dot(a, wd_s[...], preferred_element_type=F32) + bd_ref[0]

    @pl.when(b >= nb_ref[0])
    def _():
        y_ref[...] = jnp.zeros_like(y_ref)


def _combine_kernel(tab_ref, rows_ref, pos_ref, gate_ref, h_ref, mod_ref, yb_ref, o_ref, ys_ref, sem):
    i = pl.program_id(0)
    slot = i & 1
    d = h_ref.shape[1]
    tm = h_ref.shape[0]
    rmax = ys_ref.shape[1]

    @pl.when(i == 0)
    def _():
        ys_ref[...] = jnp.zeros_like(ys_ref)
        _for_each_copy(tab_ref, 0, ys_ref.at[0], yb_ref, sem.at[0], False, lambda cp: cp.start())

    @pl.when(i + 1 < pl.num_programs(0))
    def _():
        _for_each_copy(tab_ref, i + 1, ys_ref.at[1 - slot], yb_ref, sem.at[1 - slot], False, lambda cp: cp.start())

    cidx = lax.broadcasted_iota(I32, (tm, rmax), 1)
    wt = jnp.where(cidx == pos_ref[:, 0:1], gate_ref[:, 0:1], 0.0)
    for k in range(1, TOP_K):
        wt = wt + jnp.where(cidx == pos_ref[:, k:k + 1], gate_ref[:, k:k + 1], 0.0)
    _wait_rows(rows_ref[i], ys_ref.at[slot], yb_ref, sem.at[slot], False)
    f = jnp.dot(wt.astype(BF16), ys_ref[slot].astype(BF16), preferred_element_type=F32)
    o_ref[...] = h_ref[...] + mod_ref[0][:, 5 * d:6 * d] * f


def _moe_plan(cnt8, n_blocks_max):
    nt, ne = cnt8.shape
    loc_off = jnp.cumsum(cnt8, axis=1) - cnt8
    used = jnp.sum(cnt8, axis=0)
    rows_e = (used + MOE_BM - 1) // MOE_BM * MOE_BM
    e_end = jnp.cumsum(rows_e)
    e_start = e_end - rows_e
    dst = e_start[None, :] + jnp.cumsum(cnt8, axis=0) - cnt8
    tab = jnp.stack([loc_off, cnt8 // SUBLANES, dst]).reshape(3 * nt * ne).astype(I32)
    tail = jnp.stack([jnp.zeros((ne,), I32), (rows_e - used) // SUBLANES, e_start + used]).reshape(3 * ne).astype(I32)
    nb = (e_end[-1] // MOE_BM).astype(I32)
    blk = jnp.arange(n_blocks_max, dtype=I32)
    blk_c = jnp.minimum(blk, nb - 1)
    blk_e = jnp.sum((e_end[None, :] <= (blk_c * MOE_BM)[:, None]).astype(I32), axis=1)
    blk_e = jnp.minimum(blk_e, ne - 1)
    first = (blk_c * MOE_BM == e_start[blk_e]).astype(I32)
    tile_rows = jnp.sum(cnt8, axis=1).astype(I32)
    return tab, tile_rows, tail, blk_e, first, nb.reshape(1)


def moe_layer(h, mod3, layer, g2, p, n_rows, seg):
    d = h.shape[1]
    ne = N_EXPERTS
    nt = n_rows // TM
    n2, pos, gate, cnt = moe_route(h, mod3, layer, g2, p['w_router'], p['b_router'], n_rows, seg)
    max_rows = n_rows * TOP_K + nt * ne * (SUBLANES - 1) + ne * (MOE_BM - 1)
    nbm = -(-max_rows // MOE_BM)
    tab, tile_rows, tail, blk_e, first, nb = _moe_plan(cnt[:, :, 0], nbm)

    xb = pl.pallas_call(
        _dispatch_kernel,
        grid_spec=pltpu.PrefetchScalarGridSpec(
            num_scalar_prefetch=4, grid=(nt,),
            in_specs=[pl.BlockSpec((TM, d), lambda i, *_: (i, 0)),
                      pl.BlockSpec((TOP_K, TM), lambda i, *_: (0, i))],
            out_specs=pl.BlockSpec(memory_space=pl.ANY),
            scratch_shapes=[pltpu.VMEM((2, MOE_RMAX, d), F32), pltpu.VMEM((PIECES[0], d), F32),
                            pltpu.SemaphoreType.DMA((2,)), pltpu.SemaphoreType.DMA(())]),
        out_shape=jax.ShapeDtypeStruct((nbm * MOE_BM, d), F32),
        compiler_params=_cparams(("arbitrary",)),
        name="moe_dispatch",
    )(tab, tile_rows, tail, nb, n2, pos)

    row_map = lambda b, be, fi, nbr: (jnp.maximum(jnp.minimum(b, nbr[0] - 1), 0), 0)
    out_map = lambda b, be, fi, nbr: (b, 0)
    w_map = lambda b, be, fi, nbr: (layer, be[b], 0, 0)
    b_map = lambda b, be, fi, nbr: (layer * ne + be[b], 0, 0)
    yb = pl.pallas_call(
        _expert_kernel,
        grid_spec=pltpu.PrefetchScalarGridSpec(
            num_scalar_prefetch=3, grid=(nbm,),
            in_specs=[pl.BlockSpec((MOE_BM, d), row_map),
                      pl.BlockSpec((1, 1, d, d), w_map), pl.BlockSpec((1, 1, d), b_map),
                      pl.BlockSpec((1, 1, d, d), w_map), pl.BlockSpec((1, 1, d), b_map),
                      pl.BlockSpec((1, 1, d, d), w_map), pl.BlockSpec((1, 1, d), b_map)],
            out_specs=pl.BlockSpec((MOE_BM, d), out_map),
            scratch_shapes=[pltpu.VMEM((d, d), BF16)] * 3),
        out_shape=jax.ShapeDtypeStruct((nbm * MOE_BM, d), F32),
        compiler_params=_cparams(("arbitrary",)),
        name="moe_experts",
    )(blk_e, first, nb, xb, p['w_gate'], p['b_gate'].reshape(-1, 1, d), p['w_up'], p['b_up'].reshape(-1, 1, d),
      p['w_down'], p['b_down'].reshape(-1, 1, d))

    return pl.pallas_call(
        _combine_kernel,
        grid_spec=pltpu.PrefetchScalarGridSpec(
            num_scalar_prefetch=2, grid=(nt,),
            in_specs=[pl.BlockSpec((TM, TOP_K), lambda i, *_: (i, 0)),
                      pl.BlockSpec((TM, TOP_K), lambda i, *_: (i, 0)),
                      pl.BlockSpec((TM, d), lambda i, *_: (i, 0)),
                      pl.BlockSpec((1, 1, 6 * d), lambda i, *_: (layer * 16 + i // (seg // TM), 0, 0)),
                      pl.BlockSpec(memory_space=pl.ANY)],
            out_specs=pl.BlockSpec((TM, d), lambda i, *_: (i, 0)),
            scratch_shapes=[pltpu.VMEM((2, MOE_RMAX, d), F32), pltpu.SemaphoreType.DMA((2,))]),
        out_shape=jax.ShapeDtypeStruct((n_rows, d), F32),
        compiler_params=_cparams(("arbitrary",)),
        name="moe_combine",
    )(tab, tile_rows, pos.T, gate.T, h, mod3, yb)


def kernel(x, c, ctx, c_ctx, w_ada, b_ada, g_norm1, g_norm2, sg_w_in, sg_g_v, sg_w_s, sg_b_s, sg_w_out,
           ssm_w_in, ssm_lam_re, ssm_lam_im, ssm_log_dt, ssm_b_re, ssm_b_im, ssm_c_re, ssm_c_im, ssm_d, ssm_w_glu,
           mla_w_in, mla_g_q, mla_g_kv, mla_w_uq, mla_w_ukv, mla_g_qn, mla_g_kn, mla_w_out,
           moe_w_router, moe_b_router, moe_w_gate, moe_b_gate, moe_w_up, moe_b_up, moe_w_down, moe_b_down):
    nb, l_lat, d = x.shape
    l_ctx = ctx.shape[1]
    depth = w_ada.shape[0]
    n_lat, n_ctx = nb * l_lat, nb * l_ctx
    seg = l_lat
    assert n_ctx <= seg and seg % TM == 0 and n_ctx % TM == 0
    mod3 = modulation(c, c_ctx, w_ada, b_ada)
    h = jnp.concatenate([x.reshape(n_lat, d), ctx.reshape(n_ctx, d)], axis=0)
    for i in range(depth):
        mixer, slot = i % N_MIXERS, i // N_MIXERS
        ctx_out = i < depth - 1
        n_rows = n_lat + n_ctx if ctx_out else n_lat
        if mixer == 0:
            hm = gmlp_layer(h, mod3, i, g_norm1[i], sg_w_in[slot], sg_g_v[slot], sg_w_s[slot], sg_b_s[slot],
                            sg_w_out[slot], n_rows, seg)
        elif mixer == 1:
            p = dict(w_in=ssm_w_in[slot], lam_re=ssm_lam_re[slot], lam_im=ssm_lam_im[slot], log_dt=ssm_log_dt[slot],
                     b_re=ssm_b_re[slot], b_im=ssm_b_im[slot], c_re=ssm_c_re[slot], c_im=ssm_c_im[slot],
                     d=ssm_d[slot], w_glu=ssm_w_glu[slot])
            hm = s5_layer(h, mod3, i, g_norm1[i], p, n_lat, n_ctx, nb, seg)[:n_rows]
        else:
            p = dict(w_in=mla_w_in[slot], g_q=mla_g_q[slot], g_kv=mla_g_kv[slot], w_uq=mla_w_uq[slot],
                     w_ukv=mla_w_ukv[slot], g_qn=mla_g_qn[slot], g_kn=mla_g_kn[slot], w_out=mla_w_out[slot])
            hm = mla_layer(h, mod3, i, g_norm1[i], p, n_lat, n_ctx, nb, seg)[:n_rows]
        pm = dict(w_router=moe_w_router[i], b_router=moe_b_router[i], w_gate=moe_w_gate, b_gate=moe_b_gate,
                  w_up=moe_w_up, b_up=moe_b_up, w_down=moe_w_down, b_down=moe_b_down)
        h = moe_layer(hm, mod3, i, g_norm2[i], pm, n_rows, seg)
    return h[:n_lat].reshape(nb, l_lat, d)
```

```python
import functools
import math

import jax
import jax.numpy as jnp
import numpy as np
from jax import lax
from jax.experimental import pallas as pl
from jax.experimental.pallas import tpu as pltpu

F32 = jnp.float32
BF16 = jnp.bfloat16
I32 = jnp.int32
EPS = 1e-6

N_MIXERS = 3
GRID_W = 64
CHUNK = 128
SG_GROUPS = 8
S5_GROUP_CH = 16
S5_STATE = 64
MLA_HEADS = 8
MLA_NOPE = 128
MLA_ROPE = 64
MLA_QK = MLA_NOPE + MLA_ROPE
MLA_V = 128
MLA_Q_RANK = 384
MLA_KV_RANK = 256
ROPE_THETA = 10000.0
N_EXPERTS = 32
TOP_K = 4
SWIGLU_LIMIT = 7.0
SWIGLU_ALPHA = 1.702

LANES = 128
SUBLANES = 8
VMEM_LIMIT = 56 * 1024 * 1024

TM = 256
S5_TB = 32
MOE_BM = 512
MOE_SUB = 256
MOE_RMAX = TM * TOP_K + N_EXPERTS * SUBLANES
PIECES = (256, 128, 64, 32, 16, 8)


def _cparams(sem):
    return pltpu.CompilerParams(dimension_semantics=sem, vmem_limit_bytes=VMEM_LIMIT)


def _rms(x, g):
    ms = jnp.mean(x * x, axis=-1, keepdims=True)
    return x * lax.rsqrt(ms + EPS) * g


def _adaln(x, g, shift, scale):
    return _rms(x, g) * (1.0 + scale) + shift


def _gelu(x):
    return 0.5 * x * (1.0 + lax.erf(x * (1.0 / math.sqrt(2.0))))


def _sigmoid(x):
    return 1.0 / (1.0 + jnp.exp(-x))


def _mod_spec(layer, d, tiles_per_seg):
    return pl.BlockSpec((1, 1, 6 * d), lambda i: (layer * 16 + i // tiles_per_seg, 0, 0))


def _mod_kernel(s_ref, w_ref, b_ref, o_ref):
    s = s_ref[...]
    s = s * _sigmoid(s)
    o_ref[0] = jnp.dot(s, w_ref[0], preferred_element_type=F32, precision=lax.Precision.HIGHEST) + b_ref[0]


def modulation(c, c_ctx, w_ada, b_ada):
    depth, d, d6 = w_ada.shape
    nb = c.shape[0]
    s = jnp.zeros((16, d), F32).at[:nb].set(c).at[nb].set(c_ctx)
    tn = 1024
    out = pl.pallas_call(
        _mod_kernel,
        grid=(depth, d6 // tn),
        in_specs=[pl.BlockSpec((16, d), lambda l, j: (0, 0)),
                  pl.BlockSpec((1, d, tn), lambda l, j: (l, 0, j)),
                  pl.BlockSpec((1, 1, tn), lambda l, j: (l, 0, j))],
        out_specs=pl.BlockSpec((1, 16, tn), lambda l, j: (l, 0, j)),
        out_shape=jax.ShapeDtypeStruct((depth, 16, d6), F32),
        compiler_params=_cparams(("arbitrary", "arbitrary")),
        name="adaln_modulation",
    )(s, w_ada, b_ada.reshape(depth, 1, d6))
    return out.reshape(depth * 16, 1, d6)


def _gmlp_kernel(h_ref, mod_ref, g1_ref, win_ref, gv_ref, ws_ref, bs_ref, wout_ref, o_ref, gated_ref):
    d = h_ref.shape[1]
    tm = h_ref.shape[0]
    h = h_ref[...]
    mod = mod_ref[0]
    n = _adaln(h, g1_ref[...], mod[:, 0:d], mod[:, d:2 * d]).astype(BF16)
    z = _gelu(jnp.dot(n, win_ref[...], preferred_element_type=F32))
    u = z[:, :d]
    v = _rms(z[:, d:], gv_ref[...]).astype(BF16)
    gd = d // SG_GROUPS
    for j in range(tm // CHUNK):
        rows = slice(j * CHUNK, (j + 1) * CHUNK)
        for g in range(SG_GROUPS):
            cols = slice(g * gd, (g + 1) * gd)
            sv = jnp.dot(ws_ref[g], v[rows, cols], preferred_element_type=F32) + bs_ref[g]
            gated_ref[rows, cols] = (u[rows, cols] * sv).astype(BF16)
    m = jnp.dot(gated_ref[...], wout_ref[...], preferred_element_type=F32)
    o_ref[...] = h + mod[:, 2 * d:3 * d] * m


def gmlp_layer(h, mod3, layer, g1, w_in, g_v, w_s, b_s, w_out, n_rows, seg):
    d = h.shape[1]
    gd = d // SG_GROUPS
    bs_b = jnp.broadcast_to(b_s[:, :, None], (SG_GROUPS, CHUNK, gd)).astype(F32)
    full = lambda *shape: pl.BlockSpec(shape, lambda i: (0,) * len(shape))
    return pl.pallas_call(
        _gmlp_kernel,
        grid=(n_rows // TM,),
        in_specs=[pl.BlockSpec((TM, d), lambda i: (i, 0)),
                  _mod_spec(layer, d, seg // TM),
                  full(1, d), full(d, 2 * d), full(1, d),
                  full(SG_GROUPS, CHUNK, CHUNK), full(SG_GROUPS, CHUNK, gd), full(d, d)],
        out_specs=pl.BlockSpec((TM, d), lambda i: (i, 0)),
        out_shape=jax.ShapeDtypeStruct((n_rows, d), F32),
        scratch_shapes=[pltpu.VMEM((TM, d), BF16)],
        compiler_params=_cparams(("arbitrary",)),
        name="gmlp_mixer",
    )(h, mod3, g1.reshape(1, d), w_in.astype(BF16), g_v.reshape(1, d), w_s.astype(BF16), bs_b,
      w_out.astype(BF16))


def _ln_proj_kernel(h_ref, mod_ref, g1_ref, w_ref, o_ref):
    d = h_ref.shape[1]
    mod = mod_ref[0]
    n = _adaln(h_ref[...], g1_ref[...], mod[:, 0:d], mod[:, d:2 * d]).astype(BF16)
    o_ref[...] = jnp.dot(n, w_ref[...], preferred_element_type=F32)


def ln_proj(h, mod3, layer, g1, w, seg):
    n_rows, d = h.shape
    dn = w.shape[1]
    return pl.pallas_call(
        _ln_proj_kernel,
        grid=(n_rows // TM,),
        in_specs=[pl.BlockSpec((TM, d), lambda i: (i, 0)),
                  _mod_spec(layer, d, seg // TM),
                  pl.BlockSpec((1, d), lambda i: (0, 0)),
                  pl.BlockSpec((d, dn), lambda i: (0, 0))],
        out_specs=pl.BlockSpec((TM, dn), lambda i: (i, 0)),
        out_shape=jax.ShapeDtypeStruct((n_rows, dn), F32),
        compiler_params=_cparams(("arbitrary",)),
        name="adaln_in_proj",
    )(h, mod3, g1.reshape(1, d), w.astype(BF16))


def _out_proj_kernel(y_ref, h_ref, mod_ref, w_ref, o_ref, *, glu):
    d = h_ref.shape[1]
    mod = mod_ref[0]
    z = jnp.dot(y_ref[...], w_ref[...], preferred_element_type=F32)
    if glu:
        z = z[:, :d] * _sigmoid(z[:, d:])
    o_ref[...] = h_ref[...] + mod[:, 2 * d:3 * d] * z


def out_proj(y, h, mod3, layer, w, seg, glu):
    n_rows, d = h.shape
    dk, dn = w.shape
    return pl.pallas_call(
        functools.partial(_out_proj_kernel, glu=glu),
        grid=(n_rows // TM,),
        in_specs=[pl.BlockSpec((TM, dk), lambda i: (i, 0)),
                  pl.BlockSpec((TM, d), lambda i: (i, 0)),
                  _mod_spec(layer, d, seg // TM),
                  pl.BlockSpec((dk, dn), lambda i: (0, 0))],
        out_specs=pl.BlockSpec((TM, d), lambda i: (i, 0)),
        out_shape=jax.ShapeDtypeStruct((n_rows, d), F32),
        compiler_params=_cparams(("arbitrary",)),
        name="mixer_out_proj",
    )(y, h, mod3, w.astype(BF16))


def _s5_params(lam_re, lam_im, log_dt, b_re, b_im, c_re, c_im):
    ng, npst = lam_re.shape
    gpb = LANES // S5_GROUP_CH
    nj = ng // gpb
    dt = jnp.exp(log_dt.astype(F32))[:, None]
    mag = jnp.exp(lam_re * dt)
    a_re = mag * jnp.cos(lam_im * dt)
    a_im = mag * jnp.sin(lam_im * dt)
    den = lam_re * lam_re + lam_im * lam_im
    n_re = a_re - 1.0
    f_re = (n_re * lam_re + a_im * lam_im) / den
    f_im = (a_im * lam_re - n_re * lam_im) / den
    bb_re = f_re[..., None] * b_re - f_im[..., None] * b_im
    bb_im = f_re[..., None] * b_im + f_im[..., None] * b_re
    eye = jnp.eye(gpb, dtype=F32)

    def blockdiag_in(bb):
        x = bb.reshape(nj, gpb, npst, S5_GROUP_CH).transpose(0, 1, 3, 2)
        return (x[:, :, :, None, :] * eye[None, :, None, :, None]).reshape(nj, LANES, gpb * npst)

    def blockdiag_out(cc):
        x = cc.reshape(nj, gpb, S5_GROUP_CH, npst).transpose(0, 1, 3, 2)
        return (x[:, :, :, None, :] * eye[None, :, None, :, None]).reshape(nj, gpb * npst, LANES)

    b_in = jnp.concatenate([blockdiag_in(bb_re), blockdiag_in(bb_im)], axis=2)
    c_out = jnp.concatenate([blockdiag_out(c_re), blockdiag_out(-c_im)], axis=1)
    a = jnp.stack([a_re.reshape(nj, 1, gpb * npst), a_im.reshape(nj, 1, gpb * npst)], axis=1)
    return a, b_in.astype(BF16), c_out.astype(BF16)


def _s5_dir_kernel(x_ref, *rest, reverse, final):
    nj = rest[-1].shape[0]
    h_ref = rest[-1]
    u_refs = rest[-1 - nj:-1]
    if final:
        yf_ref, d_ref, b_ref, c_ref, a_ref, o_ref = rest[:-1 - nj]
    else:
        b_ref, c_ref, a_ref, o_ref = rest[:-1 - nj]
    sw = b_ref.shape[2] // 2
    steps = x_ref.shape[0] // SUBLANES

    @pl.when(pl.program_id(0) == 0)
    def _():
        h_ref[...] = jnp.zeros_like(h_ref)

    x = x_ref[...]
    xb = x.astype(BF16)
    for j, u_ref in enumerate(u_refs):
        cols = slice(j * LANES, (j + 1) * LANES)
        u_ref[...] = jnp.dot(xb[:, cols], b_ref[j], preferred_element_type=F32)
        a_re = jnp.broadcast_to(a_ref[j, 0], (SUBLANES, sw))
        a_im = jnp.broadcast_to(a_ref[j, 1], (SUBLANES, sw))
        h_re, h_im = h_ref[j, 0], h_ref[j, 1]
        for s in range(steps):
            t = (steps - 1 - s) if reverse else s
            rows = slice(t * SUBLANES, (t + 1) * SUBLANES)
            h_re, h_im = (a_re * h_re - a_im * h_im + u_ref[rows, :sw],
                          a_re * h_im + a_im * h_re + u_ref[rows, sw:])
            u_ref[rows, :sw] = h_re
            u_ref[rows, sw:] = h_im
        h_ref[j, 0] = h_re
        h_ref[j, 1] = h_im
        y = jnp.dot(u_ref[...].astype(BF16), c_ref[j], preferred_element_type=F32)
        if final:
            o_ref[:, cols] = _gelu(d_ref[:, cols] * x[:, cols] + yf_ref[:, cols] + y).astype(BF16)
        else:
            o_ref[:, cols] = y


def s5_layer(h, mod3, layer, g1, p, n_lat, n_ctx, nb, seg):
    d = h.shape[1]
    assert nb == SUBLANES
    xs = ln_proj(h, mod3, layer, g1, p['w_in'], seg)
    l_lat, l_ctx = n_lat // nb, n_ctx // nb
    to_tb = lambda x, l: x.reshape(nb, l, d).transpose(1, 0, 2)
    x_tb = jnp.concatenate([to_tb(xs[n_lat:], l_ctx), to_tb(xs[:n_lat], l_lat)], axis=0).reshape(-1, d)
    rows = S5_TB * nb
    n_blk, n_cblk = (l_ctx + l_lat) // S5_TB, l_ctx // S5_TB
    fwd = lambda i: (i, 0)
    rev = lambda i: (jnp.where(i < n_cblk, n_cblk - 1 - i, n_blk - 1 + n_cblk - i), 0)
    full = lambda a: pl.BlockSpec(a.shape, lambda i: (0,) * a.ndim)
    y = None
    for k, order in enumerate((fwd, rev)):
        a, b_in, c_out = _s5_params(p['lam_re'][k], p['lam_im'][k], p['log_dt'][k], p['b_re'][k], p['b_im'][k],
                                    p['c_re'][k], p['c_im'][k])
        final = k == 1
        blk = pl.BlockSpec((rows, d), order)
        args = [x_tb] + ([y, p['d'].reshape(1, d)] if final else []) + [b_in, c_out, a]
        specs = [blk] + ([blk, full(args[2])] if final else []) + [full(b_in), full(c_out), full(a)]
        y = pl.pallas_call(
            functools.partial(_s5_dir_kernel, reverse=bool(k), final=final),
            grid=(n_blk,),
            in_specs=specs,
            out_specs=blk,
            out_shape=jax.ShapeDtypeStruct(x_tb.shape, BF16 if final else F32),
            scratch_shapes=[pltpu.VMEM((rows, b_in.shape[2]), F32)] * b_in.shape[0]
            + [pltpu.VMEM((b_in.shape[0], 2, SUBLANES, b_in.shape[2] // 2), F32)],
            compiler_params=_cparams(("arbitrary",)),
            name="s5_scan_reverse" if final else "s5_scan_forward",
        )(*args)
    y = y.reshape(l_ctx + l_lat, nb, d)
    from_tb = lambda x: x.transpose(1, 0, 2).reshape(-1, d)
    y_rows = jnp.concatenate([from_tb(y[l_ctx:]), from_tb(y[:l_ctx])], axis=0)
    return out_proj(y_rows, h, mod3, layer, p['w_glu'], seg, glu=True)


def _mla_proj_kernel(h_ref, mod_ref, g1_ref, win_ref, gq_ref, gkv_ref, wuq_ref, wukv_ref,
                     gqn_ref, gkn_ref, cos_ref, sin_ref, q_ref, k_ref, v_ref):
    d = h_ref.shape[1]
    mod = mod_ref[0]
    n = _adaln(h_ref[...], g1_ref[...], mod[:, 0:d], mod[:, d:2 * d]).astype(BF16)
    z = jnp.dot(n, win_ref[...], preferred_element_type=F32)
    ql = _rms(z[:, :MLA_Q_RANK], gq_ref[...]).astype(BF16)
    r0 = MLA_Q_RANK + MLA_KV_RANK
    kvl = _rms(z[:, MLA_Q_RANK:r0], gkv_ref[...]).astype(BF16)
    pe, pe_sw = z[:, r0:r0 + LANES], z[:, r0 + LANES:r0 + 2 * LANES]
    qa = jnp.dot(ql, wuq_ref[...], preferred_element_type=F32)
    kv = jnp.dot(kvl, wukv_ref[...], preferred_element_type=F32)
    cos, sin = cos_ref[...], sin_ref[...]
    gqn, gkn = gqn_ref[...], gkn_ref[...]
    kr = (pe * gkn[1:2]) * cos + (pe_sw * gkn[2:3]) * sin
    pe_ss = jnp.sum(pe * pe, axis=-1, keepdims=True)
    hw = 2 * LANES
    qscale = 1.0 / math.sqrt(MLA_QK)
    for hd in range(MLA_HEADS):
        qn = qa[:, hd * hw:hd * hw + LANES]
        qr = qa[:, hd * hw + LANES:(hd + 1) * hw]
        qsw = qa[:, MLA_HEADS * hw + hd * LANES:MLA_HEADS * hw + (hd + 1) * LANES]
        ss = jnp.sum(qn * qn, axis=-1, keepdims=True) + jnp.sum(qr * qr, axis=-1, keepdims=True)
        rq = lax.rsqrt(ss * (1.0 / MLA_QK) + EPS) * qscale
        q_ref[:, hd * hw:hd * hw + LANES] = (qn * rq * gqn[0:1]).astype(BF16)
        q_ref[:, hd * hw + LANES:(hd + 1) * hw] = (rq * ((qr * gqn[1:2]) * cos + (qsw * gqn[2:3]) * sin)).astype(BF16)
        kn = kv[:, hd * LANES:(hd + 1) * LANES]
        rk = lax.rsqrt((jnp.sum(kn * kn, axis=-1, keepdims=True) + pe_ss) * (1.0 / MLA_QK) + EPS)
        k_ref[:, hd * hw:hd * hw + LANES] = (kn * rk * gkn[0:1]).astype(BF16)
        k_ref[:, hd * hw + LANES:(hd + 1) * hw] = (kr * rk).astype(BF16)
    v_ref[...] = kv[:, MLA_HEADS * LANES:].astype(BF16)


def _attn_kernel(q_ref, kc_ref, vc_ref, *rest, with_latent):
    if with_latent:
        kl_ref, vl_ref, o_ref = rest
    else:
        (o_ref,) = rest
    nt = (((1,), (1,)), ((), ()))
    q = q_ref[...]
    s1 = lax.dot_general(q, kc_ref[...], nt, preferred_element_type=F32)
    mx = jnp.max(s1, axis=-1, keepdims=True)
    if with_latent:
        s2 = lax.dot_general(q, kl_ref[...], nt, preferred_element_type=F32)
        mx = jnp.maximum(mx, jnp.max(s2, axis=-1, keepdims=True))
    p1 = jnp.exp(s1 - mx)
    den = jnp.sum(p1, axis=-1, keepdims=True)
    o = jnp.dot(p1.astype(BF16), vc_ref[...], preferred_element_type=F32)
    if with_latent:
        p2 = jnp.exp(s2 - mx)
        den = den + jnp.sum(p2, axis=-1, keepdims=True)
        o = o + jnp.dot(p2.astype(BF16), vl_ref[...], preferred_element_type=F32)
    o_ref[...] = (o / den).astype(BF16)


def _rope_tables(l_lat, n_rows_id):
    rows = l_lat // GRID_W
    row = jnp.repeat(jnp.arange(rows), GRID_W).astype(F32)
    col = jnp.tile(jnp.arange(GRID_W), rows).astype(F32)
    quarter = MLA_ROPE // 4
    inv_freq = ROPE_THETA ** (-jnp.arange(quarter, dtype=F32) / quarter)
    ang_r, ang_c = row[:, None] * inv_freq, col[:, None] * inv_freq
    cos = jnp.concatenate([jnp.cos(ang_r), jnp.cos(ang_r), jnp.cos(ang_c), jnp.cos(ang_c)], axis=1)
    sin = jnp.concatenate([-jnp.sin(ang_r), jnp.sin(ang_r), -jnp.sin(ang_c), jnp.sin(ang_c)], axis=1)
    pad = ((0, n_rows_id), (0, LANES - MLA_ROPE))
    cos = jnp.pad(cos, pad).at[l_lat:, :MLA_ROPE].set(1.0)
    return cos, jnp.pad(sin, pad)


def mla_layer(h, mod3, layer, g1, p, n_lat, n_ctx, nb, seg):
    n_rows, d = h.shape
    hw = 2 * LANES
    nh = MLA_HEADS
    l_lat, l_ctx = n_lat // nb, n_ctx // nb
    r0 = MLA_Q_RANK + MLA_KV_RANK
    swap = np.arange(MLA_ROPE)
    swap = np.where(swap % 32 < 16, swap + 16, swap - 16)
    lane_pad = lambda x: jnp.pad(x, ((0, 0),) * (x.ndim - 1) + ((0, LANES - x.shape[-1]),))
    w_pe = p['w_in'][:, r0:]
    w_in_ext = jnp.concatenate([p['w_in'][:, :r0], lane_pad(w_pe), lane_pad(w_pe[:, swap])], axis=1).astype(BF16)
    wq = p['w_uq'].reshape(MLA_Q_RANK, nh, MLA_QK)
    wq_main = jnp.pad(wq, ((0, 0), (0, 0), (0, hw - MLA_QK))).reshape(MLA_Q_RANK, nh * hw)
    wq_sw = lane_pad(wq[:, :, MLA_NOPE:][:, :, swap]).reshape(MLA_Q_RANK, nh * LANES)
    w_uq_ext = jnp.concatenate([wq_main, wq_sw], axis=1).astype(BF16)
    wkv = p['w_ukv'].reshape(MLA_KV_RANK, nh, 2, LANES).transpose(0, 2, 1, 3).reshape(MLA_KV_RANK, 2 * nh * LANES)
    gains = lambda g: jnp.zeros((SUBLANES, LANES), F32).at[0].set(g[:MLA_NOPE]).at[1, :MLA_ROPE].set(
        g[MLA_NOPE:]).at[2, :MLA_ROPE].set(g[MLA_NOPE:][swap])
    cos, sin = _rope_tables(l_lat, seg)
    t_lat = l_lat // TM
    tps = seg // TM
    rope_map = lambda i: (jnp.where(i < (n_lat // TM), i % t_lat, t_lat + i % tps), 0)
    full = lambda *shape: pl.BlockSpec(shape, lambda i: (0,) * len(shape))
    q, k, v = pl.pallas_call(
        _mla_proj_kernel,
        grid=(n_rows // TM,),
        in_specs=[pl.BlockSpec((TM, d), lambda i: (i, 0)), _mod_spec(layer, d, tps), full(1, d),
                  full(d, r0 + 2 * LANES), full(1, MLA_Q_RANK), full(1, MLA_KV_RANK),
                  full(MLA_Q_RANK, nh * (hw + LANES)), full(MLA_KV_RANK, 2 * nh * LANES),
                  full(SUBLANES, LANES), full(SUBLANES, LANES),
                  pl.BlockSpec((TM, LANES), rope_map), pl.BlockSpec((TM, LANES), rope_map)],
        out_specs=[pl.BlockSpec((TM, nh * hw), lambda i: (i, 0)), pl.BlockSpec((TM, nh * hw), lambda i: (i, 0)),
                   pl.BlockSpec((TM, nh * LANES), lambda i: (i, 0))],
        out_shape=[jax.ShapeDtypeStruct((n_rows, nh * hw), BF16), jax.ShapeDtypeStruct((n_rows, nh * hw), BF16),
                   jax.ShapeDtypeStruct((n_rows, nh * LANES), BF16)],
        compiler_params=_cparams(("arbitrary",)),
        name="mla_projection",
    )(h, mod3, g1.reshape(1, d), w_in_ext, p['g_q'].reshape(1, -1), p['g_kv'].reshape(1, -1), w_uq_ext,
      wkv.astype(BF16), gains(p['g_qn']), gains(p['g_kn']), cos, sin)

    tq = min(512, l_lat)
    cb = n_lat // l_ctx
    o_lat = pl.pallas_call(
        functools.partial(_attn_kernel, with_latent=True),
        grid=(nb, nh, l_lat // tq),
        in_specs=[pl.BlockSpec((tq, hw), lambda b, hd, i: (b * (l_lat // tq) + i, hd)),
                  pl.BlockSpec((l_ctx, hw), lambda b, hd, i: (cb + b, hd)),
                  pl.BlockSpec((l_ctx, LANES), lambda b, hd, i: (cb + b, hd)),
                  pl.BlockSpec((l_lat, hw), lambda b, hd, i: (b, hd)),
                  pl.BlockSpec((l_lat, LANES), lambda b, hd, i: (b, hd))],
        out_specs=pl.BlockSpec((tq, LANES), lambda b, hd, i: (b * (l_lat // tq) + i, hd)),
        out_shape=jax.ShapeDtypeStruct((n_lat, nh * LANES), BF16),
        compiler_params=_cparams(("arbitrary", "arbitrary", "arbitrary")),
        name="mla_attention_latent",
    )(q, k, v, k, v)
    o_ctx = pl.pallas_call(
        functools.partial(_attn_kernel, with_latent=False),
        grid=(nb, nh),
        in_specs=[pl.BlockSpec((l_ctx, hw), lambda b, hd: (cb + b, hd)),
                  pl.BlockSpec((l_ctx, hw), lambda b, hd: (cb + b, hd)),
                  pl.BlockSpec((l_ctx, LANES), lambda b, hd: (cb + b, hd))],
        out_specs=pl.BlockSpec((l_ctx, LANES), lambda b, hd: (b, hd)),
        out_shape=jax.ShapeDtypeStruct((n_ctx, nh * LANES), BF16),
        compiler_params=_cparams(("arbitrary", "arbitrary")),
        name="mla_attention_context",
    )(q, k, v)
    return out_proj(jnp.concatenate([o_lat, o_ctx], axis=0), h, mod3, layer, p['w_out'], seg, glu=False)


def _route_kernel(h_ref, mod_ref, g2_ref, wrt_ref, br_ref, n2_ref, pos_ref, gate_ref, cnt_ref):
    d = h_ref.shape[1]
    tm = h_ref.shape[0]
    ne = wrt_ref.shape[1]
    mod = mod_ref[0]
    n2 = _adaln(h_ref[...], g2_ref[...], mod[:, 3 * d:4 * d], mod[:, 4 * d:5 * d])
    n2_hi = n2.astype(BF16)
    n2_ref[...] = n2_hi
    n2_lo = (n2 - n2_hi.astype(F32)).astype(BF16)
    nt = (((1,), (1,)), ((), ()))
    w_hi, w_lo = wrt_ref[0], wrt_ref[1]
    logits = (lax.dot_general(w_hi, n2_hi, nt, preferred_element_type=F32)
              + (lax.dot_general(w_hi, n2_lo, nt, preferred_element_type=F32)
                 + lax.dot_general(w_lo, n2_hi, nt, preferred_element_type=F32))) + br_ref[...]
    eidx = lax.broadcasted_iota(I32, (ne, tm), 0).astype(F32)
    work = logits
    vals, idxs = [], []
    for _ in range(TOP_K):
        mx = jnp.max(work, axis=0, keepdims=True)
        idx = jnp.min(jnp.where(work == mx, eidx, float(ne)), axis=0, keepdims=True)
        vals.append(mx)
        idxs.append(idx)
        work = jnp.where(eidx == idx, -jnp.inf, work)
    ex = [jnp.exp(v - vals[0]) for v in vals]
    den = ex[0] + ex[1] + ex[2] + ex[3]
    sel = (eidx == idxs[0]) | (eidx == idxs[1]) | (eidx == idxs[2]) | (eidx == idxs[3])
    onehot = jnp.where(sel, 1.0, 0.0).astype(BF16)
    r = lax.broadcasted_iota(I32, (tm, tm), 0)
    c = lax.broadcasted_iota(I32, (tm, tm), 1)
    before = jnp.where(r < c, 1.0, 0.0).astype(BF16)
    rank_all = jnp.dot(onehot, before, preferred_element_type=F32)
    cnt = jnp.dot(onehot, jnp.ones((tm, tm), BF16), preferred_element_type=F32)
    cnt8 = jnp.floor((cnt + (SUBLANES - 1)) * (1.0 / SUBLANES)) * SUBLANES
    er = lax.broadcasted_iota(I32, (ne, LANES), 0)
    ec = lax.broadcasted_iota(I32, (ne, LANES), 1)
    lower = jnp.where(ec < er, 1.0, 0.0).astype(BF16)
    cnt8_p = jnp.concatenate([cnt8, jnp.zeros((LANES - ne, tm), F32)], axis=0).astype(BF16)
    off = jnp.dot(lower, cnt8_p, preferred_element_type=F32)
    base = rank_all + off
    for k in range(TOP_K):
        pos = jnp.sum(jnp.where(eidx == idxs[k], base, 0.0), axis=0, keepdims=True)
        pos_ref[k:k + 1, :] = pos.astype(I32)
        gate_ref[k:k + 1, :] = ex[k] / den
    cnt_ref[0] = cnt8[:, :LANES].astype(I32)


def moe_route(h, mod3, layer, g2, w_router, b_router, n_rows, seg):
    d = h.shape[1]
    ne = w_router.shape[1]
    nt = n_rows // TM
    wt = w_router.T
    wt_hi = wt.astype(BF16)
    wt_split = jnp.stack([wt_hi, (wt - wt_hi.astype(F32)).astype(BF16)])
    return pl.pallas_call(
        _route_kernel,
        grid=(nt,),
        in_specs=[pl.BlockSpec((TM, d), lambda i: (i, 0)), _mod_spec(layer, d, seg // TM),
                  pl.BlockSpec((1, d), lambda i: (0, 0)), pl.BlockSpec((2, ne, d), lambda i: (0, 0, 0)),
                  pl.BlockSpec((ne, 1), lambda i: (0, 0))],
        out_specs=[pl.BlockSpec((TM, d), lambda i: (i, 0)), pl.BlockSpec((TOP_K, TM), lambda i: (0, i)),
                   pl.BlockSpec((TOP_K, TM), lambda i: (0, i)), pl.BlockSpec((1, ne, LANES), lambda i: (i, 0, 0))],
        out_shape=[jax.ShapeDtypeStruct((n_rows, d), BF16), jax.ShapeDtypeStruct((TOP_K, n_rows), I32),
                   jax.ShapeDtypeStruct((TOP_K, n_rows), F32), jax.ShapeDtypeStruct((nt, ne, LANES), I32)],
        compiler_params=_cparams(("arbitrary",)),
        name="moe_route",
    )(h, mod3, g2.reshape(1, d), wt_split, b_router.reshape(ne, 1))


def _chunk_copies(tab_ref, tile, e, vm_ref, hbm_ref, sem, to_hbm):
    ne = N_EXPERTS
    stride = tab_ref.shape[0] // 3
    off = tab_ref[tile * ne + e]
    n8 = tab_ref[stride + tile * ne + e]
    dst = tab_ref[2 * stride + tile * ne + e]
    out = []
    for size in PIECES:
        bit = (size // SUBLANES).bit_length() - 1
        done = ((n8 >> (bit + 1)) << (bit + 1)) * SUBLANES
        lo = pl.multiple_of(off + done, SUBLANES)
        hi = pl.multiple_of(dst + done, SUBLANES)
        v, hb = vm_ref.at[pl.ds(lo, size), :], hbm_ref.at[pl.ds(hi, size), :]
        cp = pltpu.make_async_copy(v, hb, sem) if to_hbm else pltpu.make_async_copy(hb, v, sem)
        out.append((((n8 >> bit) & 1) == 1, cp))
    return out


def _for_each_copy(tab_ref, tile, vm_ref, hbm_ref, sem, to_hbm, action):
    small = PIECES.index(32)

    def body(e, carry):
        copies = _chunk_copies(tab_ref, tile, e, vm_ref, hbm_ref, sem, to_hbm)

        def run(items):
            for cond, cp in items:
                @pl.when(cond)
                def _():
                    action(cp)
        run(copies[small:])

        @pl.when(tab_ref[tab_ref.shape[0] // 3 + tile * N_EXPERTS + e] >= 2 * PIECES[small] // SUBLANES)
        def _():
            run(copies[:small])
        return carry
    lax.fori_loop(0, N_EXPERTS, body, 0)


def _wait_rows(n_rows, vm_ref, hbm_ref, sem, to_hbm):
    n8 = n_rows >> (SUBLANES.bit_length() - 1)
    size = SUBLANES
    while size * 2 <= vm_ref.shape[0]:
        size *= 2
    while size >= SUBLANES:
        v, hb = vm_ref.at[pl.ds(0, size), :], hbm_ref.at[pl.ds(0, size), :]
        cp = pltpu.make_async_copy(v, hb, sem) if to_hbm else pltpu.make_async_copy(hb, v, sem)

        @pl.when(((n8 >> ((size // SUBLANES).bit_length() - 1)) & 1) == 1)
        def _():
            cp.wait()
        size //= 2


def _dispatch_kernel(tab_ref, rows_ref, tail_ref, nb_ref, n2_ref, pos_ref, xb_ref, sorted_ref, zero_ref, sem, zsem):
    i = pl.program_id(0)
    last = pl.num_programs(0) - 1
    slot = i & 1
    cur = sorted_ref.at[slot]
    tm = n2_ref.shape[0]
    rmax = sorted_ref.shape[1]
    ridx = lax.broadcasted_iota(I32, (rmax, tm), 0)
    hit = ridx == pos_ref[0:1, :]
    for k in range(1, TOP_K):
        hit = hit | (ridx == pos_ref[k:k + 1, :])
    perm = jnp.where(hit, 1.0, 0.0).astype(BF16)
    cur[...] = jnp.dot(perm, n2_ref[...], preferred_element_type=F32)
    _for_each_copy(tab_ref, i, cur, xb_ref, sem.at[slot], True, lambda cp: cp.start())

    @pl.when(i > 0)
    def _():
        _wait_rows(rows_ref[i - 1], sorted_ref.at[1 - slot], xb_ref, sem.at[1 - slot], True)

    @pl.when(i == last)
    def _():
        _wait_rows(rows_ref[i], cur, xb_ref, sem.at[slot], True)
        zero_ref[...] = jnp.zeros_like(zero_ref)
        _for_each_copy(tail_ref, 0, zero_ref, xb_ref, zsem, True, lambda cp: cp.start())
        _for_each_copy(tail_ref, 0, zero_ref, xb_ref, zsem, True, lambda cp: cp.wait())

        zr = zero_ref.shape[0]

        def spare(b, carry):
            cp = pltpu.make_async_copy(zero_ref, xb_ref.at[pl.ds(pl.multiple_of(b * zr, zr), zr), :], zsem)
            cp.start()
            cp.wait()
            return carry
        lax.fori_loop(nb_ref[0] * (MOE_BM // zr), xb_ref.shape[0] // zr, spare, 0)


def _expert_kernel(be_ref, first_ref, nb_ref, x_ref, wg_ref, bg_ref, wu_ref, bu_ref, wd_ref, bd_ref, y_ref,
                   wg_s, wu_s, wd_s):
    b = pl.program_id(0)

    @pl.when(b < nb_ref[0])
    def _():
        @pl.when(first_ref[b] == 1)
        def _():
            wg_s[...] = wg_ref[0, 0].astype(BF16)
            wu_s[...] = wu_ref[0, 0].astype(BF16)
            wd_s[...] = wd_ref[0, 0].astype(BF16)

        for sb in range(x_ref.shape[0] // MOE_SUB):
            rows = slice(sb * MOE_SUB, (sb + 1) * MOE_SUB)
            x = x_ref[rows, :].astype(BF16)
            g = jnp.minimum(jnp.dot(x, wg_s[...], preferred_element_type=F32) + bg_ref[0], SWIGLU_LIMIT)
            u = jnp.clip(jnp.dot(x, wu_s[...], preferred_element_type=F32) + bu_ref[0], -SWIGLU_LIMIT, SWIGLU_LIMIT)
            a = (g * _sigmoid(SWIGLU_ALPHA * g) * (u + 1.0)).astype(BF16)
            y_ref[rows, :] = jnp.dot(a, wd_s[...], preferred_element_type=F32) + bd_ref[0]

    @pl.when(b >= nb_ref[0])
    def _():
        y_ref[...] = jnp.zeros_like(y_ref)


def _combine_kernel(tab_ref, rows_ref, pos_ref, gate_ref, h_ref, mod_ref, yb_ref, o_ref, ys_ref, sem):
    i = pl.program_id(0)
    slot = i & 1
    d = h_ref.shape[1]
    tm = h_ref.shape[0]
    rmax = ys_ref.shape[1]

    @pl.when(i == 0)
    def _():
        ys_ref[...] = jnp.zeros_like(ys_ref)
        _for_each_copy(tab_ref, 0, ys_ref.at[0], yb_ref, sem.at[0], False, lambda cp: cp.start())

    @pl.when(i + 1 < pl.num_programs(0))
    def _():
        _for_each_copy(tab_ref, i + 1, ys_ref.at[1 - slot], yb_ref, sem.at[1 - slot], False, lambda cp: cp.start())

    cidx = lax.broadcasted_iota(I32, (tm, rmax), 1)
    wt = jnp.where(cidx == pos_ref[:, 0:1], gate_ref[:, 0:1], 0.0)
    for k in range(1, TOP_K):
        wt = wt + jnp.where(cidx == pos_ref[:, k:k + 1], gate_ref[:, k:k + 1], 0.0)
    _wait_rows(rows_ref[i], ys_ref.at[slot], yb_ref, sem.at[slot], False)
    f = jnp.dot(wt.astype(BF16), ys_ref[slot].astype(BF16), preferred_element_type=F32)
    o_ref[...] = h_ref[...] + mod_ref[0][:, 5 * d:6 * d] * f


def _moe_plan(cnt8, n_blocks_max):
    nt, ne = cnt8.shape
    loc_off = jnp.cumsum(cnt8, axis=1) - cnt8
    used = jnp.sum(cnt8, axis=0)
    rows_e = (used + MOE_BM - 1) // MOE_BM * MOE_BM
    e_end = jnp.cumsum(rows_e)
    e_start = e_end - rows_e
    dst = e_start[None, :] + jnp.cumsum(cnt8, axis=0) - cnt8
    tab = jnp.stack([loc_off, cnt8 // SUBLANES, dst]).reshape(3 * nt * ne).astype(I32)
    tail = jnp.stack([jnp.zeros((ne,), I32), (rows_e - used) // SUBLANES, e_start + used]).reshape(3 * ne).astype(I32)
    nb = (e_end[-1] // MOE_BM).astype(I32)
    blk = jnp.arange(n_blocks_max, dtype=I32)
    blk_c = jnp.minimum(blk, nb - 1)
    blk_e = jnp.sum((e_end[None, :] <= (blk_c * MOE_BM)[:, None]).astype(I32), axis=1)
    blk_e = jnp.minimum(blk_e, ne - 1)
    first = (blk_c * MOE_BM == e_start[blk_e]).astype(I32)
    tile_rows = jnp.sum(cnt8, axis=1).astype(I32)
    return tab, tile_rows, tail, blk_e, first, nb.reshape(1)


def moe_layer(h, mod3, layer, g2, p, n_rows, seg):
    d = h.shape[1]
    ne = N_EXPERTS
    nt = n_rows // TM
    n2, pos, gate, cnt = moe_route(h, mod3, layer, g2, p['w_router'], p['b_router'], n_rows, seg)
    max_rows = n_rows * TOP_K + nt * ne * (SUBLANES - 1) + ne * (MOE_BM - 1)
    nbm = -(-max_rows // MOE_BM)
    tab, tile_rows, tail, blk_e, first, nb = _moe_plan(cnt[:, :, 0], nbm)

    xb = pl.pallas_call(
        _dispatch_kernel,
        grid_spec=pltpu.PrefetchScalarGridSpec(
            num_scalar_prefetch=4, grid=(nt,),
            in_specs=[pl.BlockSpec((TM, d), lambda i, *_: (i, 0)),
                      pl.BlockSpec((TOP_K, TM), lambda i, *_: (0, i))],
            out_specs=pl.BlockSpec(memory_space=pl.ANY),
            scratch_shapes=[pltpu.VMEM((2, MOE_RMAX, d), F32), pltpu.VMEM((MOE_BM, d), F32),
                            pltpu.SemaphoreType.DMA((2,)), pltpu.SemaphoreType.DMA(())]),
        out_shape=jax.ShapeDtypeStruct((nbm * MOE_BM, d), F32),
        compiler_params=_cparams(("arbitrary",)),
        name="moe_dispatch",
    )(tab, tile_rows, tail, nb, n2, pos)

    row_map = lambda b, be, fi, nbr: (jnp.maximum(jnp.minimum(b, nbr[0] - 1), 0), 0)
    out_map = lambda b, be, fi, nbr: (b, 0)
    w_map = lambda b, be, fi, nbr: (layer, be[b], 0, 0)
    b_map = lambda b, be, fi, nbr: (layer * ne + be[b], 0, 0)
    yb = pl.pallas_call(
        _expert_kernel,
        grid_spec=pltpu.PrefetchScalarGridSpec(
            num_scalar_prefetch=3, grid=(nbm,),
            in_specs=[pl.BlockSpec((MOE_BM, d), row_map),
                      pl.BlockSpec((1, 1, d, d), w_map), pl.BlockSpec((1, 1, d), b_map),
                      pl.BlockSpec((1, 1, d, d), w_map), pl.BlockSpec((1, 1, d), b_map),
                      pl.BlockSpec((1, 1, d, d), w_map), pl.BlockSpec((1, 1, d), b_map)],
            out_specs=pl.BlockSpec((MOE_BM, d), out_map),
            scratch_shapes=[pltpu.VMEM((d, d), BF16)] * 3),
        out_shape=jax.ShapeDtypeStruct((nbm * MOE_BM, d), F32),
        compiler_params=_cparams(("arbitrary",)),
        name="moe_experts",
    )(blk_e, first, nb, xb, p['w_gate'], p['b_gate'].reshape(-1, 1, d), p['w_up'], p['b_up'].reshape(-1, 1, d),
      p['w_down'], p['b_down'].reshape(-1, 1, d))

    return pl.pallas_call(
        _combine_kernel,
        grid_spec=pltpu.PrefetchScalarGridSpec(
            num_scalar_prefetch=2, grid=(nt,),
            in_specs=[pl.BlockSpec((TM, TOP_K), lambda i, *_: (i, 0)),
                      pl.BlockSpec((TM, TOP_K), lambda i, *_: (i, 0)),
                      pl.BlockSpec((TM, d), lambda i, *_: (i, 0)),
                      pl.BlockSpec((1, 1, 6 * d), lambda i, *_: (layer * 16 + i // (seg // TM), 0, 0)),
                      pl.BlockSpec(memory_space=pl.ANY)],
            out_specs=pl.BlockSpec((TM, d), lambda i, *_: (i, 0)),
            scratch_shapes=[pltpu.VMEM((2, MOE_RMAX, d), F32), pltpu.SemaphoreType.DMA((2,))]),
        out_shape=jax.ShapeDtypeStruct((n_rows, d), F32),
        compiler_params=_cparams(("arbitrary",)),
        name="moe_combine",
    )(tab, tile_rows, pos.T, gate.T, h, mod3, yb)


def kernel(x, c, ctx, c_ctx, w_ada, b_ada, g_norm1, g_norm2, sg_w_in, sg_g_v, sg_w_s, sg_b_s, sg_w_out,
           ssm_w_in, ssm_lam_re, ssm_lam_im, ssm_log_dt, ssm_b_re, ssm_b_im, ssm_c_re, ssm_c_im, ssm_d, ssm_w_glu,
           mla_w_in, mla_g_q, mla_g_kv, mla_w_uq, mla_w_ukv, mla_g_qn, mla_g_kn, mla_w_out,
           moe_w_router, moe_b_router, moe_w_gate, moe_b_gate, moe_w_up, moe_b_up, moe_w_down, moe_b_down):
    nb, l_lat, d = x.shape
    l_ctx = ctx.shape[1]
    depth = w_ada.shape[0]
    n_lat, n_ctx = nb * l_lat, nb * l_ctx
    seg = l_lat
    assert n_ctx <= seg and seg % TM == 0 and n_ctx % TM == 0
    mod3 = modulation(c, c_ctx, w_ada, b_ada)
    h = jnp.concatenate([x.reshape(n_lat, d), ctx.reshape(n_ctx, d)], axis=0)
    for i in range(depth):
        mixer, slot = i % N_MIXERS, i // N_MIXERS
        ctx_out = i < depth - 1
        n_rows = n_lat + n_ctx if ctx_out else n_lat
        if mixer == 0:
            hm = gmlp_layer(h, mod3, i, g_norm1[i], sg_w_in[slot], sg_g_v[slot], sg_w_s[slot], sg_b_s[slot],
                            sg_w_out[slot], n_rows, seg)
        elif mixer == 1:
            p = dict(w_in=ssm_w_in[slot], lam_re=ssm_lam_re[slot], lam_im=ssm_lam_im[slot], log_dt=ssm_log_dt[slot],
                     b_re=ssm_b_re[slot], b_im=ssm_b_im[slot], c_re=ssm_c_re[slot], c_im=ssm_c_im[slot],
                     d=ssm_d[slot], w_glu=ssm_w_glu[slot])
            hm = s5_layer(h, mod3, i, g_norm1[i], p, n_lat, n_ctx, nb, seg)[:n_rows]
        else:
            p = dict(w_in=mla_w_in[slot], g_q=mla_g_q[slot], g_kv=mla_g_kv[slot], w_uq=mla_w_uq[slot],
                     w_ukv=mla_w_ukv[slot], g_qn=mla_g_qn[slot], g_kn=mla_g_kn[slot], w_out=mla_w_out[slot])
            hm = mla_layer(h, mod3, i, g_norm1[i], p, n_lat, n_ctx, nb, seg)[:n_rows]
        pm = dict(w_router=moe_w_router[i], b_router=moe_b_router[i], w_gate=moe_w_gate, b_gate=moe_b_gate,
                  w_up=moe_w_up, b_up=moe_b_up, w_down=moe_w_down, b_down=moe_b_down)
        h = moe_layer(hm, mod3, i, g_norm2[i], pm, n_rows, seg)
    return h[:n_lat].reshape(nb, l_lat, d)
```

```python
import functools
import math

import jax
import jax.numpy as jnp
import numpy as np
from jax import lax
from jax.experimental import pallas as pl
from jax.experimental.pallas import tpu as pltpu

F32 = jnp.float32
BF16 = jnp.bfloat16
I32 = jnp.int32
EPS = 1e-6

N_MIXERS = 3
GRID_W = 64
CHUNK = 128
SG_GROUPS = 8
S5_GROUP_CH = 16
S5_STATE = 64
MLA_HEADS = 8
MLA_NOPE = 128
MLA_ROPE = 64
MLA_QK = MLA_NOPE + MLA_ROPE
MLA_V = 128
MLA_Q_RANK = 384
MLA_KV_RANK = 256
ROPE_THETA = 10000.0
N_EXPERTS = 32
TOP_K = 4
SWIGLU_LIMIT = 7.0
SWIGLU_ALPHA = 1.702

LANES = 128
SUBLANES = 8
VMEM_LIMIT = 56 * 1024 * 1024

TM = 256
TMK = 1024
SUB = 256
S5_TB = 32
ATTN_TQ = 2048
ATTN_QS = 256
MOE_BM = 512
MOE_SUB = 256
MOE_RMAX = TM * TOP_K + N_EXPERTS * SUBLANES
PIECES = (256, 128, 64, 32, 16, 8)


def _cparams(sem):
    return pltpu.CompilerParams(dimension_semantics=sem, vmem_limit_bytes=VMEM_LIMIT)


def _rms(x, g):
    ms = jnp.mean(x * x, axis=-1, keepdims=True)
    return x * lax.rsqrt(ms + EPS) * g


def _adaln(x, g, shift, scale):
    return _rms(x, g) * (1.0 + scale) + shift


def _gelu(x):
    return 0.5 * x * (1.0 + lax.erf(x * (1.0 / math.sqrt(2.0))))


def _sigmoid(x):
    return 1.0 / (1.0 + jnp.exp(-x))


def _mod_spec(layer, d, tiles_per_seg):
    return pl.BlockSpec((1, 1, 6 * d), lambda i: (layer * 16 + i // tiles_per_seg, 0, 0))


def _mod_kernel(s_ref, w_ref, b_ref, o_ref):
    s = s_ref[...]
    s = s * _sigmoid(s)
    o_ref[0] = jnp.dot(s, w_ref[0], preferred_element_type=F32, precision=lax.Precision.HIGHEST) + b_ref[0]


def modulation(c, c_ctx, w_ada, b_ada):
    depth, d, d6 = w_ada.shape
    nb = c.shape[0]
    s = jnp.zeros((16, d), F32).at[:nb].set(c).at[nb].set(c_ctx)
    tn = 1024
    out = pl.pallas_call(
        _mod_kernel,
        grid=(depth, d6 // tn),
        in_specs=[pl.BlockSpec((16, d), lambda l, j: (0, 0)),
                  pl.BlockSpec((1, d, tn), lambda l, j: (l, 0, j)),
                  pl.BlockSpec((1, 1, tn), lambda l, j: (l, 0, j))],
        out_specs=pl.BlockSpec((1, 16, tn), lambda l, j: (l, 0, j)),
        out_shape=jax.ShapeDtypeStruct((depth, 16, d6), F32),
        compiler_params=_cparams(("arbitrary", "arbitrary")),
        name="adaln_modulation",
    )(s, w_ada, b_ada.reshape(depth, 1, d6))
    return out.reshape(depth * 16, 1, d6)


def _gmlp_kernel(h_ref, mod_ref, g1_ref, win_ref, gv_ref, ws_ref, bs_ref, wout_ref, o_ref, gated_ref):
    d = h_ref.shape[1]
    mod = mod_ref[0]
    gd = d // SG_GROUPS
    for sb in range(h_ref.shape[0] // SUB):
        r0 = sb * SUB
        h = h_ref[r0:r0 + SUB, :]
        n = _adaln(h, g1_ref[...], mod[:, 0:d], mod[:, d:2 * d]).astype(BF16)
        z = _gelu(jnp.dot(n, win_ref[...], preferred_element_type=F32))
        u = z[:, :d]
        v = _rms(z[:, d:], gv_ref[...]).astype(BF16)
        for j in range(SUB // CHUNK):
            rows = slice(j * CHUNK, (j + 1) * CHUNK)
            out_rows = slice(r0 + j * CHUNK, r0 + (j + 1) * CHUNK)
            for g in range(SG_GROUPS):
                cols = slice(g * gd, (g + 1) * gd)
                sv = jnp.dot(ws_ref[g], v[rows, cols], preferred_element_type=F32) + bs_ref[g]
                gated_ref[out_rows, cols] = (u[rows, cols] * sv).astype(BF16)
        m = jnp.dot(gated_ref[r0:r0 + SUB, :], wout_ref[...], preferred_element_type=F32)
        o_ref[r0:r0 + SUB, :] = h + mod[:, 2 * d:3 * d] * m


def gmlp_layer(h, mod3, layer, g1, w_in, g_v, w_s, b_s, w_out, n_rows, seg):
    d = h.shape[1]
    gd = d // SG_GROUPS
    tm = min(TMK, seg)
    bs_b = jnp.broadcast_to(b_s[:, :, None], (SG_GROUPS, CHUNK, gd)).astype(F32)
    full = lambda *shape: pl.BlockSpec(shape, lambda i: (0,) * len(shape))
    return pl.pallas_call(
        _gmlp_kernel,
        grid=(n_rows // tm,),
        in_specs=[pl.BlockSpec((tm, d), lambda i: (i, 0)),
                  _mod_spec(layer, d, seg // tm),
                  full(1, d), full(d, 2 * d), full(1, d),
                  full(SG_GROUPS, CHUNK, CHUNK), full(SG_GROUPS, CHUNK, gd), full(d, d)],
        out_specs=pl.BlockSpec((tm, d), lambda i: (i, 0)),
        out_shape=jax.ShapeDtypeStruct((n_rows, d), F32),
        scratch_shapes=[pltpu.VMEM((tm, d), BF16)],
        compiler_params=_cparams(("arbitrary",)),
        name="gmlp_mixer",
    )(h, mod3, g1.reshape(1, d), w_in.astype(BF16), g_v.reshape(1, d), w_s.astype(BF16), bs_b,
      w_out.astype(BF16))


def _ln_proj_kernel(h_ref, mod_ref, g1_ref, w_ref, o_ref):
    d = h_ref.shape[1]
    mod = mod_ref[0]
    for sb in range(h_ref.shape[0] // SUB):
        rows = slice(sb * SUB, (sb + 1) * SUB)
        n = _adaln(h_ref[rows, :], g1_ref[...], mod[:, 0:d], mod[:, d:2 * d]).astype(BF16)
        o_ref[rows, :] = jnp.dot(n, w_ref[...], preferred_element_type=F32)


def ln_proj(h, mod3, layer, g1, w, seg):
    n_rows, d = h.shape
    dn = w.shape[1]
    tm = min(TMK, seg)
    return pl.pallas_call(
        _ln_proj_kernel,
        grid=(n_rows // tm,),
        in_specs=[pl.BlockSpec((tm, d), lambda i: (i, 0)),
                  _mod_spec(layer, d, seg // tm),
                  pl.BlockSpec((1, d), lambda i: (0, 0)),
                  pl.BlockSpec((d, dn), lambda i: (0, 0))],
        out_specs=pl.BlockSpec((tm, dn), lambda i: (i, 0)),
        out_shape=jax.ShapeDtypeStruct((n_rows, dn), F32),
        compiler_params=_cparams(("arbitrary",)),
        name="adaln_in_proj",
    )(h, mod3, g1.reshape(1, d), w.astype(BF16))


def _out_proj_kernel(y_ref, h_ref, mod_ref, w_ref, o_ref, *, glu):
    d = h_ref.shape[1]
    mod = mod_ref[0]
    for sb in range(h_ref.shape[0] // SUB):
        rows = slice(sb * SUB, (sb + 1) * SUB)
        z = jnp.dot(y_ref[rows, :], w_ref[...], preferred_element_type=F32)
        if glu:
            z = z[:, :d] * _sigmoid(z[:, d:])
        o_ref[rows, :] = h_ref[rows, :] + mod[:, 2 * d:3 * d] * z


def out_proj(y, h, mod3, layer, w, seg, glu):
    n_rows, d = h.shape
    dk, dn = w.shape
    tm = min(TMK, seg)
    return pl.pallas_call(
        functools.partial(_out_proj_kernel, glu=glu),
        grid=(n_rows // tm,),
        in_specs=[pl.BlockSpec((tm, dk), lambda i: (i, 0)),
                  pl.BlockSpec((tm, d), lambda i: (i, 0)),
                  _mod_spec(layer, d, seg // tm),
                  pl.BlockSpec((dk, dn), lambda i: (0, 0))],
        out_specs=pl.BlockSpec((tm, d), lambda i: (i, 0)),
        out_shape=jax.ShapeDtypeStruct((n_rows, d), F32),
        compiler_params=_cparams(("arbitrary",)),
        name="mixer_out_proj",
    )(y, h, mod3, w.astype(BF16))


def _s5_params(lam_re, lam_im, log_dt, b_re, b_im, c_re, c_im):
    ng, npst = lam_re.shape
    gpb = LANES // S5_GROUP_CH
    nj = ng // gpb
    dt = jnp.exp(log_dt.astype(F32))[:, None]
    mag = jnp.exp(lam_re * dt)
    a_re = mag * jnp.cos(lam_im * dt)
    a_im = mag * jnp.sin(lam_im * dt)
    den = lam_re * lam_re + lam_im * lam_im
    n_re = a_re - 1.0
    f_re = (n_re * lam_re + a_im * lam_im) / den
    f_im = (a_im * lam_re - n_re * lam_im) / den
    bb_re = f_re[..., None] * b_re - f_im[..., None] * b_im
    bb_im = f_re[..., None] * b_im + f_im[..., None] * b_re
    eye = jnp.eye(gpb, dtype=F32)

    def blockdiag_in(bb):
        x = bb.reshape(nj, gpb, npst, S5_GROUP_CH).transpose(0, 1, 3, 2)
        return (x[:, :, :, None, :] * eye[None, :, None, :, None]).reshape(nj, LANES, gpb * npst)

    def blockdiag_out(cc):
        x = cc.reshape(nj, gpb, S5_GROUP_CH, npst).transpose(0, 1, 3, 2)
        return (x[:, :, :, None, :] * eye[None, :, None, :, None]).reshape(nj, gpb * npst, LANES)

    b_in = jnp.concatenate([blockdiag_in(bb_re), blockdiag_in(bb_im)], axis=2)
    c_out = jnp.concatenate([blockdiag_out(c_re), blockdiag_out(-c_im)], axis=1)
    a = jnp.stack([a_re.reshape(nj, 1, gpb * npst), a_im.reshape(nj, 1, gpb * npst)], axis=1)
    return a, b_in.astype(BF16), c_out.astype(BF16)


def _s5_dir_kernel(x_ref, *rest, reverse, final):
    nj = rest[-1].shape[0]
    h_ref = rest[-1]
    u_refs = rest[-1 - nj:-1]
    if final:
        yf_ref, d_ref, b_ref, c_ref, a_ref, o_ref = rest[:-1 - nj]
    else:
        b_ref, c_ref, a_ref, o_ref = rest[:-1 - nj]
    sw = b_ref.shape[2] // 2
    steps = x_ref.shape[0] // SUBLANES

    @pl.when(pl.program_id(0) == 0)
    def _():
        h_ref[...] = jnp.zeros_like(h_ref)

    x = x_ref[...]
    xb = x.astype(BF16)
    for j, u_ref in enumerate(u_refs):
        cols = slice(j * LANES, (j + 1) * LANES)
        u_ref[...] = jnp.dot(xb[:, cols], b_ref[j], preferred_element_type=F32)
        a_re = jnp.broadcast_to(a_ref[j, 0], (SUBLANES, sw))
        a_im = jnp.broadcast_to(a_ref[j, 1], (SUBLANES, sw))
        h_re, h_im = h_ref[j, 0], h_ref[j, 1]
        for s in range(steps):
            t = (steps - 1 - s) if reverse else s
            rows = slice(t * SUBLANES, (t + 1) * SUBLANES)
            h_re, h_im = (a_re * h_re - a_im * h_im + u_ref[rows, :sw],
                          a_re * h_im + a_im * h_re + u_ref[rows, sw:])
            u_ref[rows, :sw] = h_re
            u_ref[rows, sw:] = h_im
        h_ref[j, 0] = h_re
        h_ref[j, 1] = h_im
        y = jnp.dot(u_ref[...].astype(BF16), c_ref[j], preferred_element_type=F32)
        if final:
            o_ref[:, cols] = _gelu(d_ref[:, cols] * x[:, cols] + yf_ref[:, cols] + y).astype(BF16)
        else:
            o_ref[:, cols] = y


def s5_layer(h, mod3, layer, g1, p, n_lat, n_ctx, nb, seg):
    d = h.shape[1]
    assert nb == SUBLANES
    xs = ln_proj(h, mod3, layer, g1, p['w_in'], seg)
    l_lat, l_ctx = n_lat // nb, n_ctx // nb
    to_tb = lambda x, l: x.reshape(nb, l, d).transpose(1, 0, 2)
    x_tb = jnp.concatenate([to_tb(xs[n_lat:], l_ctx), to_tb(xs[:n_lat], l_lat)], axis=0).reshape(-1, d)
    rows = S5_TB * nb
    n_blk, n_cblk = (l_ctx + l_lat) // S5_TB, l_ctx // S5_TB
    fwd = lambda i: (i, 0)
    rev = lambda i: (jnp.where(i < n_cblk, n_cblk - 1 - i, n_blk - 1 + n_cblk - i), 0)
    full = lambda a: pl.BlockSpec(a.shape, lambda i: (0,) * a.ndim)
    y = None
    for k, order in enumerate((fwd, rev)):
        a, b_in, c_out = _s5_params(p['lam_re'][k], p['lam_im'][k], p['log_dt'][k], p['b_re'][k], p['b_im'][k],
                                    p['c_re'][k], p['c_im'][k])
        final = k == 1
        blk = pl.BlockSpec((rows, d), order)
        args = [x_tb] + ([y, p['d'].reshape(1, d)] if final else []) + [b_in, c_out, a]
        specs = [blk] + ([blk, full(args[2])] if final else []) + [full(b_in), full(c_out), full(a)]
        y = pl.pallas_call(
            functools.partial(_s5_dir_kernel, reverse=bool(k), final=final),
            grid=(n_blk,),
            in_specs=specs,
            out_specs=blk,
            out_shape=jax.ShapeDtypeStruct(x_tb.shape, BF16 if final else F32),
            scratch_shapes=[pltpu.VMEM((rows, b_in.shape[2]), F32)] * b_in.shape[0]
            + [pltpu.VMEM((b_in.shape[0], 2, SUBLANES, b_in.shape[2] // 2), F32)],
            compiler_params=_cparams(("arbitrary",)),
            name="s5_scan_reverse" if final else "s5_scan_forward",
        )(*args)
    y = y.reshape(l_ctx + l_lat, nb, d)
    from_tb = lambda x: x.transpose(1, 0, 2).reshape(-1, d)
    y_rows = jnp.concatenate([from_tb(y[l_ctx:]), from_tb(y[:l_ctx])], axis=0)
    return out_proj(y_rows, h, mod3, layer, p['w_glu'], seg, glu=True)


def _mla_proj_kernel(h_ref, mod_ref, g1_ref, win_ref, gq_ref, gkv_ref, wuq_ref, wukv_ref,
                     gqn_ref, gkn_ref, cos_ref, sin_ref, q_ref, k_ref, v_ref):
    d = h_ref.shape[1]
    mod = mod_ref[0]
    r0 = MLA_Q_RANK + MLA_KV_RANK
    hw = 2 * LANES
    qscale = 1.0 / math.sqrt(MLA_QK)
    gqn, gkn = gqn_ref[...], gkn_ref[...]
    for sb in range(h_ref.shape[0] // SUB):
        rows = slice(sb * SUB, (sb + 1) * SUB)
        n = _adaln(h_ref[rows, :], g1_ref[...], mod[:, 0:d], mod[:, d:2 * d]).astype(BF16)
        z = jnp.dot(n, win_ref[...], preferred_element_type=F32)
        ql = _rms(z[:, :MLA_Q_RANK], gq_ref[...]).astype(BF16)
        kvl = _rms(z[:, MLA_Q_RANK:r0], gkv_ref[...]).astype(BF16)
        pe, pe_sw = z[:, r0:r0 + LANES], z[:, r0 + LANES:r0 + 2 * LANES]
        qa = jnp.dot(ql, wuq_ref[...], preferred_element_type=F32)
        kv = jnp.dot(kvl, wukv_ref[...], preferred_element_type=F32)
        cos, sin = cos_ref[rows, :], sin_ref[rows, :]
        kr = (pe * gkn[1:2]) * cos + (pe_sw * gkn[2:3]) * sin
        pe_ss = jnp.sum(pe * pe, axis=-1, keepdims=True)
        for hd in range(MLA_HEADS):
            qn = qa[:, hd * hw:hd * hw + LANES]
            qr = qa[:, hd * hw + LANES:(hd + 1) * hw]
            qsw = qa[:, MLA_HEADS * hw + hd * LANES:MLA_HEADS * hw + (hd + 1) * LANES]
            ss = jnp.sum(qn * qn, axis=-1, keepdims=True) + jnp.sum(qr * qr, axis=-1, keepdims=True)
            rq = lax.rsqrt(ss * (1.0 / MLA_QK) + EPS) * qscale
            q_ref[rows, hd * hw:hd * hw + LANES] = (qn * rq * gqn[0:1]).astype(BF16)
            q_ref[rows, hd * hw + LANES:(hd + 1) * hw] = (
                rq * ((qr * gqn[1:2]) * cos + (qsw * gqn[2:3]) * sin)).astype(BF16)
            kn = kv[:, hd * LANES:(hd + 1) * LANES]
            rk = lax.rsqrt((jnp.sum(kn * kn, axis=-1, keepdims=True) + pe_ss) * (1.0 / MLA_QK) + EPS)
            k_ref[rows, hd * hw:hd * hw + LANES] = (kn * rk * gkn[0:1]).astype(BF16)
            k_ref[rows, hd * hw + LANES:(hd + 1) * hw] = (kr * rk).astype(BF16)
        v_ref[rows, :] = kv[:, MLA_HEADS * LANES:].astype(BF16)


def _attn_kernel(q_ref, kc_ref, vc_ref, *rest, with_latent):
    if with_latent:
        kl_ref, vl_ref, o_ref = rest
    else:
        (o_ref,) = rest
    nt = (((1,), (1,)), ((), ()))
    tq = q_ref.shape[0]
    qs = min(ATTN_QS, tq)
    for qi in range(tq // qs):
        rows = slice(qi * qs, (qi + 1) * qs)
        q = q_ref[rows, :]
        s1 = lax.dot_general(q, kc_ref[...], nt, preferred_element_type=F32)
        mx = jnp.max(s1, axis=-1, keepdims=True)
        if with_latent:
            s2 = lax.dot_general(q, kl_ref[...], nt, preferred_element_type=F32)
            mx = jnp.maximum(mx, jnp.max(s2, axis=-1, keepdims=True))
        p1 = jnp.exp(s1 - mx)
        den = jnp.sum(p1, axis=-1, keepdims=True)
        o = jnp.dot(p1.astype(BF16), vc_ref[...], preferred_element_type=F32)
        if with_latent:
            p2 = jnp.exp(s2 - mx)
            den = den + jnp.sum(p2, axis=-1, keepdims=True)
            o = o + jnp.dot(p2.astype(BF16), vl_ref[...], preferred_element_type=F32)
        o_ref[rows, :] = (o / den).astype(BF16)


def _rope_tables(l_lat, n_rows_id):
    rows = l_lat // GRID_W
    row = jnp.repeat(jnp.arange(rows), GRID_W).astype(F32)
    col = jnp.tile(jnp.arange(GRID_W), rows).astype(F32)
    quarter = MLA_ROPE // 4
    inv_freq = ROPE_THETA ** (-jnp.arange(quarter, dtype=F32) / quarter)
    ang_r, ang_c = row[:, None] * inv_freq, col[:, None] * inv_freq
    cos = jnp.concatenate([jnp.cos(ang_r), jnp.cos(ang_r), jnp.cos(ang_c), jnp.cos(ang_c)], axis=1)
    sin = jnp.concatenate([-jnp.sin(ang_r), jnp.sin(ang_r), -jnp.sin(ang_c), jnp.sin(ang_c)], axis=1)
    pad = ((0, n_rows_id), (0, LANES - MLA_ROPE))
    cos = jnp.pad(cos, pad).at[l_lat:, :MLA_ROPE].set(1.0)
    return cos, jnp.pad(sin, pad)


def mla_layer(h, mod3, layer, g1, p, n_lat, n_ctx, nb, seg):
    n_rows, d = h.shape
    hw = 2 * LANES
    nh = MLA_HEADS
    l_lat, l_ctx = n_lat // nb, n_ctx // nb
    r0 = MLA_Q_RANK + MLA_KV_RANK
    swap = np.arange(MLA_ROPE)
    swap = np.where(swap % 32 < 16, swap + 16, swap - 16)
    lane_pad = lambda x: jnp.pad(x, ((0, 0),) * (x.ndim - 1) + ((0, LANES - x.shape[-1]),))
    w_pe = p['w_in'][:, r0:]
    w_in_ext = jnp.concatenate([p['w_in'][:, :r0], lane_pad(w_pe), lane_pad(w_pe[:, swap])], axis=1).astype(BF16)
    wq = p['w_uq'].reshape(MLA_Q_RANK, nh, MLA_QK)
    wq_main = jnp.pad(wq, ((0, 0), (0, 0), (0, hw - MLA_QK))).reshape(MLA_Q_RANK, nh * hw)
    wq_sw = lane_pad(wq[:, :, MLA_NOPE:][:, :, swap]).reshape(MLA_Q_RANK, nh * LANES)
    w_uq_ext = jnp.concatenate([wq_main, wq_sw], axis=1).astype(BF16)
    wkv = p['w_ukv'].reshape(MLA_KV_RANK, nh, 2, LANES).transpose(0, 2, 1, 3).reshape(MLA_KV_RANK, 2 * nh * LANES)
    gains = lambda g: jnp.zeros((SUBLANES, LANES), F32).at[0].set(g[:MLA_NOPE]).at[1, :MLA_ROPE].set(
        g[MLA_NOPE:]).at[2, :MLA_ROPE].set(g[MLA_NOPE:][swap])
    cos, sin = _rope_tables(l_lat, seg)
    tm = min(TMK // 2, seg)
    t_lat = l_lat // tm
    tps = seg // tm
    rope_map = lambda i: (jnp.where(i < (n_lat // tm), i % t_lat, t_lat + i % tps), 0)
    full = lambda *shape: pl.BlockSpec(shape, lambda i: (0,) * len(shape))
    q, k, v = pl.pallas_call(
        _mla_proj_kernel,
        grid=(n_rows // tm,),
        in_specs=[pl.BlockSpec((tm, d), lambda i: (i, 0)), _mod_spec(layer, d, tps), full(1, d),
                  full(d, r0 + 2 * LANES), full(1, MLA_Q_RANK), full(1, MLA_KV_RANK),
                  full(MLA_Q_RANK, nh * (hw + LANES)), full(MLA_KV_RANK, 2 * nh * LANES),
                  full(SUBLANES, LANES), full(SUBLANES, LANES),
                  pl.BlockSpec((tm, LANES), rope_map), pl.BlockSpec((tm, LANES), rope_map)],
        out_specs=[pl.BlockSpec((tm, nh * hw), lambda i: (i, 0)), pl.BlockSpec((tm, nh * hw), lambda i: (i, 0)),
                   pl.BlockSpec((tm, nh * LANES), lambda i: (i, 0))],
        out_shape=[jax.ShapeDtypeStruct((n_rows, nh * hw), BF16), jax.ShapeDtypeStruct((n_rows, nh * hw), BF16),
                   jax.ShapeDtypeStruct((n_rows, nh * LANES), BF16)],
        compiler_params=_cparams(("arbitrary",)),
        name="mla_projection",
    )(h, mod3, g1.reshape(1, d), w_in_ext, p['g_q'].reshape(1, -1), p['g_kv'].reshape(1, -1), w_uq_ext,
      wkv.astype(BF16), gains(p['g_qn']), gains(p['g_kn']), cos, sin)

    tq = min(ATTN_TQ, l_lat)
    cb = n_lat // l_ctx
    o_lat = pl.pallas_call(
        functools.partial(_attn_kernel, with_latent=True),
        grid=(nb, nh, l_lat // tq),
        in_specs=[pl.BlockSpec((tq, hw), lambda b, hd, i: (b * (l_lat // tq) + i, hd)),
                  pl.BlockSpec((l_ctx, hw), lambda b, hd, i: (cb + b, hd)),
                  pl.BlockSpec((l_ctx, LANES), lambda b, hd, i: (cb + b, hd)),
                  pl.BlockSpec((l_lat, hw), lambda b, hd, i: (b, hd)),
                  pl.BlockSpec((l_lat, LANES), lambda b, hd, i: (b, hd))],
        out_specs=pl.BlockSpec((tq, LANES), lambda b, hd, i: (b * (l_lat // tq) + i, hd)),
        out_shape=jax.ShapeDtypeStruct((n_lat, nh * LANES), BF16),
        compiler_params=_cparams(("arbitrary", "arbitrary", "arbitrary")),
        name="mla_attention_latent",
    )(q, k, v, k, v)
    o_ctx = pl.pallas_call(
        functools.partial(_attn_kernel, with_latent=False),
        grid=(nb, nh),
        in_specs=[pl.BlockSpec((l_ctx, hw), lambda b, hd: (cb + b, hd)),
                  pl.BlockSpec((l_ctx, hw), lambda b, hd: (cb + b, hd)),
                  pl.BlockSpec((l_ctx, LANES), lambda b, hd: (cb + b, hd))],
        out_specs=pl.BlockSpec((l_ctx, LANES), lambda b, hd: (b, hd)),
        out_shape=jax.ShapeDtypeStruct((n_ctx, nh * LANES), BF16),
        compiler_params=_cparams(("arbitrary", "arbitrary")),
        name="mla_attention_context",
    )(q, k, v)
    return out_proj(jnp.concatenate([o_lat, o_ctx], axis=0), h, mod3, layer, p['w_out'], seg, glu=False)


def _route_kernel(h_ref, mod_ref, g2_ref, wrt_ref, br_ref, n2_ref, pos_ref, gate_ref, cnt_ref):
    d = h_ref.shape[1]
    tm = h_ref.shape[0]
    ne = wrt_ref.shape[1]
    mod = mod_ref[0]
    n2 = _adaln(h_ref[...], g2_ref[...], mod[:, 3 * d:4 * d], mod[:, 4 * d:5 * d])
    n2_hi = n2.astype(BF16)
    n2_ref[...] = n2_hi
    n2_lo = (n2 - n2_hi.astype(F32)).astype(BF16)
    nt = (((1,), (1,)), ((), ()))
    w_hi, w_lo = wrt_ref[0], wrt_ref[1]
    logits = (lax.dot_general(w_hi, n2_hi, nt, preferred_element_type=F32)
              + (lax.dot_general(w_hi, n2_lo, nt, preferred_element_type=F32)
                 + lax.dot_general(w_lo, n2_hi, nt, preferred_element_type=F32))) + br_ref[...]
    eidx = lax.broadcasted_iota(I32, (ne, tm), 0).astype(F32)
    work = logits
    vals, idxs = [], []
    for _ in range(TOP_K):
        mx = jnp.max(work, axis=0, keepdims=True)
        idx = jnp.min(jnp.where(work == mx, eidx, float(ne)), axis=0, keepdims=True)
        vals.append(mx)
        idxs.append(idx)
        work = jnp.where(eidx == idx, -jnp.inf, work)
    ex = [jnp.exp(v - vals[0]) for v in vals]
    den = ex[0] + ex[1] + ex[2] + ex[3]
    sel = (eidx == idxs[0]) | (eidx == idxs[1]) | (eidx == idxs[2]) | (eidx == idxs[3])
    onehot = jnp.where(sel, 1.0, 0.0).astype(BF16)
    r = lax.broadcasted_iota(I32, (tm, tm), 0)
    c = lax.broadcasted_iota(I32, (tm, tm), 1)
    before = jnp.where(r < c, 1.0, 0.0).astype(BF16)
    rank_all = jnp.dot(onehot, before, preferred_element_type=F32)
    cnt = jnp.dot(onehot, jnp.ones((tm, tm), BF16), preferred_element_type=F32)
    cnt8 = jnp.floor((cnt + (SUBLANES - 1)) * (1.0 / SUBLANES)) * SUBLANES
    er = lax.broadcasted_iota(I32, (ne, LANES), 0)
    ec = lax.broadcasted_iota(I32, (ne, LANES), 1)
    lower = jnp.where(ec < er, 1.0, 0.0).astype(BF16)
    cnt8_p = jnp.concatenate([cnt8, jnp.zeros((LANES - ne, tm), F32)], axis=0).astype(BF16)
    off = jnp.dot(lower, cnt8_p, preferred_element_type=F32)
    base = rank_all + off
    for k in range(TOP_K):
        pos = jnp.sum(jnp.where(eidx == idxs[k], base, 0.0), axis=0, keepdims=True)
        pos_ref[k:k + 1, :] = pos.astype(I32)
        gate_ref[k:k + 1, :] = ex[k] / den
    cnt_ref[0] = cnt8[:, :LANES].astype(I32)


def moe_route(h, mod3, layer, g2, w_router, b_router, n_rows, seg):
    d = h.shape[1]
    ne = w_router.shape[1]
    nt = n_rows // TM
    wt = w_router.T
    wt_hi = wt.astype(BF16)
    wt_split = jnp.stack([wt_hi, (wt - wt_hi.astype(F32)).astype(BF16)])
    return pl.pallas_call(
        _route_kernel,
        grid=(nt,),
        in_specs=[pl.BlockSpec((TM, d), lambda i: (i, 0)), _mod_spec(layer, d, seg // TM),
                  pl.BlockSpec((1, d), lambda i: (0, 0)), pl.BlockSpec((2, ne, d), lambda i: (0, 0, 0)),
                  pl.BlockSpec((ne, 1), lambda i: (0, 0))],
        out_specs=[pl.BlockSpec((TM, d), lambda i: (i, 0)), pl.BlockSpec((TOP_K, TM), lambda i: (0, i)),
                   pl.BlockSpec((TOP_K, TM), lambda i: (0, i)), pl.BlockSpec((1, ne, LANES), lambda i: (i, 0, 0))],
        out_shape=[jax.ShapeDtypeStruct((n_rows, d), BF16), jax.ShapeDtypeStruct((TOP_K, n_rows), I32),
                   jax.ShapeDtypeStruct((TOP_K, n_rows), F32), jax.ShapeDtypeStruct((nt, ne, LANES), I32)],
        compiler_params=_cparams(("arbitrary",)),
        name="moe_route",
    )(h, mod3, g2.reshape(1, d), wt_split, b_router.reshape(ne, 1))


def _chunk_copies(tab_ref, tile, e, vm_ref, hbm_ref, sem, to_hbm):
    ne = N_EXPERTS
    stride = tab_ref.shape[0] // 3
    off = tab_ref[tile * ne + e]
    n8 = tab_ref[stride + tile * ne + e]
    dst = tab_ref[2 * stride + tile * ne + e]
    out = []
    for size in PIECES:
        bit = (size // SUBLANES).bit_length() - 1
        done = ((n8 >> (bit + 1)) << (bit + 1)) * SUBLANES
        lo = pl.multiple_of(off + done, SUBLANES)
        hi = pl.multiple_of(dst + done, SUBLANES)
        v, hb = vm_ref.at[pl.ds(lo, size), :], hbm_ref.at[pl.ds(hi, size), :]
        cp = pltpu.make_async_copy(v, hb, sem) if to_hbm else pltpu.make_async_copy(hb, v, sem)
        out.append((((n8 >> bit) & 1) == 1, cp))
    return out


def _for_each_copy(tab_ref, tile, vm_ref, hbm_ref, sem, to_hbm, action):
    small = PIECES.index(32)

    def body(e, carry):
        copies = _chunk_copies(tab_ref, tile, e, vm_ref, hbm_ref, sem, to_hbm)

        def run(items):
            for cond, cp in items:
                @pl.when(cond)
                def _():
                    action(cp)
        run(copies[small:])

        @pl.when(tab_ref[tab_ref.shape[0] // 3 + tile * N_EXPERTS + e] >= 2 * PIECES[small] // SUBLANES)
        def _():
            run(copies[:small])
        return carry
    lax.fori_loop(0, N_EXPERTS, body, 0)


def _wait_rows(n_rows, vm_ref, hbm_ref, sem, to_hbm):
    n8 = n_rows >> (SUBLANES.bit_length() - 1)
    size = SUBLANES
    while size * 2 <= vm_ref.shape[0]:
        size *= 2
    while size >= SUBLANES:
        v, hb = vm_ref.at[pl.ds(0, size), :], hbm_ref.at[pl.ds(0, size), :]
        cp = pltpu.make_async_copy(v, hb, sem) if to_hbm else pltpu.make_async_copy(hb, v, sem)

        @pl.when(((n8 >> ((size // SUBLANES).bit_length() - 1)) & 1) == 1)
        def _():
            cp.wait()
        size //= 2


def _dispatch_kernel(tab_ref, rows_ref, tail_ref, nb_ref, n2_ref, pos_ref, xb_ref, sorted_ref, zero_ref, sem, zsem):
    i = pl.program_id(0)
    last = pl.num_programs(0) - 1
    slot = i & 1
    cur = sorted_ref.at[slot]
    tm = n2_ref.shape[0]
    rmax = sorted_ref.shape[1]
    ridx = lax.broadcasted_iota(I32, (rmax, tm), 0)
    hit = ridx == pos_ref[0:1, :]
    for k in range(1, TOP_K):
        hit = hit | (ridx == pos_ref[k:k + 1, :])
    perm = jnp.where(hit, 1.0, 0.0).astype(BF16)
    cur[...] = jnp.dot(perm, n2_ref[...], preferred_element_type=F32)
    _for_each_copy(tab_ref, i, cur, xb_ref, sem.at[slot], True, lambda cp: cp.start())

    @pl.when(i > 0)
    def _():
        _wait_rows(rows_ref[i - 1], sorted_ref.at[1 - slot], xb_ref, sem.at[1 - slot], True)

    @pl.when(i == last)
    def _():
        _wait_rows(rows_ref[i], cur, xb_ref, sem.at[slot], True)
        zero_ref[...] = jnp.zeros_like(zero_ref)
        _for_each_copy(tail_ref, 0, zero_ref, xb_ref, zsem, True, lambda cp: cp.start())
        _for_each_copy(tail_ref, 0, zero_ref, xb_ref, zsem, True, lambda cp: cp.wait())

        zr = zero_ref.shape[0]

        def spare(b, carry):
            cp = pltpu.make_async_copy(zero_ref, xb_ref.at[pl.ds(pl.multiple_of(b * zr, zr), zr), :], zsem)
            cp.start()
            cp.wait()
            return carry
        lax.fori_loop(nb_ref[0] * (MOE_BM // zr), xb_ref.shape[0] // zr, spare, 0)


def _expert_kernel(be_ref, first_ref, nb_ref, x_ref, wg_ref, bg_ref, wu_ref, bu_ref, wd_ref, bd_ref, y_ref,
                   wg_s, wu_s, wd_s):
    b = pl.program_id(0)

    @pl.when(b < nb_ref[0])
    def _():
        @pl.when((first_ref[b] & 1) == 1)
        def _():
            wg_s[...] = wg_ref[0, 0].astype(BF16)
            wu_s[...] = wu_ref[0, 0].astype(BF16)
            wd_s[...] = wd_ref[0, 0].astype(BF16)

        def ffn(rows):
            x = x_ref[rows, :].astype(BF16)
            g = jnp.minimum(jnp.dot(x, wg_s[...], preferred_element_type=F32) + bg_ref[0], SWIGLU_LIMIT)
            u = jnp.clip(jnp.dot(x, wu_s[...], preferred_element_type=F32) + bu_ref[0], -SWIGLU_LIMIT, SWIGLU_LIMIT)
            a = (g * _sigmoid(SWIGLU_ALPHA * g) * (u + 1.0)).astype(BF16)
            y_ref[rows, :] = jnp.dot(a, wd_s[...], preferred_element_type=F32) + bd_ref[0]

        n_sub = x_ref.shape[0] // MOE_SUB
        live = first_ref[b] >> 1

        @pl.when(live == n_sub)
        def _():
            for sb in range(n_sub):
                ffn(slice(sb * MOE_SUB, (sb + 1) * MOE_SUB))

        @pl.when(live < n_sub)
        def _():
            for sb in range(n_sub):
                rows = slice(sb * MOE_SUB, (sb + 1) * MOE_SUB)

                @pl.when(sb < live)
                def _():
                    ffn(rows)

                @pl.when(sb >= live)
                def _():
                    y_ref[rows, :] = jnp.zeros((MOE_SUB, y_ref.shape[1]), F32)

    @pl.when(b >= nb_ref[0])
    def _():
        y_ref[...] = jnp.zeros_like(y_ref)


def _combine_kernel(tab_ref, rows_ref, pos_ref, gate_ref, h_ref, mod_ref, yb_ref, o_ref, ys_ref, sem):
    i = pl.program_id(0)
    slot = i & 1
    d = h_ref.shape[1]
    tm = h_ref.shape[0]
    rmax = ys_ref.shape[1]

    @pl.when(i == 0)
    def _():
        ys_ref[...] = jnp.zeros_like(ys_ref)
        _for_each_copy(tab_ref, 0, ys_ref.at[0], yb_ref, sem.at[0], False, lambda cp: cp.start())

    @pl.when(i + 1 < pl.num_programs(0))
    def _():
        _for_each_copy(tab_ref, i + 1, ys_ref.at[1 - slot], yb_ref, sem.at[1 - slot], False, lambda cp: cp.start())

    cidx = lax.broadcasted_iota(I32, (tm, rmax), 1)
    wt = jnp.where(cidx == pos_ref[:, 0:1], gate_ref[:, 0:1], 0.0)
    for k in range(1, TOP_K):
        wt = wt + jnp.where(cidx == pos_ref[:, k:k + 1], gate_ref[:, k:k + 1], 0.0)
    _wait_rows(rows_ref[i], ys_ref.at[slot], yb_ref, sem.at[slot], False)
    f = jnp.dot(wt.astype(BF16), ys_ref[slot].astype(BF16), preferred_element_type=F32)
    o_ref[...] = h_ref[...] + mod_ref[0][:, 5 * d:6 * d] * f


def _moe_plan(cnt8, n_blocks_max):
    nt, ne = cnt8.shape
    loc_off = jnp.cumsum(cnt8, axis=1) - cnt8
    used = jnp.sum(cnt8, axis=0)
    rows_e = (used + MOE_BM - 1) // MOE_BM * MOE_BM
    e_end = jnp.cumsum(rows_e)
    e_start = e_end - rows_e
    dst = e_start[None, :] + jnp.cumsum(cnt8, axis=0) - cnt8
    tab = jnp.stack([loc_off, cnt8 // SUBLANES, dst]).reshape(3 * nt * ne).astype(I32)
    tail = jnp.stack([jnp.zeros((ne,), I32), (rows_e - used) // SUBLANES, e_start + used]).reshape(3 * ne).astype(I32)
    nb = (e_end[-1] // MOE_BM).astype(I32)
    blk = jnp.arange(n_blocks_max, dtype=I32)
    blk_c = jnp.minimum(blk, nb - 1)
    blk_e = jnp.sum((e_end[None, :] <= (blk_c * MOE_BM)[:, None]).astype(I32), axis=1)
    blk_e = jnp.minimum(blk_e, ne - 1)
    first = (blk_c * MOE_BM == e_start[blk_e]).astype(I32)
    live = jnp.clip(e_start[blk_e] + used[blk_e] - blk_c * MOE_BM, 0, MOE_BM)
    first = first + 2 * ((live + MOE_SUB - 1) // MOE_SUB).astype(I32)
    tile_rows = jnp.sum(cnt8, axis=1).astype(I32)
    return tab, tile_rows, tail, blk_e, first, nb.reshape(1)


def moe_layer(h, mod3, layer, g2, p, n_rows, seg):
    d = h.shape[1]
    ne = N_EXPERTS
    nt = n_rows // TM
    n2, pos, gate, cnt = moe_route(h, mod3, layer, g2, p['w_router'], p['b_router'], n_rows, seg)
    max_rows = n_rows * TOP_K + nt * ne * (SUBLANES - 1) + ne * (MOE_BM - 1)
    nbm = -(-max_rows // MOE_BM)
    tab, tile_rows, tail, blk_e, first, nb = _moe_plan(cnt[:, :, 0], nbm)

    xb = pl.pallas_call(
        _dispatch_kernel,
        grid_spec=pltpu.PrefetchScalarGridSpec(
            num_scalar_prefetch=4, grid=(nt,),
            in_specs=[pl.BlockSpec((TM, d), lambda i, *_: (i, 0)),
                      pl.BlockSpec((TOP_K, TM), lambda i, *_: (0, i))],
            out_specs=pl.BlockSpec(memory_space=pl.ANY),
            scratch_shapes=[pltpu.VMEM((2, MOE_RMAX, d), F32), pltpu.VMEM((MOE_BM, d), F32),
                            pltpu.SemaphoreType.DMA((2,)), pltpu.SemaphoreType.DMA(())]),
        out_shape=jax.ShapeDtypeStruct((nbm * MOE_BM, d), F32),
        compiler_params=_cparams(("arbitrary",)),
        name="moe_dispatch",
    )(tab, tile_rows, tail, nb, n2, pos)

    row_map = lambda b, be, fi, nbr: (jnp.maximum(jnp.minimum(b, nbr[0] - 1), 0), 0)
    out_map = lambda b, be, fi, nbr: (b, 0)
    w_map = lambda b, be, fi, nbr: (layer, be[b], 0, 0)
    b_map = lambda b, be, fi, nbr: (layer * ne + be[b], 0, 0)
    yb = pl.pallas_call(
        _expert_kernel,
        grid_spec=pltpu.PrefetchScalarGridSpec(
            num_scalar_prefetch=3, grid=(nbm,),
            in_specs=[pl.BlockSpec((MOE_BM, d), row_map),
                      pl.BlockSpec((1, 1, d, d), w_map), pl.BlockSpec((1, 1, d), b_map),
                      pl.BlockSpec((1, 1, d, d), w_map), pl.BlockSpec((1, 1, d), b_map),
                      pl.BlockSpec((1, 1, d, d), w_map), pl.BlockSpec((1, 1, d), b_map)],
            out_specs=pl.BlockSpec((MOE_BM, d), out_map),
            scratch_shapes=[pltpu.VMEM((d, d), BF16)] * 3),
        out_shape=jax.ShapeDtypeStruct((nbm * MOE_BM, d), F32),
        compiler_params=_cparams(("arbitrary",)),
        name="moe_experts",
    )(blk_e, first, nb, xb, p['w_gate'], p['b_gate'].reshape(-1, 1, d), p['w_up'], p['b_up'].reshape(-1, 1, d),
      p['w_down'], p['b_down'].reshape(-1, 1, d))

    return pl.pallas_call(
        _combine_kernel,
        grid_spec=pltpu.PrefetchScalarGridSpec(
            num_scalar_prefetch=2, grid=(nt,),
            in_specs=[pl.BlockSpec((TM, TOP_K), lambda i, *_: (i, 0)),
                      pl.BlockSpec((TM, TOP_K), lambda i, *_: (i, 0)),
                      pl.BlockSpec((TM, d), lambda i, *_: (i, 0)),
                      pl.BlockSpec((1, 1, 6 * d), lambda i, *_: (layer * 16 + i // (seg // TM), 0, 0)),
                      pl.BlockSpec(memory_space=pl.ANY)],
            out_specs=pl.BlockSpec((TM, d), lambda i, *_: (i, 0)),
            scratch_shapes=[pltpu.VMEM((2, MOE_RMAX, d), F32), pltpu.SemaphoreType.DMA((2,))]),
        out_shape=jax.ShapeDtypeStruct((n_rows, d), F32),
        compiler_params=_cparams(("arbitrary",)),
        name="moe_combine",
    )(tab, tile_rows, pos.T, gate.T, h, mod3, yb)


def kernel(x, c, ctx, c_ctx, w_ada, b_ada, g_norm1, g_norm2, sg_w_in, sg_g_v, sg_w_s, sg_b_s, sg_w_out,
           ssm_w_in, ssm_lam_re, ssm_lam_im, ssm_log_dt, ssm_b_re, ssm_b_im, ssm_c_re, ssm_c_im, ssm_d, ssm_w_glu,
           mla_w_in, mla_g_q, mla_g_kv, mla_w_uq, mla_w_ukv, mla_g_qn, mla_g_kn, mla_w_out,
           moe_w_router, moe_b_router, moe_w_gate, moe_b_gate, moe_w_up, moe_b_up, moe_w_down, moe_b_down):
    nb, l_lat, d = x.shape
    l_ctx = ctx.shape[1]
    depth = w_ada.shape[0]
    n_lat, n_ctx = nb * l_lat, nb * l_ctx
    seg = l_lat
    assert n_ctx <= seg and seg % TM == 0 and n_ctx % TM == 0
    mod3 = modulation(c, c_ctx, w_ada, b_ada)
    h = jnp.concatenate([x.reshape(n_lat, d), ctx.reshape(n_ctx, d)], axis=0)
    for i in range(depth):
        mixer, slot = i % N_MIXERS, i // N_MIXERS
        ctx_out = i < depth - 1
        n_rows = n_lat + n_ctx if ctx_out else n_lat
        if mixer == 0:
            hm = gmlp_layer(h, mod3, i, g_norm1[i], sg_w_in[slot], sg_g_v[slot], sg_w_s[slot], sg_b_s[slot],
                            sg_w_out[slot], n_rows, seg)
        elif mixer == 1:
            p = dict(w_in=ssm_w_in[slot], lam_re=ssm_lam_re[slot], lam_im=ssm_lam_im[slot], log_dt=ssm_log_dt[slot],
                     b_re=ssm_b_re[slot], b_im=ssm_b_im[slot], c_re=ssm_c_re[slot], c_im=ssm_c_im[slot],
                     d=ssm_d[slot], w_glu=ssm_w_glu[slot])
            hm = s5_layer(h, mod3, i, g_norm1[i], p, n_lat, n_ctx, nb, seg)[:n_rows]
        else:
            p = dict(w_in=mla_w_in[slot], g_q=mla_g_q[slot], g_kv=mla_g_kv[slot], w_uq=mla_w_uq[slot],
                     w_ukv=mla_w_ukv[slot], g_qn=mla_g_qn[slot], g_kn=mla_g_kn[slot], w_out=mla_w_out[slot])
            hm = mla_layer(h, mod3, i, g_norm1[i], p, n_lat, n_ctx, nb, seg)[:n_rows]
        pm = dict(w_router=moe_w_router[i], b_router=moe_b_router[i], w_gate=moe_w_gate, b_gate=moe_b_gate,
                  w_up=moe_w_up, b_up=moe_b_up, w_down=moe_w_down, b_down=moe_b_down)
        h = moe_layer(hm, mod3, i, g_norm2[i], pm, n_rows, seg)
    return h[:n_lat].reshape(nb, l_lat, d)
```

```python
import functools
import math

import jax
import jax.numpy as jnp
import numpy as np
from jax import lax
from jax.experimental import pallas as pl
from jax.experimental.pallas import tpu as pltpu

F32 = jnp.float32
BF16 = jnp.bfloat16
I32 = jnp.int32
U32 = jnp.uint32
EPS = 1e-6

N_MIXERS = 3
GRID_W = 64
CHUNK = 128
SG_GROUPS = 8
S5_GROUP_CH = 16
S5_STATE = 64
MLA_HEADS = 8
MLA_NOPE = 128
MLA_ROPE = 64
MLA_QK = MLA_NOPE + MLA_ROPE
MLA_V = 128
MLA_Q_RANK = 384
MLA_KV_RANK = 256
ROPE_THETA = 10000.0
N_EXPERTS = 32
TOP_K = 4
SWIGLU_LIMIT = 7.0
SWIGLU_ALPHA = 1.702

LANES = 128
SUBLANES = 8
VMEM_LIMIT = 56 * 1024 * 1024

TM = 256
TMK = 1024
SUB = 256
S5_TB = 64
ATTN_TQ = 2048
ATTN_QS = 256
MOE_BM = 512
MOE_SUB = 256
MOE_RMAX = TM * TOP_K + N_EXPERTS * SUBLANES
PIECES = (256, 128, 64, 32, 16, 8)


def _cparams(sem):
    return pltpu.CompilerParams(dimension_semantics=sem, vmem_limit_bytes=VMEM_LIMIT)


def _rms(x, g):
    ms = jnp.mean(x * x, axis=-1, keepdims=True)
    return x * lax.rsqrt(ms + EPS) * g


def _adaln(x, g, shift, scale):
    return _rms(x, g) * (1.0 + scale) + shift


def _gelu(x):
    return 0.5 * x * (1.0 + lax.erf(x * (1.0 / math.sqrt(2.0))))


def _sigmoid(x):
    return 1.0 / (1.0 + jnp.exp(-x))


def _pack_bf16_pairs(x):
    k = x.shape[1] // 2
    lo = lax.bitcast_convert_type(x[:, :k], U32) >> 16
    hi = lax.bitcast_convert_type(x[:, k:], U32) & jnp.uint32(0xFFFF0000)
    return hi | lo


def _unpack_bf16_pairs(u):
    lo = lax.bitcast_convert_type(u << 16, F32).astype(BF16)
    hi = lax.bitcast_convert_type(u & jnp.uint32(0xFFFF0000), F32).astype(BF16)
    return jnp.concatenate([lo, hi], axis=1)


def _mod_spec(layer, d, tiles_per_seg):
    return pl.BlockSpec((1, 1, 6 * d), lambda i: (layer * 16 + i // tiles_per_seg, 0, 0))


def _mod_kernel(s_ref, w_ref, b_ref, o_ref):
    s = s_ref[...]
    s = s * _sigmoid(s)
    o_ref[0] = jnp.dot(s, w_ref[0], preferred_element_type=F32, precision=lax.Precision.HIGHEST) + b_ref[0]


def modulation(c, c_ctx, w_ada, b_ada):
    depth, d, d6 = w_ada.shape
    nb = c.shape[0]
    s = jnp.zeros((16, d), F32).at[:nb].set(c).at[nb].set(c_ctx)
    tn = 1024
    out = pl.pallas_call(
        _mod_kernel,
        grid=(depth, d6 // tn),
        in_specs=[pl.BlockSpec((16, d), lambda l, j: (0, 0)),
                  pl.BlockSpec((1, d, tn), lambda l, j: (l, 0, j)),
                  pl.BlockSpec((1, 1, tn), lambda l, j: (l, 0, j))],
        out_specs=pl.BlockSpec((1, 16, tn), lambda l, j: (l, 0, j)),
        out_shape=jax.ShapeDtypeStruct((depth, 16, d6), F32),
        compiler_params=_cparams(("arbitrary", "arbitrary")),
        name="adaln_modulation",
    )(s, w_ada, b_ada.reshape(depth, 1, d6))
    return out.reshape(depth * 16, 1, d6)


def _gmlp_kernel(h_ref, mod_ref, g1_ref, win_ref, gv_ref, ws_ref, bs_ref, wout_ref, o_ref, gated_ref):
    d = h_ref.shape[1]
    mod = mod_ref[0]
    gd = d // SG_GROUPS
    for sb in range(h_ref.shape[0] // SUB):
        r0 = sb * SUB
        h = h_ref[r0:r0 + SUB, :]
        n = _adaln(h, g1_ref[...], mod[:, 0:d], mod[:, d:2 * d]).astype(BF16)
        z = _gelu(jnp.dot(n, win_ref[...], preferred_element_type=F32))
        u = z[:, :d]
        v = _rms(z[:, d:], gv_ref[...]).astype(BF16)
        for j in range(SUB // CHUNK):
            rows = slice(j * CHUNK, (j + 1) * CHUNK)
            out_rows = slice(r0 + j * CHUNK, r0 + (j + 1) * CHUNK)
            for g in range(SG_GROUPS):
                cols = slice(g * gd, (g + 1) * gd)
                sv = jnp.dot(ws_ref[g], v[rows, cols], preferred_element_type=F32) + bs_ref[g]
                gated_ref[out_rows, cols] = (u[rows, cols] * sv).astype(BF16)
        m = jnp.dot(gated_ref[r0:r0 + SUB, :], wout_ref[...], preferred_element_type=F32)
        o_ref[r0:r0 + SUB, :] = h + mod[:, 2 * d:3 * d] * m


def gmlp_layer(h, mod3, layer, g1, w_in, g_v, w_s, b_s, w_out, n_rows, seg):
    d = h.shape[1]
    gd = d // SG_GROUPS
    tm = min(TMK, seg)
    bs_b = jnp.broadcast_to(b_s[:, :, None], (SG_GROUPS, CHUNK, gd)).astype(F32)
    full = lambda *shape: pl.BlockSpec(shape, lambda i: (0,) * len(shape))
    return pl.pallas_call(
        _gmlp_kernel,
        grid=(n_rows // tm,),
        in_specs=[pl.BlockSpec((tm, d), lambda i: (i, 0)),
                  _mod_spec(layer, d, seg // tm),
                  full(1, d), full(d, 2 * d), full(1, d),
                  full(SG_GROUPS, CHUNK, CHUNK), full(SG_GROUPS, CHUNK, gd), full(d, d)],
        out_specs=pl.BlockSpec((tm, d), lambda i: (i, 0)),
        out_shape=jax.ShapeDtypeStruct((n_rows, d), F32),
        scratch_shapes=[pltpu.VMEM((tm, d), BF16)],
        compiler_params=_cparams(("arbitrary",)),
        name="gmlp_mixer",
    )(h, mod3, g1.reshape(1, d), w_in.astype(BF16), g_v.reshape(1, d), w_s.astype(BF16), bs_b,
      w_out.astype(BF16))


def _ln_proj_kernel(h_ref, mod_ref, g1_ref, w_ref, o_ref):
    d = h_ref.shape[1]
    mod = mod_ref[0]
    for sb in range(h_ref.shape[0] // SUB):
        rows = slice(sb * SUB, (sb + 1) * SUB)
        n = _adaln(h_ref[rows, :], g1_ref[...], mod[:, 0:d], mod[:, d:2 * d]).astype(BF16)
        o_ref[rows, :] = jnp.dot(n, w_ref[...], preferred_element_type=F32)


def ln_proj(h, mod3, layer, g1, w, seg):
    n_rows, d = h.shape
    dn = w.shape[1]
    tm = min(TMK, seg)
    return pl.pallas_call(
        _ln_proj_kernel,
        grid=(n_rows // tm,),
        in_specs=[pl.BlockSpec((tm, d), lambda i: (i, 0)),
                  _mod_spec(layer, d, seg // tm),
                  pl.BlockSpec((1, d), lambda i: (0, 0)),
                  pl.BlockSpec((d, dn), lambda i: (0, 0))],
        out_specs=pl.BlockSpec((tm, dn), lambda i: (i, 0)),
        out_shape=jax.ShapeDtypeStruct((n_rows, dn), F32),
        compiler_params=_cparams(("arbitrary",)),
        name="adaln_in_proj",
    )(h, mod3, g1.reshape(1, d), w.astype(BF16))


def _out_proj_kernel(y_ref, h_ref, mod_ref, w_ref, o_ref, *, glu):
    d = h_ref.shape[1]
    mod = mod_ref[0]
    for sb in range(h_ref.shape[0] // SUB):
        rows = slice(sb * SUB, (sb + 1) * SUB)
        z = jnp.dot(y_ref[rows, :], w_ref[...], preferred_element_type=F32)
        if glu:
            z = z[:, :d] * _sigmoid(z[:, d:])
        o_ref[rows, :] = h_ref[rows, :] + mod[:, 2 * d:3 * d] * z


def out_proj(y, h, mod3, layer, w, seg, glu):
    n_rows, d = h.shape
    dk, dn = w.shape
    tm = min(TMK, seg)
    return pl.pallas_call(
        functools.partial(_out_proj_kernel, glu=glu),
        grid=(n_rows // tm,),
        in_specs=[pl.BlockSpec((tm, dk), lambda i: (i, 0)),
                  pl.BlockSpec((tm, d), lambda i: (i, 0)),
                  _mod_spec(layer, d, seg // tm),
                  pl.BlockSpec((dk, dn), lambda i: (0, 0))],
        out_specs=pl.BlockSpec((tm, d), lambda i: (i, 0)),
        out_shape=jax.ShapeDtypeStruct((n_rows, d), F32),
        compiler_params=_cparams(("arbitrary",)),
        name="mixer_out_proj",
    )(y, h, mod3, w.astype(BF16))


def _s5_params(lam_re, lam_im, log_dt, b_re, b_im, c_re, c_im):
    ng, npst = lam_re.shape
    gpb = LANES // S5_GROUP_CH
    nj = ng // gpb
    dt = jnp.exp(log_dt.astype(F32))[:, None]
    mag = jnp.exp(lam_re * dt)
    a_re = mag * jnp.cos(lam_im * dt)
    a_im = mag * jnp.sin(lam_im * dt)
    den = lam_re * lam_re + lam_im * lam_im
    n_re = a_re - 1.0
    f_re = (n_re * lam_re + a_im * lam_im) / den
    f_im = (a_im * lam_re - n_re * lam_im) / den
    bb_re = f_re[..., None] * b_re - f_im[..., None] * b_im
    bb_im = f_re[..., None] * b_im + f_im[..., None] * b_re
    eye = jnp.eye(gpb, dtype=F32)

    def blockdiag_in(bb):
        x = bb.reshape(nj, gpb, npst, S5_GROUP_CH).transpose(0, 1, 3, 2)
        return (x[:, :, :, None, :] * eye[None, :, None, :, None]).reshape(nj, LANES, gpb * npst)

    def blockdiag_out(cc):
        x = cc.reshape(nj, gpb, S5_GROUP_CH, npst).transpose(0, 1, 3, 2)
        return (x[:, :, :, None, :] * eye[None, :, None, :, None]).reshape(nj, gpb * npst, LANES)

    b_in = jnp.concatenate([blockdiag_in(bb_re), blockdiag_in(bb_im)], axis=2)
    c_out = jnp.concatenate([blockdiag_out(c_re), blockdiag_out(-c_im)], axis=1)
    a = jnp.stack([a_re.reshape(nj, 1, gpb * npst), a_im.reshape(nj, 1, gpb * npst)], axis=1)
    return a, b_in.astype(BF16), c_out.astype(BF16)


def _s5_dir_kernel(x_ref, *rest, reverse, final):
    nj = rest[-1].shape[0]
    h_ref = rest[-1]
    u_refs = rest[-1 - nj:-1]
    if final:
        yf_ref, d_ref, b_ref, c_ref, a_ref, o_ref = rest[:-1 - nj]
    else:
        b_ref, c_ref, a_ref, o_ref = rest[:-1 - nj]
    sw = b_ref.shape[2] // 2
    steps = x_ref.shape[0] // SUBLANES

    @pl.when(pl.program_id(0) == 0)
    def _():
        h_ref[...] = jnp.zeros_like(h_ref)

    x = x_ref[...]
    xb = x.astype(BF16)
    for j, u_ref in enumerate(u_refs):
        cols = slice(j * LANES, (j + 1) * LANES)
        u_ref[...] = jnp.dot(xb[:, cols], b_ref[j], preferred_element_type=F32)
        a_re = jnp.broadcast_to(a_ref[j, 0], (SUBLANES, sw))
        a_im = jnp.broadcast_to(a_ref[j, 1], (SUBLANES, sw))
        h_re, h_im = h_ref[j, 0], h_ref[j, 1]
        for s in range(steps):
            t = (steps - 1 - s) if reverse else s
            rows = slice(t * SUBLANES, (t + 1) * SUBLANES)
            h_re, h_im = (a_re * h_re - a_im * h_im + u_ref[rows, :sw],
                          a_re * h_im + a_im * h_re + u_ref[rows, sw:])
            u_ref[rows, :sw] = h_re
            u_ref[rows, sw:] = h_im
        h_ref[j, 0] = h_re
        h_ref[j, 1] = h_im
        y = jnp.dot(u_ref[...].astype(BF16), c_ref[j], preferred_element_type=F32)
        if final:
            o_ref[:, cols] = _gelu(d_ref[:, cols] * x[:, cols] + yf_ref[:, cols] + y).astype(BF16)
        else:
            o_ref[:, cols] = y


def s5_layer(h, mod3, layer, g1, p, n_lat, n_ctx, nb, seg):
    d = h.shape[1]
    assert nb == SUBLANES
    xs = ln_proj(h, mod3, layer, g1, p['w_in'], seg)
    l_lat, l_ctx = n_lat // nb, n_ctx // nb
    to_tb = lambda x, l: x.reshape(nb, l, d).transpose(1, 0, 2)
    x_tb = jnp.concatenate([to_tb(xs[n_lat:], l_ctx), to_tb(xs[:n_lat], l_lat)], axis=0).reshape(-1, d)
    rows = S5_TB * nb
    n_blk, n_cblk = (l_ctx + l_lat) // S5_TB, l_ctx // S5_TB
    fwd = lambda i: (i, 0)
    rev = lambda i: (jnp.where(i < n_cblk, n_cblk - 1 - i, n_blk - 1 + n_cblk - i), 0)
    full = lambda a: pl.BlockSpec(a.shape, lambda i: (0,) * a.ndim)
    y = None
    for k, order in enumerate((fwd, rev)):
        a, b_in, c_out = _s5_params(p['lam_re'][k], p['lam_im'][k], p['log_dt'][k], p['b_re'][k], p['b_im'][k],
                                    p['c_re'][k], p['c_im'][k])
        final = k == 1
        blk = pl.BlockSpec((rows, d), order)
        args = [x_tb] + ([y, p['d'].reshape(1, d)] if final else []) + [b_in, c_out, a]
        specs = [blk] + ([blk, full(args[2])] if final else []) + [full(b_in), full(c_out), full(a)]
        y = pl.pallas_call(
            functools.partial(_s5_dir_kernel, reverse=bool(k), final=final),
            grid=(n_blk,),
            in_specs=specs,
            out_specs=blk,
            out_shape=jax.ShapeDtypeStruct(x_tb.shape, BF16 if final else F32),
            scratch_shapes=[pltpu.VMEM((rows, b_in.shape[2]), F32)] * b_in.shape[0]
            + [pltpu.VMEM((b_in.shape[0], 2, SUBLANES, b_in.shape[2] // 2), F32)],
            compiler_params=_cparams(("arbitrary",)),
            name="s5_scan_reverse" if final else "s5_scan_forward",
        )(*args)
    y = y.reshape(l_ctx + l_lat, nb, d)
    from_tb = lambda x: x.transpose(1, 0, 2).reshape(-1, d)
    y_rows = jnp.concatenate([from_tb(y[l_ctx:]), from_tb(y[:l_ctx])], axis=0)
    return out_proj(y_rows, h, mod3, layer, p['w_glu'], seg, glu=True)


def _mla_proj_kernel(h_ref, mod_ref, g1_ref, win_ref, gq_ref, gkv_ref, wuq_ref, wukv_ref,
                     gqn_ref, gkn_ref, cos_ref, sin_ref, q_ref, k_ref, v_ref):
    d = h_ref.shape[1]
    mod = mod_ref[0]
    r0 = MLA_Q_RANK + MLA_KV_RANK
    hw = 2 * LANES
    qscale = 1.0 / math.sqrt(MLA_QK)
    gqn, gkn = gqn_ref[...], gkn_ref[...]
    for sb in range(h_ref.shape[0] // SUB):
        rows = slice(sb * SUB, (sb + 1) * SUB)
        n = _adaln(h_ref[rows, :], g1_ref[...], mod[:, 0:d], mod[:, d:2 * d]).astype(BF16)
        z = jnp.dot(n, win_ref[...], preferred_element_type=F32)
        ql = _rms(z[:, :MLA_Q_RANK], gq_ref[...]).astype(BF16)
        kvl = _rms(z[:, MLA_Q_RANK:r0], gkv_ref[...]).astype(BF16)
        pe, pe_sw = z[:, r0:r0 + LANES], z[:, r0 + LANES:r0 + 2 * LANES]
        qa = jnp.dot(ql, wuq_ref[...], preferred_element_type=F32)
        kv = jnp.dot(kvl, wukv_ref[...], preferred_element_type=F32)
        cos, sin = cos_ref[rows, :], sin_ref[rows, :]
        kr = (pe * gkn[1:2]) * cos + (pe_sw * gkn[2:3]) * sin
        pe_ss = jnp.sum(pe * pe, axis=-1, keepdims=True)
        for hd in range(MLA_HEADS):
            qn = qa[:, hd * hw:hd * hw + LANES]
            qr = qa[:, hd * hw + LANES:(hd + 1) * hw]
            qsw = qa[:, MLA_HEADS * hw + hd * LANES:MLA_HEADS * hw + (hd + 1) * LANES]
            ss = jnp.sum(qn * qn, axis=-1, keepdims=True) + jnp.sum(qr * qr, axis=-1, keepdims=True)
            rq = lax.rsqrt(ss * (1.0 / MLA_QK) + EPS) * qscale
            q_ref[rows, hd * hw:hd * hw + LANES] = (qn * rq * gqn[0:1]).astype(BF16)
            q_ref[rows, hd * hw + LANES:(hd + 1) * hw] = (
                rq * ((qr * gqn[1:2]) * cos + (qsw * gqn[2:3]) * sin)).astype(BF16)
            kn = kv[:, hd * LANES:(hd + 1) * LANES]
            rk = lax.rsqrt((jnp.sum(kn * kn, axis=-1, keepdims=True) + pe_ss) * (1.0 / MLA_QK) + EPS)
            k_ref[rows, hd * hw:hd * hw + LANES] = (kn * rk * gkn[0:1]).astype(BF16)
            k_ref[rows, hd * hw + LANES:(hd + 1) * hw] = (kr * rk).astype(BF16)
        v_ref[rows, :] = kv[:, MLA_HEADS * LANES:].astype(BF16)


def _attn_kernel(q_ref, kc_ref, vc_ref, *rest, with_latent):
    if with_latent:
        kl_ref, vl_ref, o_ref = rest
    else:
        (o_ref,) = rest
    nt = (((1,), (1,)), ((), ()))
    tq = q_ref.shape[0]
    qs = min(ATTN_QS, tq)
    for qi in range(tq // qs):
        rows = slice(qi * qs, (qi + 1) * qs)
        q = q_ref[rows, :]
        s1 = lax.dot_general(q, kc_ref[...], nt, preferred_element_type=F32)
        mx = jnp.max(s1, axis=-1, keepdims=True)
        if with_latent:
            s2 = lax.dot_general(q, kl_ref[...], nt, preferred_element_type=F32)
            mx = jnp.maximum(mx, jnp.max(s2, axis=-1, keepdims=True))
        p1 = jnp.exp(s1 - mx)
        den = jnp.sum(p1, axis=-1, keepdims=True)
        o = jnp.dot(p1.astype(BF16), vc_ref[...], preferred_element_type=F32)
        if with_latent:
            p2 = jnp.exp(s2 - mx)
            den = den + jnp.sum(p2, axis=-1, keepdims=True)
            o = o + jnp.dot(p2.astype(BF16), vl_ref[...], preferred_element_type=F32)
        o_ref[rows, :] = (o / den).astype(BF16)


def _rope_tables(l_lat, n_rows_id):
    rows = l_lat // GRID_W
    row = jnp.repeat(jnp.arange(rows), GRID_W).astype(F32)
    col = jnp.tile(jnp.arange(GRID_W), rows).astype(F32)
    quarter = MLA_ROPE // 4
    inv_freq = ROPE_THETA ** (-jnp.arange(quarter, dtype=F32) / quarter)
    ang_r, ang_c = row[:, None] * inv_freq, col[:, None] * inv_freq
    cos = jnp.concatenate([jnp.cos(ang_r), jnp.cos(ang_r), jnp.cos(ang_c), jnp.cos(ang_c)], axis=1)
    sin = jnp.concatenate([-jnp.sin(ang_r), jnp.sin(ang_r), -jnp.sin(ang_c), jnp.sin(ang_c)], axis=1)
    pad = ((0, n_rows_id), (0, LANES - MLA_ROPE))
    cos = jnp.pad(cos, pad).at[l_lat:, :MLA_ROPE].set(1.0)
    return cos, jnp.pad(sin, pad)


def mla_layer(h, mod3, layer, g1, p, n_lat, n_ctx, nb, seg):
    n_rows, d = h.shape
    hw = 2 * LANES
    nh = MLA_HEADS
    l_lat, l_ctx = n_lat // nb, n_ctx // nb
    r0 = MLA_Q_RANK + MLA_KV_RANK
    swap = np.arange(MLA_ROPE)
    swap = np.where(swap % 32 < 16, swap + 16, swap - 16)
    lane_pad = lambda x: jnp.pad(x, ((0, 0),) * (x.ndim - 1) + ((0, LANES - x.shape[-1]),))
    w_pe = p['w_in'][:, r0:]
    w_in_ext = jnp.concatenate([p['w_in'][:, :r0], lane_pad(w_pe), lane_pad(w_pe[:, swap])], axis=1).astype(BF16)
    wq = p['w_uq'].reshape(MLA_Q_RANK, nh, MLA_QK)
    wq_main = jnp.pad(wq, ((0, 0), (0, 0), (0, hw - MLA_QK))).reshape(MLA_Q_RANK, nh * hw)
    wq_sw = lane_pad(wq[:, :, MLA_NOPE:][:, :, swap]).reshape(MLA_Q_RANK, nh * LANES)
    w_uq_ext = jnp.concatenate([wq_main, wq_sw], axis=1).astype(BF16)
    wkv = p['w_ukv'].reshape(MLA_KV_RANK, nh, 2, LANES).transpose(0, 2, 1, 3).reshape(MLA_KV_RANK, 2 * nh * LANES)
    gains = lambda g: jnp.zeros((SUBLANES, LANES), F32).at[0].set(g[:MLA_NOPE]).at[1, :MLA_ROPE].set(
        g[MLA_NOPE:]).at[2, :MLA_ROPE].set(g[MLA_NOPE:][swap])
    cos, sin = _rope_tables(l_lat, seg)
    tm = min(TMK // 2, seg)
    t_lat = l_lat // tm
    tps = seg // tm
    rope_map = lambda i: (jnp.where(i < (n_lat // tm), i % t_lat, t_lat + i % tps), 0)
    full = lambda *shape: pl.BlockSpec(shape, lambda i: (0,) * len(shape))
    q, k, v = pl.pallas_call(
        _mla_proj_kernel,
        grid=(n_rows // tm,),
        in_specs=[pl.BlockSpec((tm, d), lambda i: (i, 0)), _mod_spec(layer, d, tps), full(1, d),
                  full(d, r0 + 2 * LANES), full(1, MLA_Q_RANK), full(1, MLA_KV_RANK),
                  full(MLA_Q_RANK, nh * (hw + LANES)), full(MLA_KV_RANK, 2 * nh * LANES),
                  full(SUBLANES, LANES), full(SUBLANES, LANES),
                  pl.BlockSpec((tm, LANES), rope_map), pl.BlockSpec((tm, LANES), rope_map)],
        out_specs=[pl.BlockSpec((tm, nh * hw), lambda i: (i, 0)), pl.BlockSpec((tm, nh * hw), lambda i: (i, 0)),
                   pl.BlockSpec((tm, nh * LANES), lambda i: (i, 0))],
        out_shape=[jax.ShapeDtypeStruct((n_rows, nh * hw), BF16), jax.ShapeDtypeStruct((n_rows, nh * hw), BF16),
                   jax.ShapeDtypeStruct((n_rows, nh * LANES), BF16)],
        compiler_params=_cparams(("arbitrary",)),
        name="mla_projection",
    )(h, mod3, g1.reshape(1, d), w_in_ext, p['g_q'].reshape(1, -1), p['g_kv'].reshape(1, -1), w_uq_ext,
      wkv.astype(BF16), gains(p['g_qn']), gains(p['g_kn']), cos, sin)

    tq = min(ATTN_TQ, l_lat)
    cb = n_lat // l_ctx
    o_lat = pl.pallas_call(
        functools.partial(_attn_kernel, with_latent=True),
        grid=(nb, nh, l_lat // tq),
        in_specs=[pl.BlockSpec((tq, hw), lambda b, hd, i: (b * (l_lat // tq) + i, hd)),
                  pl.BlockSpec((l_ctx, hw), lambda b, hd, i: (cb + b, hd)),
                  pl.BlockSpec((l_ctx, LANES), lambda b, hd, i: (cb + b, hd)),
                  pl.BlockSpec((l_lat, hw), lambda b, hd, i: (b, hd)),
                  pl.BlockSpec((l_lat, LANES), lambda b, hd, i: (b, hd))],
        out_specs=pl.BlockSpec((tq, LANES), lambda b, hd, i: (b * (l_lat // tq) + i, hd)),
        out_shape=jax.ShapeDtypeStruct((n_lat, nh * LANES), BF16),
        compiler_params=_cparams(("arbitrary", "arbitrary", "arbitrary")),
        name="mla_attention_latent",
    )(q, k, v, k, v)
    o_ctx = pl.pallas_call(
        functools.partial(_attn_kernel, with_latent=False),
        grid=(nb, nh),
        in_specs=[pl.BlockSpec((l_ctx, hw), lambda b, hd: (cb + b, hd)),
                  pl.BlockSpec((l_ctx, hw), lambda b, hd: (cb + b, hd)),
                  pl.BlockSpec((l_ctx, LANES), lambda b, hd: (cb + b, hd))],
        out_specs=pl.BlockSpec((l_ctx, LANES), lambda b, hd: (b, hd)),
        out_shape=jax.ShapeDtypeStruct((n_ctx, nh * LANES), BF16),
        compiler_params=_cparams(("arbitrary", "arbitrary")),
        name="mla_attention_context",
    )(q, k, v)
    return out_proj(jnp.concatenate([o_lat, o_ctx], axis=0), h, mod3, layer, p['w_out'], seg, glu=False)


def _route_kernel(h_ref, mod_ref, g2_ref, wrt_ref, br_ref, n2_ref, pos_ref, gate_ref, cnt_ref):
    d = h_ref.shape[1]
    tm = h_ref.shape[0]
    ne = wrt_ref.shape[1]
    mod = mod_ref[0]
    n2 = _adaln(h_ref[...], g2_ref[...], mod[:, 3 * d:4 * d], mod[:, 4 * d:5 * d])
    n2_hi = n2.astype(BF16)
    n2_ref[...] = n2_hi
    n2_lo = (n2 - n2_hi.astype(F32)).astype(BF16)
    nt = (((1,), (1,)), ((), ()))
    w_hi, w_lo = wrt_ref[0], wrt_ref[1]
    logits = (lax.dot_general(w_hi, n2_hi, nt, preferred_element_type=F32)
              + (lax.dot_general(w_hi, n2_lo, nt, preferred_element_type=F32)
                 + lax.dot_general(w_lo, n2_hi, nt, preferred_element_type=F32))) + br_ref[...]
    eidx = lax.broadcasted_iota(I32, (ne, tm), 0).astype(F32)
    work = logits
    vals, idxs = [], []
    for _ in range(TOP_K):
        mx = jnp.max(work, axis=0, keepdims=True)
        idx = jnp.min(jnp.where(work == mx, eidx, float(ne)), axis=0, keepdims=True)
        vals.append(mx)
        idxs.append(idx)
        work = jnp.where(eidx == idx, -jnp.inf, work)
    ex = [jnp.exp(v - vals[0]) for v in vals]
    den = ex[0] + ex[1] + ex[2] + ex[3]
    sel = (eidx == idxs[0]) | (eidx == idxs[1]) | (eidx == idxs[2]) | (eidx == idxs[3])
    onehot = jnp.where(sel, 1.0, 0.0).astype(BF16)
    r = lax.broadcasted_iota(I32, (tm, tm), 0)
    c = lax.broadcasted_iota(I32, (tm, tm), 1)
    before = jnp.where(r < c, 1.0, 0.0).astype(BF16)
    rank_all = jnp.dot(onehot, before, preferred_element_type=F32)
    cnt = jnp.dot(onehot, jnp.ones((tm, tm), BF16), preferred_element_type=F32)
    cnt8 = jnp.floor((cnt + (SUBLANES - 1)) * (1.0 / SUBLANES)) * SUBLANES
    er = lax.broadcasted_iota(I32, (ne, LANES), 0)
    ec = lax.broadcasted_iota(I32, (ne, LANES), 1)
    lower = jnp.where(ec < er, 1.0, 0.0).astype(BF16)
    cnt8_p = jnp.concatenate([cnt8, jnp.zeros((LANES - ne, tm), F32)], axis=0).astype(BF16)
    off = jnp.dot(lower, cnt8_p, preferred_element_type=F32)
    base = rank_all + off
    for k in range(TOP_K):
        pos = jnp.sum(jnp.where(eidx == idxs[k], base, 0.0), axis=0, keepdims=True)
        pos_ref[k:k + 1, :] = pos.astype(I32)
        gate_ref[k:k + 1, :] = ex[k] / den
    cnt_ref[0] = cnt8[:, :LANES].astype(I32)


def moe_route(h, mod3, layer, g2, w_router, b_router, n_rows, seg):
    d = h.shape[1]
    ne = w_router.shape[1]
    nt = n_rows // TM
    wt = w_router.T
    wt_hi = wt.astype(BF16)
    wt_split = jnp.stack([wt_hi, (wt - wt_hi.astype(F32)).astype(BF16)])
    return pl.pallas_call(
        _route_kernel,
        grid=(nt,),
        in_specs=[pl.BlockSpec((TM, d), lambda i: (i, 0)), _mod_spec(layer, d, seg // TM),
                  pl.BlockSpec((1, d), lambda i: (0, 0)), pl.BlockSpec((2, ne, d), lambda i: (0, 0, 0)),
                  pl.BlockSpec((ne, 1), lambda i: (0, 0))],
        out_specs=[pl.BlockSpec((TM, d), lambda i: (i, 0)), pl.BlockSpec((TOP_K, TM), lambda i: (0, i)),
                   pl.BlockSpec((TOP_K, TM), lambda i: (0, i)), pl.BlockSpec((1, ne, LANES), lambda i: (i, 0, 0))],
        out_shape=[jax.ShapeDtypeStruct((n_rows, d), BF16), jax.ShapeDtypeStruct((TOP_K, n_rows), I32),
                   jax.ShapeDtypeStruct((TOP_K, n_rows), F32), jax.ShapeDtypeStruct((nt, ne, LANES), I32)],
        compiler_params=_cparams(("arbitrary",)),
        name="moe_route",
    )(h, mod3, g2.reshape(1, d), wt_split, b_router.reshape(ne, 1))


def _chunk_copies(tab_ref, tile, e, vm_ref, hbm_ref, sem, to_hbm):
    ne = N_EXPERTS
    stride = tab_ref.shape[0] // 3
    off = tab_ref[tile * ne + e]
    n8 = tab_ref[stride + tile * ne + e]
    dst = tab_ref[2 * stride + tile * ne + e]
    out = []
    for size in PIECES:
        bit = (size // SUBLANES).bit_length() - 1
        done = ((n8 >> (bit + 1)) << (bit + 1)) * SUBLANES
        lo = pl.multiple_of(off + done, SUBLANES)
        hi = pl.multiple_of(dst + done, SUBLANES)
        v, hb = vm_ref.at[pl.ds(lo, size), :], hbm_ref.at[pl.ds(hi, size), :]
        cp = pltpu.make_async_copy(v, hb, sem) if to_hbm else pltpu.make_async_copy(hb, v, sem)
        out.append((((n8 >> bit) & 1) == 1, cp))
    return out


def _for_each_copy(tab_ref, tile, vm_ref, hbm_ref, sem, to_hbm, action):
    small = PIECES.index(32)

    def body(e, carry):
        copies = _chunk_copies(tab_ref, tile, e, vm_ref, hbm_ref, sem, to_hbm)

        def run(items):
            for cond, cp in items:
                @pl.when(cond)
                def _():
                    action(cp)
        run(copies[small:])

        @pl.when(tab_ref[tab_ref.shape[0] // 3 + tile * N_EXPERTS + e] >= 2 * PIECES[small] // SUBLANES)
        def _():
            run(copies[:small])
        return carry
    lax.fori_loop(0, N_EXPERTS, body, 0)


def _wait_rows(n_rows, vm_ref, hbm_ref, sem, to_hbm):
    n8 = n_rows >> (SUBLANES.bit_length() - 1)
    size = SUBLANES
    while size * 2 <= vm_ref.shape[0]:
        size *= 2
    while size >= SUBLANES:
        v, hb = vm_ref.at[pl.ds(0, size), :], hbm_ref.at[pl.ds(0, size), :]
        cp = pltpu.make_async_copy(v, hb, sem) if to_hbm else pltpu.make_async_copy(hb, v, sem)

        @pl.when(((n8 >> ((size // SUBLANES).bit_length() - 1)) & 1) == 1)
        def _():
            cp.wait()
        size //= 2


def _dispatch_kernel(tab_ref, rows_ref, tail_ref, nb_ref, n2_ref, pos_ref, xb_ref, sorted_ref, zero_ref, sem, zsem):
    i = pl.program_id(0)
    last = pl.num_programs(0) - 1
    slot = i & 1
    cur = sorted_ref.at[slot]
    tm = n2_ref.shape[0]
    rmax = sorted_ref.shape[1]
    ridx = lax.broadcasted_iota(I32, (rmax, tm), 0)
    hit = ridx == pos_ref[0:1, :]
    for k in range(1, TOP_K):
        hit = hit | (ridx == pos_ref[k:k + 1, :])
    perm = jnp.where(hit, 1.0, 0.0).astype(BF16)
    cur[...] = _pack_bf16_pairs(jnp.dot(perm, n2_ref[...], preferred_element_type=F32))
    _for_each_copy(tab_ref, i, cur, xb_ref, sem.at[slot], True, lambda cp: cp.start())

    @pl.when(i > 0)
    def _():
        _wait_rows(rows_ref[i - 1], sorted_ref.at[1 - slot], xb_ref, sem.at[1 - slot], True)

    @pl.when(i == last)
    def _():
        _wait_rows(rows_ref[i], cur, xb_ref, sem.at[slot], True)
        zero_ref[...] = jnp.zeros_like(zero_ref)
        _for_each_copy(tail_ref, 0, zero_ref, xb_ref, zsem, True, lambda cp: cp.start())
        _for_each_copy(tail_ref, 0, zero_ref, xb_ref, zsem, True, lambda cp: cp.wait())

        zr = zero_ref.shape[0]

        def spare(b, carry):
            cp = pltpu.make_async_copy(zero_ref, xb_ref.at[pl.ds(pl.multiple_of(b * zr, zr), zr), :], zsem)
            cp.start()
            cp.wait()
            return carry
        lax.fori_loop(nb_ref[0] * (MOE_BM // zr), xb_ref.shape[0] // zr, spare, 0)


def _expert_kernel(be_ref, first_ref, nb_ref, x_ref, wg_ref, bg_ref, wu_ref, bu_ref, wd_ref, bd_ref, y_ref,
                   wg_s, wu_s, wd_s):
    b = pl.program_id(0)

    @pl.when(b < nb_ref[0])
    def _():
        @pl.when((first_ref[b] & 1) == 1)
        def _():
            wg_s[...] = wg_ref[0, 0].astype(BF16)
            wu_s[...] = wu_ref[0, 0].astype(BF16)
            wd_s[...] = wd_ref[0, 0].astype(BF16)

        def ffn(rows):
            x = _unpack_bf16_pairs(x_ref[rows, :])
            g = jnp.minimum(jnp.dot(x, wg_s[...], preferred_element_type=F32) + bg_ref[0], SWIGLU_LIMIT)
            u = jnp.clip(jnp.dot(x, wu_s[...], preferred_element_type=F32) + bu_ref[0], -SWIGLU_LIMIT, SWIGLU_LIMIT)
            a = (g * _sigmoid(SWIGLU_ALPHA * g) * (u + 1.0)).astype(BF16)
            y = jnp.dot(a, wd_s[...], preferred_element_type=F32) + bd_ref[0]
            y_ref[rows, :] = _pack_bf16_pairs(y.astype(BF16).astype(F32))

        n_sub = x_ref.shape[0] // MOE_SUB
        live = first_ref[b] >> 1

        @pl.when(live == n_sub)
        def _():
            for sb in range(n_sub):
                ffn(slice(sb * MOE_SUB, (sb + 1) * MOE_SUB))

        @pl.when(live < n_sub)
        def _():
            for sb in range(n_sub):
                rows = slice(sb * MOE_SUB, (sb + 1) * MOE_SUB)

                @pl.when(sb < live)
                def _():
                    ffn(rows)

                @pl.when(sb >= live)
                def _():
                    y_ref[rows, :] = jnp.zeros((MOE_SUB, y_ref.shape[1]), y_ref.dtype)

    @pl.when(b >= nb_ref[0])
    def _():
        y_ref[...] = jnp.zeros_like(y_ref)


def _combine_kernel(tab_ref, rows_ref, pos_ref, gate_ref, h_ref, mod_ref, yb_ref, o_ref, ys_ref, sem):
    i = pl.program_id(0)
    slot = i & 1
    d = h_ref.shape[1]
    tm = h_ref.shape[0]
    rmax = ys_ref.shape[1]

    @pl.when(i == 0)
    def _():
        ys_ref[...] = jnp.zeros_like(ys_ref)
        _for_each_copy(tab_ref, 0, ys_ref.at[0], yb_ref, sem.at[0], False, lambda cp: cp.start())

    @pl.when(i + 1 < pl.num_programs(0))
    def _():
        _for_each_copy(tab_ref, i + 1, ys_ref.at[1 - slot], yb_ref, sem.at[1 - slot], False, lambda cp: cp.start())

    cidx = lax.broadcasted_iota(I32, (tm, rmax), 1)
    wt = jnp.where(cidx == pos_ref[:, 0:1], gate_ref[:, 0:1], 0.0)
    for k in range(1, TOP_K):
        wt = wt + jnp.where(cidx == pos_ref[:, k:k + 1], gate_ref[:, k:k + 1], 0.0)
    _wait_rows(rows_ref[i], ys_ref.at[slot], yb_ref, sem.at[slot], False)
    f = jnp.dot(wt.astype(BF16), _unpack_bf16_pairs(ys_ref[slot]), preferred_element_type=F32)
    o_ref[...] = h_ref[...] + mod_ref[0][:, 5 * d:6 * d] * f


def _moe_plan(cnt8, n_blocks_max):
    nt, ne = cnt8.shape
    loc_off = jnp.cumsum(cnt8, axis=1) - cnt8
    used = jnp.sum(cnt8, axis=0)
    rows_e = (used + MOE_BM - 1) // MOE_BM * MOE_BM
    e_end = jnp.cumsum(rows_e)
    e_start = e_end - rows_e
    dst = e_start[None, :] + jnp.cumsum(cnt8, axis=0) - cnt8
    tab = jnp.stack([loc_off, cnt8 // SUBLANES, dst]).reshape(3 * nt * ne).astype(I32)
    tail = jnp.stack([jnp.zeros((ne,), I32), (rows_e - used) // SUBLANES, e_start + used]).reshape(3 * ne).astype(I32)
    nb = (e_end[-1] // MOE_BM).astype(I32)
    blk = jnp.arange(n_blocks_max, dtype=I32)
    blk_c = jnp.minimum(blk, nb - 1)
    blk_e = jnp.sum((e_end[None, :] <= (blk_c * MOE_BM)[:, None]).astype(I32), axis=1)
    blk_e = jnp.minimum(blk_e, ne - 1)
    first = (blk_c * MOE_BM == e_start[blk_e]).astype(I32)
    live = jnp.clip(e_start[blk_e] + used[blk_e] - blk_c * MOE_BM, 0, MOE_BM)
    first = first + 2 * ((live + MOE_SUB - 1) // MOE_SUB).astype(I32)
    tile_rows = jnp.sum(cnt8, axis=1).astype(I32)
    return tab, tile_rows, tail, blk_e, first, nb.reshape(1)


def moe_layer(h, mod3, layer, g2, p, n_rows, seg):
    d = h.shape[1]
    ne = N_EXPERTS
    nt = n_rows // TM
    n2, pos, gate, cnt = moe_route(h, mod3, layer, g2, p['w_router'], p['b_router'], n_rows, seg)
    max_rows = n_rows * TOP_K + nt * ne * (SUBLANES - 1) + ne * (MOE_BM - 1)
    nbm = -(-max_rows // MOE_BM)
    dp = d // 2
    tab, tile_rows, tail, blk_e, first, nb = _moe_plan(cnt[:, :, 0], nbm)

    xb = pl.pallas_call(
        _dispatch_kernel,
        grid_spec=pltpu.PrefetchScalarGridSpec(
            num_scalar_prefetch=4, grid=(nt,),
            in_specs=[pl.BlockSpec((TM, d), lambda i, *_: (i, 0)),
                      pl.BlockSpec((TOP_K, TM), lambda i, *_: (0, i))],
            out_specs=pl.BlockSpec(memory_space=pl.ANY),
            scratch_shapes=[pltpu.VMEM((2, MOE_RMAX, dp), U32), pltpu.VMEM((MOE_BM, dp), U32),
                            pltpu.SemaphoreType.DMA((2,)), pltpu.SemaphoreType.DMA(())]),
        out_shape=jax.ShapeDtypeStruct((nbm * MOE_BM, dp), U32),
        compiler_params=_cparams(("arbitrary",)),
        name="moe_dispatch",
    )(tab, tile_rows, tail, nb, n2, pos)

    row_map = lambda b, be, fi, nbr: (jnp.maximum(jnp.minimum(b, nbr[0] - 1), 0), 0)
    out_map = lambda b, be, fi, nbr: (b, 0)
    w_map = lambda b, be, fi, nbr: (layer, be[b], 0, 0)
    b_map = lambda b, be, fi, nbr: (layer * ne + be[b], 0, 0)
    yb = pl.pallas_call(
        _expert_kernel,
        grid_spec=pltpu.PrefetchScalarGridSpec(
            num_scalar_prefetch=3, grid=(nbm,),
            in_specs=[pl.BlockSpec((MOE_BM, dp), row_map),
                      pl.BlockSpec((1, 1, d, d), w_map), pl.BlockSpec((1, 1, d), b_map),
                      pl.BlockSpec((1, 1, d, d), w_map), pl.BlockSpec((1, 1, d), b_map),
                      pl.BlockSpec((1, 1, d, d), w_map), pl.BlockSpec((1, 1, d), b_map)],
            out_specs=pl.BlockSpec((MOE_BM, dp), out_map),
            scratch_shapes=[pltpu.VMEM((d, d), BF16)] * 3),
        out_shape=jax.ShapeDtypeStruct((nbm * MOE_BM, dp), U32),
        compiler_params=_cparams(("arbitrary",)),
        name="moe_experts",
    )(blk_e, first, nb, xb, p['w_gate'], p['b_gate'].reshape(-1, 1, d), p['w_up'], p['b_up'].reshape(-1, 1, d),
      p['w_down'], p['b_down'].reshape(-1, 1, d))

    return pl.pallas_call(
        _combine_kernel,
        grid_spec=pltpu.PrefetchScalarGridSpec(
            num_scalar_prefetch=2, grid=(nt,),
            in_specs=[pl.BlockSpec((TM, TOP_K), lambda i, *_: (i, 0)),
                      pl.BlockSpec((TM, TOP_K), lambda i, *_: (i, 0)),
                      pl.BlockSpec((TM, d), lambda i, *_: (i, 0)),
                      pl.BlockSpec((1, 1, 6 * d), lambda i, *_: (layer * 16 + i // (seg // TM), 0, 0)),
                      pl.BlockSpec(memory_space=pl.ANY)],
            out_specs=pl.BlockSpec((TM, d), lambda i, *_: (i, 0)),
            scratch_shapes=[pltpu.VMEM((2, MOE_RMAX, dp), U32), pltpu.SemaphoreType.DMA((2,))]),
        out_shape=jax.ShapeDtypeStruct((n_rows, d), F32),
        compiler_params=_cparams(("arbitrary",)),
        name="moe_combine",
    )(tab, tile_rows, pos.T, gate.T, h, mod3, yb)


def kernel(x, c, ctx, c_ctx, w_ada, b_ada, g_norm1, g_norm2, sg_w_in, sg_g_v, sg_w_s, sg_b_s, sg_w_out,
           ssm_w_in, ssm_lam_re, ssm_lam_im, ssm_log_dt, ssm_b_re, ssm_b_im, ssm_c_re, ssm_c_im, ssm_d, ssm_w_glu,
           mla_w_in, mla_g_q, mla_g_kv, mla_w_uq, mla_w_ukv, mla_g_qn, mla_g_kn, mla_w_out,
           moe_w_router, moe_b_router, moe_w_gate, moe_b_gate, moe_w_up, moe_b_up, moe_w_down, moe_b_down):
    nb, l_lat, d = x.shape
    l_ctx = ctx.shape[1]
    depth = w_ada.shape[0]
    n_lat, n_ctx = nb * l_lat, nb * l_ctx
    seg = l_lat
    assert n_ctx <= seg and seg % TM == 0 and n_ctx % TM == 0
    mod3 = modulation(c, c_ctx, w_ada, b_ada)
    h = jnp.concatenate([x.reshape(n_lat, d), ctx.reshape(n_ctx, d)], axis=0)
    for i in range(depth):
        mixer, slot = i % N_MIXERS, i // N_MIXERS
        ctx_out = i < depth - 1
        n_rows = n_lat + n_ctx if ctx_out else n_lat
        if mixer == 0:
            hm = gmlp_layer(h, mod3, i, g_norm1[i], sg_w_in[slot], sg_g_v[slot], sg_w_s[slot], sg_b_s[slot],
                            sg_w_out[slot], n_rows, seg)
        elif mixer == 1:
            p = dict(w_in=ssm_w_in[slot], lam_re=ssm_lam_re[slot], lam_im=ssm_lam_im[slot], log_dt=ssm_log_dt[slot],
                     b_re=ssm_b_re[slot], b_im=ssm_b_im[slot], c_re=ssm_c_re[slot], c_im=ssm_c_im[slot],
                     d=ssm_d[slot], w_glu=ssm_w_glu[slot])
            hm = s5_layer(h, mod3, i, g_norm1[i], p, n_lat, n_ctx, nb, seg)[:n_rows]
        else:
            p = dict(w_in=mla_w_in[slot], g_q=mla_g_q[slot], g_kv=mla_g_kv[slot], w_uq=mla_w_uq[slot],
                     w_ukv=mla_w_ukv[slot], g_qn=mla_g_qn[slot], g_kn=mla_g_kn[slot], w_out=mla_w_out[slot])
            hm = mla_layer(h, mod3, i, g_norm1[i], p, n_lat, n_ctx, nb, seg)[:n_rows]
        pm = dict(w_router=moe_w_router[i], b_router=moe_b_router[i], w_gate=moe_w_gate, b_gate=moe_b_gate,
                  w_up=moe_w_up, b_up=moe_b_up, w_down=moe_w_down, b_down=moe_b_down)
        h = moe_layer(hm, mod3, i, g_norm2[i], pm, n_rows, seg)
    return h[:n_lat].reshape(nb, l_lat, d)
```

```python
import functools
import math

import jax
import jax.numpy as jnp
import numpy as np
from jax import lax
from jax.experimental import pallas as pl
from jax.experimental.pallas import tpu as pltpu

F32 = jnp.float32
BF16 = jnp.bfloat16
I32 = jnp.int32
U32 = jnp.uint32
EPS = 1e-6

N_MIXERS = 3
GRID_W = 64
CHUNK = 128
SG_GROUPS = 8
S5_GROUP_CH = 16
S5_STATE = 64
MLA_HEADS = 8
MLA_NOPE = 128
MLA_ROPE = 64
MLA_QK = MLA_NOPE + MLA_ROPE
MLA_V = 128
MLA_Q_RANK = 384
MLA_KV_RANK = 256
ROPE_THETA = 10000.0
N_EXPERTS = 32
TOP_K = 4
SWIGLU_LIMIT = 7.0
SWIGLU_ALPHA = 1.702

LANES = 128
SUBLANES = 8
VMEM_LIMIT = 56 * 1024 * 1024

TM = 256
TMK = 1024
SUB = 256
S5_TB = 64
ATTN_TQ = 2048
ATTN_QS = 256
MOE_BM = 512
MOE_SUB = 256
MOE_RMAX = TM * TOP_K + N_EXPERTS * SUBLANES
PIECES = (256, 128, 64, 32, 16, 8)


def _cparams(sem):
    return pltpu.CompilerParams(dimension_semantics=sem, vmem_limit_bytes=VMEM_LIMIT)


def _rms(x, g):
    ms = jnp.mean(x * x, axis=-1, keepdims=True)
    return x * lax.rsqrt(ms + EPS) * g


def _adaln(x, g, shift, scale):
    return _rms(x, g) * (1.0 + scale) + shift


def _gelu(x):
    return 0.5 * x * (1.0 + lax.erf(x * (1.0 / math.sqrt(2.0))))


def _sigmoid(x):
    return 1.0 / (1.0 + jnp.exp(-x))


def _pack_bf16_pairs(x):
    k = x.shape[1] // 2
    lo = lax.bitcast_convert_type(x[:, :k], U32) >> 16
    hi = lax.bitcast_convert_type(x[:, k:], U32) & jnp.uint32(0xFFFF0000)
    return hi | lo


def _unpack_bf16_pairs(u):
    lo = lax.bitcast_convert_type(u << 16, F32).astype(BF16)
    hi = lax.bitcast_convert_type(u & jnp.uint32(0xFFFF0000), F32).astype(BF16)
    return jnp.concatenate([lo, hi], axis=1)


def _subtiles(rows):
    n = max(1, rows // SUB)
    return [slice(i * (rows // n), (i + 1) * (rows // n)) for i in range(n)]


def _mod_spec(layer, d, tiles_per_seg):
    return pl.BlockSpec((1, 1, 6 * d), lambda i: (layer * 16 + i // tiles_per_seg, 0, 0))


def _mod_kernel(s_ref, w_ref, b_ref, o_ref):
    s = s_ref[...]
    s = s * _sigmoid(s)
    w = w_ref[0]
    s_hi, w_hi = s.astype(BF16), w.astype(BF16)
    s_lo, w_lo = (s - s_hi.astype(F32)).astype(BF16), (w - w_hi.astype(F32)).astype(BF16)
    acc = jnp.dot(s_hi, w_lo, preferred_element_type=F32) + jnp.dot(s_lo, w_hi, preferred_element_type=F32)
    o_ref[0] = jnp.dot(s_hi, w_hi, preferred_element_type=F32) + acc + b_ref[0]


def modulation(c, c_ctx, w_ada, b_ada):
    depth, d, d6 = w_ada.shape
    nb = c.shape[0]
    s = jnp.zeros((16, d), F32).at[:nb].set(c).at[nb].set(c_ctx)
    tn = 1024
    out = pl.pallas_call(
        _mod_kernel,
        grid=(depth, d6 // tn),
        in_specs=[pl.BlockSpec((16, d), lambda l, j: (0, 0)),
                  pl.BlockSpec((1, d, tn), lambda l, j: (l, 0, j)),
                  pl.BlockSpec((1, 1, tn), lambda l, j: (l, 0, j))],
        out_specs=pl.BlockSpec((1, 16, tn), lambda l, j: (l, 0, j)),
        out_shape=jax.ShapeDtypeStruct((depth, 16, d6), F32),
        compiler_params=_cparams(("arbitrary", "arbitrary")),
        name="adaln_modulation",
    )(s, w_ada, b_ada.reshape(depth, 1, d6))
    return out.reshape(depth * 16, 1, d6)


def _gmlp_kernel(h_ref, mod_ref, g1_ref, win_ref, gv_ref, ws_ref, bs_ref, wout_ref, o_ref, gated_ref):
    d = h_ref.shape[1]
    mod = mod_ref[0]
    gd = d // SG_GROUPS
    for sb in range(h_ref.shape[0] // SUB):
        r0 = sb * SUB
        h = h_ref[r0:r0 + SUB, :]
        n = _adaln(h, g1_ref[...], mod[:, 0:d], mod[:, d:2 * d]).astype(BF16)
        z = _gelu(jnp.dot(n, win_ref[...], preferred_element_type=F32))
        u = z[:, :d]
        v = _rms(z[:, d:], gv_ref[...]).astype(BF16)
        for j in range(SUB // CHUNK):
            rows = slice(j * CHUNK, (j + 1) * CHUNK)
            out_rows = slice(r0 + j * CHUNK, r0 + (j + 1) * CHUNK)
            for g in range(SG_GROUPS):
                cols = slice(g * gd, (g + 1) * gd)
                sv = jnp.dot(ws_ref[g], v[rows, cols], preferred_element_type=F32) + bs_ref[g]
                gated_ref[out_rows, cols] = (u[rows, cols] * sv).astype(BF16)
        m = jnp.dot(gated_ref[r0:r0 + SUB, :], wout_ref[...], preferred_element_type=F32)
        o_ref[r0:r0 + SUB, :] = h + mod[:, 2 * d:3 * d] * m


def gmlp_layer(h, mod3, layer, g1, w_in, g_v, w_s, b_s, w_out, n_rows, seg):
    d = h.shape[1]
    gd = d // SG_GROUPS
    tm = min(TMK, seg)
    bs_b = jnp.broadcast_to(b_s[:, :, None], (SG_GROUPS, CHUNK, gd)).astype(F32)
    full = lambda *shape: pl.BlockSpec(shape, lambda i: (0,) * len(shape))
    return pl.pallas_call(
        _gmlp_kernel,
        grid=(n_rows // tm,),
        in_specs=[pl.BlockSpec((tm, d), lambda i: (i, 0)),
                  _mod_spec(layer, d, seg // tm),
                  full(1, d), full(d, 2 * d), full(1, d),
                  full(SG_GROUPS, CHUNK, CHUNK), full(SG_GROUPS, CHUNK, gd), full(d, d)],
        out_specs=pl.BlockSpec((tm, d), lambda i: (i, 0)),
        out_shape=jax.ShapeDtypeStruct((n_rows, d), F32),
        scratch_shapes=[pltpu.VMEM((tm, d), BF16)],
        compiler_params=_cparams(("arbitrary",)),
        name="gmlp_mixer",
    )(h, mod3, g1.reshape(1, d), w_in.astype(BF16), g_v.reshape(1, d), w_s.astype(BF16), bs_b,
      w_out.astype(BF16))


def _ln_proj_kernel(h_ref, mod_ref, g1_ref, w_ref, o_ref):
    d = h_ref.shape[1]
    mod = mod_ref[0]
    for rows in _subtiles(h_ref.shape[0]):
        n = _adaln(h_ref[rows, :], g1_ref[...], mod[:, 0:d], mod[:, d:2 * d]).astype(BF16)
        o_ref[rows, :] = jnp.dot(n, w_ref[...], preferred_element_type=F32)


def _time_major_map(tm, n_lat, l_lat, l_ctx):
    lat_tiles, tpl, tpc = n_lat // tm, l_lat // tm, l_ctx // tm

    def index_map(i):
        j = i - lat_tiles
        return (jnp.where(i < lat_tiles, tpc + i % tpl, j % tpc), jnp.where(i < lat_tiles, i // tpl, j // tpc))
    return index_map


def ln_proj(h, mod3, layer, g1, w, seg, n_lat, nb):
    n_rows, d = h.shape
    dn = w.shape[1]
    l_lat, l_ctx = n_lat // nb, (n_rows - n_lat) // nb
    tm = min(SUB, l_ctx)
    return pl.pallas_call(
        _ln_proj_kernel,
        grid=(n_rows // tm,),
        in_specs=[pl.BlockSpec((tm, d), lambda i: (i, 0)),
                  _mod_spec(layer, d, seg // tm),
                  pl.BlockSpec((1, d), lambda i: (0, 0)),
                  pl.BlockSpec((d, dn), lambda i: (0, 0))],
        out_specs=pl.BlockSpec((tm, dn), _time_major_map(tm, n_lat, l_lat, l_ctx)),
        out_shape=jax.ShapeDtypeStruct((l_ctx + l_lat, nb * dn), F32),
        compiler_params=_cparams(("arbitrary",)),
        name="adaln_in_proj",
    )(h, mod3, g1.reshape(1, d), w.astype(BF16))


def _out_proj_kernel(y_ref, h_ref, mod_ref, w_ref, o_ref, *, glu):
    d = h_ref.shape[1]
    mod = mod_ref[0]
    for rows in _subtiles(h_ref.shape[0]):
        z = jnp.dot(y_ref[rows, :], w_ref[...], preferred_element_type=F32)
        if glu:
            z = z[:, :d] * _sigmoid(z[:, d:])
        o_ref[rows, :] = h_ref[rows, :] + mod[:, 2 * d:3 * d] * z


def out_proj(y, h, mod3, layer, w, seg, glu, time_major=None):
    n_rows, d = h.shape
    dk, dn = w.shape
    if time_major is None:
        tm = min(TMK, seg)
        y_map = lambda i: (i, 0)
    else:
        n_lat, nb = time_major
        l_lat, l_ctx = n_lat // nb, (n_rows - n_lat) // nb
        tm = min(SUB, l_ctx)
        y_map = _time_major_map(tm, n_lat, l_lat, l_ctx)
    return pl.pallas_call(
        functools.partial(_out_proj_kernel, glu=glu),
        grid=(n_rows // tm,),
        in_specs=[pl.BlockSpec((tm, dk), y_map),
                  pl.BlockSpec((tm, d), lambda i: (i, 0)),
                  _mod_spec(layer, d, seg // tm),
                  pl.BlockSpec((dk, dn), lambda i: (0, 0))],
        out_specs=pl.BlockSpec((tm, d), lambda i: (i, 0)),
        out_shape=jax.ShapeDtypeStruct((n_rows, d), F32),
        compiler_params=_cparams(("arbitrary",)),
        name="mixer_out_proj",
    )(y, h, mod3, w.astype(BF16))


def _s5_params(lam_re, lam_im, log_dt, b_re, b_im, c_re, c_im):
    ng, npst = lam_re.shape
    gpb = LANES // S5_GROUP_CH
    nj = ng // gpb
    dt = jnp.exp(log_dt.astype(F32))[:, None]
    mag = jnp.exp(lam_re * dt)
    a_re = mag * jnp.cos(lam_im * dt)
    a_im = mag * jnp.sin(lam_im * dt)
    den = lam_re * lam_re + lam_im * lam_im
    n_re = a_re - 1.0
    f_re = (n_re * lam_re + a_im * lam_im) / den
    f_im = (a_im * lam_re - n_re * lam_im) / den
    bb_re = f_re[..., None] * b_re - f_im[..., None] * b_im
    bb_im = f_re[..., None] * b_im + f_im[..., None] * b_re
    eye = jnp.eye(gpb, dtype=F32)

    def blockdiag_in(bb):
        x = bb.reshape(nj, gpb, npst, S5_GROUP_CH).transpose(0, 1, 3, 2)
        return (x[:, :, :, None, :] * eye[None, :, None, :, None]).reshape(nj, LANES, gpb * npst)

    def blockdiag_out(cc):
        x = cc.reshape(nj, gpb, S5_GROUP_CH, npst).transpose(0, 1, 3, 2)
        return (x[:, :, :, None, :] * eye[None, :, None, :, None]).reshape(nj, gpb * npst, LANES)

    b_in = jnp.concatenate([blockdiag_in(bb_re), blockdiag_in(bb_im)], axis=2)
    c_out = jnp.concatenate([blockdiag_out(c_re), blockdiag_out(-c_im)], axis=1)
    a = jnp.stack([a_re.reshape(nj, 1, gpb * npst), a_im.reshape(nj, 1, gpb * npst)], axis=1)
    return a, b_in.astype(BF16), c_out.astype(BF16)


def _s5_dir_kernel(x_ref, *rest, reverse, final):
    nj = rest[-1].shape[0]
    h_ref = rest[-1]
    u_refs = rest[-1 - nj:-1]
    if final:
        yf_ref, d_ref, b_ref, c_ref, a_ref, o_ref = rest[:-1 - nj]
    else:
        b_ref, c_ref, a_ref, o_ref = rest[:-1 - nj]
    sw = b_ref.shape[2] // 2
    steps = x_ref.shape[0] // SUBLANES

    @pl.when(pl.program_id(0) == 0)
    def _():
        h_ref[...] = jnp.zeros_like(h_ref)

    x = x_ref[...]
    xb = x.astype(BF16)
    for j, u_ref in enumerate(u_refs):
        cols = slice(j * LANES, (j + 1) * LANES)
        u_ref[...] = jnp.dot(xb[:, cols], b_ref[j], preferred_element_type=F32)
        a_re = jnp.broadcast_to(a_ref[j, 0], (SUBLANES, sw))
        a_im = jnp.broadcast_to(a_ref[j, 1], (SUBLANES, sw))
        h_re, h_im = h_ref[j, 0], h_ref[j, 1]
        for s in range(steps):
            t = (steps - 1 - s) if reverse else s
            rows = slice(t * SUBLANES, (t + 1) * SUBLANES)
            h_re, h_im = (a_re * h_re - a_im * h_im + u_ref[rows, :sw],
                          a_re * h_im + a_im * h_re + u_ref[rows, sw:])
            u_ref[rows, :sw] = h_re
            u_ref[rows, sw:] = h_im
        h_ref[j, 0] = h_re
        h_ref[j, 1] = h_im
        y = jnp.dot(u_ref[...].astype(BF16), c_ref[j], preferred_element_type=F32)
        if final:
            skip = d_ref[:, cols] * x[:, cols] + yf_ref[:, cols]
            o_ref[:, cols] = _gelu(jnp.maximum(y, -jnp.inf) + skip).astype(BF16)
        else:
            o_ref[:, cols] = y


def s5_layer(h, mod3, layer, g1, p, n_lat, n_ctx, nb, seg):
    d = h.shape[1]
    assert nb == SUBLANES
    l_lat, l_ctx = n_lat // nb, n_ctx // nb
    x_tb = ln_proj(h, mod3, layer, g1, p['w_in'], seg, n_lat, nb).reshape(-1, d)
    rows = S5_TB * nb
    n_blk, n_cblk = (l_ctx + l_lat) // S5_TB, l_ctx // S5_TB
    fwd = lambda i: (i, 0)
    rev = lambda i: (jnp.where(i < n_cblk, n_cblk - 1 - i, n_blk - 1 + n_cblk - i), 0)
    full = lambda a: pl.BlockSpec(a.shape, lambda i: (0,) * a.ndim)
    y = None
    for k, order in enumerate((fwd, rev)):
        a, b_in, c_out = _s5_params(p['lam_re'][k], p['lam_im'][k], p['log_dt'][k], p['b_re'][k], p['b_im'][k],
                                    p['c_re'][k], p['c_im'][k])
        final = k == 1
        blk = pl.BlockSpec((rows, d), order)
        args = [x_tb] + ([y, p['d'].reshape(1, d)] if final else []) + [b_in, c_out, a]
        specs = [blk] + ([blk, full(args[2])] if final else []) + [full(b_in), full(c_out), full(a)]
        y = pl.pallas_call(
            functools.partial(_s5_dir_kernel, reverse=bool(k), final=final),
            grid=(n_blk,),
            in_specs=specs,
            out_specs=blk,
            out_shape=jax.ShapeDtypeStruct(x_tb.shape, BF16 if final else F32),
            scratch_shapes=[pltpu.VMEM((rows, b_in.shape[2]), F32)] * b_in.shape[0]
            + [pltpu.VMEM((b_in.shape[0], 2, SUBLANES, b_in.shape[2] // 2), F32)],
            compiler_params=_cparams(("arbitrary",)),
            name="s5_scan_reverse" if final else "s5_scan_forward",
        )(*args)
    return out_proj(y.reshape(l_ctx + l_lat, nb * d), h, mod3, layer, p['w_glu'], seg, glu=True, time_major=(n_lat, nb))


def _mla_proj_kernel(h_ref, mod_ref, g1_ref, win_ref, gq_ref, gkv_ref, wuq_ref, wukv_ref,
                     gqn_ref, gkn_ref, cos_ref, sin_ref, q_ref, k_ref, v_ref):
    d = h_ref.shape[1]
    mod = mod_ref[0]
    r0 = MLA_Q_RANK + MLA_KV_RANK
    hw = 2 * LANES
    qscale = 1.0 / math.sqrt(MLA_QK)
    gqn, gkn = gqn_ref[...], gkn_ref[...]
    for sb in range(h_ref.shape[0] // SUB):
        rows = slice(sb * SUB, (sb + 1) * SUB)
        n = _adaln(h_ref[rows, :], g1_ref[...], mod[:, 0:d], mod[:, d:2 * d]).astype(BF16)
        z = jnp.dot(n, win_ref[...], preferred_element_type=F32)
        ql = _rms(z[:, :MLA_Q_RANK], gq_ref[...]).astype(BF16)
        kvl = _rms(z[:, MLA_Q_RANK:r0], gkv_ref[...]).astype(BF16)
        pe, pe_sw = z[:, r0:r0 + LANES], z[:, r0 + LANES:r0 + 2 * LANES]
        qa = jnp.dot(ql, wuq_ref[...], preferred_element_type=F32)
        kv = jnp.dot(kvl, wukv_ref[...], preferred_element_type=F32)
        cos, sin = cos_ref[rows, :], sin_ref[rows, :]
        kr = (pe * gkn[1:2]) * cos + (pe_sw * gkn[2:3]) * sin
        pe_ss = jnp.sum(pe * pe, axis=-1, keepdims=True)
        for hd in range(MLA_HEADS):
            qn = qa[:, hd * hw:hd * hw + LANES]
            qr = qa[:, hd * hw + LANES:(hd + 1) * hw]
            qsw = qa[:, MLA_HEADS * hw + hd * LANES:MLA_HEADS * hw + (hd + 1) * LANES]
            ss = jnp.sum(qn * qn, axis=-1, keepdims=True) + jnp.sum(qr * qr, axis=-1, keepdims=True)
            rq = lax.rsqrt(ss * (1.0 / MLA_QK) + EPS) * qscale
            q_ref[rows, hd * hw:hd * hw + LANES] = (qn * rq * gqn[0:1]).astype(BF16)
            q_ref[rows, hd * hw + LANES:(hd + 1) * hw] = (
                rq * ((qr * gqn[1:2]) * cos + (qsw * gqn[2:3]) * sin)).astype(BF16)
            kn = kv[:, hd * LANES:(hd + 1) * LANES]
            rk = lax.rsqrt((jnp.sum(kn * kn, axis=-1, keepdims=True) + pe_ss) * (1.0 / MLA_QK) + EPS)
            k_ref[rows, hd * hw:hd * hw + LANES] = (kn * rk * gkn[0:1]).astype(BF16)
            k_ref[rows, hd * hw + LANES:(hd + 1) * hw] = (kr * rk).astype(BF16)
        v_ref[rows, :] = kv[:, MLA_HEADS * LANES:].astype(BF16)


def _attn_kernel(q_ref, kc_ref, vc_ref, *rest, with_latent):
    if with_latent:
        kl_ref, vl_ref, o_ref = rest
    else:
        (o_ref,) = rest
    nt = (((1,), (1,)), ((), ()))
    tq = q_ref.shape[0]
    qs = min(ATTN_QS, tq)
    for qi in range(tq // qs):
        rows = slice(qi * qs, (qi + 1) * qs)
        q = q_ref[rows, :]
        s1 = lax.dot_general(q, kc_ref[...], nt, preferred_element_type=F32)
        mx = jnp.max(s1, axis=-1, keepdims=True)
        if with_latent:
            s2 = lax.dot_general(q, kl_ref[...], nt, preferred_element_type=F32)
            mx = jnp.maximum(mx, jnp.max(s2, axis=-1, keepdims=True))
        p1 = jnp.exp(s1 - mx)
        den = jnp.sum(p1, axis=-1, keepdims=True)
        o = jnp.dot(p1.astype(BF16), vc_ref[...], preferred_element_type=F32)
        if with_latent:
            p2 = jnp.exp(s2 - mx)
            den = den + jnp.sum(p2, axis=-1, keepdims=True)
            o = o + jnp.dot(p2.astype(BF16), vl_ref[...], preferred_element_type=F32)
        o_ref[rows, :] = (o / den).astype(BF16)


def _rope_tables(l_lat, n_rows_id):
    rows = l_lat // GRID_W
    row = jnp.repeat(jnp.arange(rows), GRID_W).astype(F32)
    col = jnp.tile(jnp.arange(GRID_W), rows).astype(F32)
    quarter = MLA_ROPE // 4
    inv_freq = ROPE_THETA ** (-jnp.arange(quarter, dtype=F32) / quarter)
    ang_r, ang_c = row[:, None] * inv_freq, col[:, None] * inv_freq
    cos = jnp.concatenate([jnp.cos(ang_r), jnp.cos(ang_r), jnp.cos(ang_c), jnp.cos(ang_c)], axis=1)
    sin = jnp.concatenate([-jnp.sin(ang_r), jnp.sin(ang_r), -jnp.sin(ang_c), jnp.sin(ang_c)], axis=1)
    pad = ((0, n_rows_id), (0, LANES - MLA_ROPE))
    cos = jnp.pad(cos, pad).at[l_lat:, :MLA_ROPE].set(1.0)
    return cos, jnp.pad(sin, pad)


def mla_layer(h, mod3, layer, g1, p, n_lat, n_ctx, nb, seg):
    n_rows, d = h.shape
    hw = 2 * LANES
    nh = MLA_HEADS
    l_lat, l_ctx = n_lat // nb, n_ctx // nb
    r0 = MLA_Q_RANK + MLA_KV_RANK
    swap = np.arange(MLA_ROPE)
    swap = np.where(swap % 32 < 16, swap + 16, swap - 16)
    lane_pad = lambda x: jnp.pad(x, ((0, 0),) * (x.ndim - 1) + ((0, LANES - x.shape[-1]),))
    w_pe = p['w_in'][:, r0:]
    w_in_ext = jnp.concatenate([p['w_in'][:, :r0], lane_pad(w_pe), lane_pad(w_pe[:, swap])], axis=1).astype(BF16)
    wq = p['w_uq'].reshape(MLA_Q_RANK, nh, MLA_QK)
    wq_main = jnp.pad(wq, ((0, 0), (0, 0), (0, hw - MLA_QK))).reshape(MLA_Q_RANK, nh * hw)
    wq_sw = lane_pad(wq[:, :, MLA_NOPE:][:, :, swap]).reshape(MLA_Q_RANK, nh * LANES)
    w_uq_ext = jnp.concatenate([wq_main, wq_sw], axis=1).astype(BF16)
    wkv = p['w_ukv'].reshape(MLA_KV_RANK, nh, 2, LANES).transpose(0, 2, 1, 3).reshape(MLA_KV_RANK, 2 * nh * LANES)
    gains = lambda g: jnp.zeros((SUBLANES, LANES), F32).at[0].set(g[:MLA_NOPE]).at[1, :MLA_ROPE].set(
        g[MLA_NOPE:]).at[2, :MLA_ROPE].set(g[MLA_NOPE:][swap])
    cos, sin = _rope_tables(l_lat, seg)
    tm = min(TMK // 2, seg)
    t_lat = l_lat // tm
    tps = seg // tm
    rope_map = lambda i: (jnp.where(i < (n_lat // tm), i % t_lat, t_lat + i % tps), 0)
    full = lambda *shape: pl.BlockSpec(shape, lambda i: (0,) * len(shape))
    q, k, v = pl.pallas_call(
        _mla_proj_kernel,
        grid=(n_rows // tm,),
        in_specs=[pl.BlockSpec((tm, d), lambda i: (i, 0)), _mod_spec(layer, d, tps), full(1, d),
                  full(d, r0 + 2 * LANES), full(1, MLA_Q_RANK), full(1, MLA_KV_RANK),
                  full(MLA_Q_RANK, nh * (hw + LANES)), full(MLA_KV_RANK, 2 * nh * LANES),
                  full(SUBLANES, LANES), full(SUBLANES, LANES),
                  pl.BlockSpec((tm, LANES), rope_map), pl.BlockSpec((tm, LANES), rope_map)],
        out_specs=[pl.BlockSpec((tm, nh * hw), lambda i: (i, 0)), pl.BlockSpec((tm, nh * hw), lambda i: (i, 0)),
                   pl.BlockSpec((tm, nh * LANES), lambda i: (i, 0))],
        out_shape=[jax.ShapeDtypeStruct((n_rows, nh * hw), BF16), jax.ShapeDtypeStruct((n_rows, nh * hw), BF16),
                   jax.ShapeDtypeStruct((n_rows, nh * LANES), BF16)],
        compiler_params=_cparams(("arbitrary",)),
        name="mla_projection",
    )(h, mod3, g1.reshape(1, d), w_in_ext, p['g_q'].reshape(1, -1), p['g_kv'].reshape(1, -1), w_uq_ext,
      wkv.astype(BF16), gains(p['g_qn']), gains(p['g_kn']), cos, sin)

    tq = min(ATTN_TQ, l_lat)
    cb = n_lat // l_ctx
    o_lat = pl.pallas_call(
        functools.partial(_attn_kernel, with_latent=True),
        grid=(nb, nh, l_lat // tq),
        in_specs=[pl.BlockSpec((tq, hw), lambda b, hd, i: (b * (l_lat // tq) + i, hd)),
                  pl.BlockSpec((l_ctx, hw), lambda b, hd, i: (cb + b, hd)),
                  pl.BlockSpec((l_ctx, LANES), lambda b, hd, i: (cb + b, hd)),
                  pl.BlockSpec((l_lat, hw), lambda b, hd, i: (b, hd)),
                  pl.BlockSpec((l_lat, LANES), lambda b, hd, i: (b, hd))],
        out_specs=pl.BlockSpec((tq, LANES), lambda b, hd, i: (b * (l_lat // tq) + i, hd)),
        out_shape=jax.ShapeDtypeStruct((n_lat, nh * LANES), BF16),
        compiler_params=_cparams(("arbitrary", "arbitrary", "arbitrary")),
        name="mla_attention_latent",
    )(q, k, v, k, v)
    o_ctx = pl.pallas_call(
        functools.partial(_attn_kernel, with_latent=False),
        grid=(nb, nh),
        in_specs=[pl.BlockSpec((l_ctx, hw), lambda b, hd: (cb + b, hd)),
                  pl.BlockSpec((l_ctx, hw), lambda b, hd: (cb + b, hd)),
                  pl.BlockSpec((l_ctx, LANES), lambda b, hd: (cb + b, hd))],
        out_specs=pl.BlockSpec((l_ctx, LANES), lambda b, hd: (b, hd)),
        out_shape=jax.ShapeDtypeStruct((n_ctx, nh * LANES), BF16),
        compiler_params=_cparams(("arbitrary", "arbitrary")),
        name="mla_attention_context",
    )(q, k, v)
    return out_proj(jnp.concatenate([o_lat, o_ctx], axis=0), h, mod3, layer, p['w_out'], seg, glu=False)


def _route_kernel(h_ref, mod_ref, g2_ref, wrt_ref, br_ref, n2_ref, pos_ref, gate_ref, cnt_ref):
    d = h_ref.shape[1]
    tm = h_ref.shape[0]
    ne = wrt_ref.shape[1]
    mod = mod_ref[0]
    n2 = _adaln(h_ref[...], g2_ref[...], mod[:, 3 * d:4 * d], mod[:, 4 * d:5 * d])
    n2_hi = n2.astype(BF16)
    n2_ref[...] = n2_hi
    n2_lo = (n2 - n2_hi.astype(F32)).astype(BF16)
    nt = (((1,), (1,)), ((), ()))
    w_hi, w_lo = wrt_ref[0], wrt_ref[1]
    logits = (lax.dot_general(w_hi, n2_hi, nt, preferred_element_type=F32)
              + (lax.dot_general(w_hi, n2_lo, nt, preferred_element_type=F32)
                 + lax.dot_general(w_lo, n2_hi, nt, preferred_element_type=F32))) + br_ref[...]
    eidx = lax.broadcasted_iota(I32, (ne, tm), 0).astype(F32)
    work = logits
    vals, idxs = [], []
    for _ in range(TOP_K):
        mx = jnp.max(work, axis=0, keepdims=True)
        idx = jnp.min(jnp.where(work == mx, eidx, float(ne)), axis=0, keepdims=True)
        vals.append(mx)
        idxs.append(idx)
        work = jnp.where(eidx == idx, -jnp.inf, work)
    ex = [jnp.exp(v - vals[0]) for v in vals]
    den = ex[0] + ex[1] + ex[2] + ex[3]
    sel = (eidx == idxs[0]) | (eidx == idxs[1]) | (eidx == idxs[2]) | (eidx == idxs[3])
    onehot = jnp.where(sel, 1.0, 0.0).astype(BF16)
    r = lax.broadcasted_iota(I32, (tm, tm), 0)
    c = lax.broadcasted_iota(I32, (tm, tm), 1)
    before = jnp.where(r < c, 1.0, 0.0).astype(BF16)
    rank_all = jnp.dot(onehot, before, preferred_element_type=F32)
    cnt = jnp.dot(onehot, jnp.ones((tm, tm), BF16), preferred_element_type=F32)
    cnt8 = jnp.floor((cnt + (SUBLANES - 1)) * (1.0 / SUBLANES)) * SUBLANES
    er = lax.broadcasted_iota(I32, (ne, LANES), 0)
    ec = lax.broadcasted_iota(I32, (ne, LANES), 1)
    lower = jnp.where(ec < er, 1.0, 0.0).astype(BF16)
    cnt8_p = jnp.concatenate([cnt8, jnp.zeros((LANES - ne, tm), F32)], axis=0).astype(BF16)
    off = jnp.dot(lower, cnt8_p, preferred_element_type=F32)
    base = rank_all + off
    for k in range(TOP_K):
        pos = jnp.sum(jnp.where(eidx == idxs[k], base, 0.0), axis=0, keepdims=True)
        pos_ref[k:k + 1, :] = pos.astype(I32)
        gate_ref[k:k + 1, :] = ex[k] / den
    cnt_ref[0] = cnt8[:, :LANES].astype(I32)


def moe_route(h, mod3, layer, g2, w_router, b_router, n_rows, seg):
    d = h.shape[1]
    ne = w_router.shape[1]
    nt = n_rows // TM
    wt = w_router.T
    wt_hi = wt.astype(BF16)
    wt_split = jnp.stack([wt_hi, (wt - wt_hi.astype(F32)).astype(BF16)])
    return pl.pallas_call(
        _route_kernel,
        grid=(nt,),
        in_specs=[pl.BlockSpec((TM, d), lambda i: (i, 0)), _mod_spec(layer, d, seg // TM),
                  pl.BlockSpec((1, d), lambda i: (0, 0)), pl.BlockSpec((2, ne, d), lambda i: (0, 0, 0)),
                  pl.BlockSpec((ne, 1), lambda i: (0, 0))],
        out_specs=[pl.BlockSpec((TM, d), lambda i: (i, 0)), pl.BlockSpec((TOP_K, TM), lambda i: (0, i)),
                   pl.BlockSpec((TOP_K, TM), lambda i: (0, i)), pl.BlockSpec((1, ne, LANES), lambda i: (i, 0, 0))],
        out_shape=[jax.ShapeDtypeStruct((n_rows, d), BF16), jax.ShapeDtypeStruct((TOP_K, n_rows), I32),
                   jax.ShapeDtypeStruct((TOP_K, n_rows), F32), jax.ShapeDtypeStruct((nt, ne, LANES), I32)],
        compiler_params=_cparams(("arbitrary",)),
        name="moe_route",
    )(h, mod3, g2.reshape(1, d), wt_split, b_router.reshape(ne, 1))


def _chunk_copies(tab_ref, tile, e, vm_ref, hbm_ref, sem, to_hbm):
    ne = N_EXPERTS
    stride = tab_ref.shape[0] // 3
    off = tab_ref[tile * ne + e]
    n8 = tab_ref[stride + tile * ne + e]
    dst = tab_ref[2 * stride + tile * ne + e]
    out = []
    for size in PIECES:
        bit = (size // SUBLANES).bit_length() - 1
        done = ((n8 >> (bit + 1)) << (bit + 1)) * SUBLANES
        lo = pl.multiple_of(off + done, SUBLANES)
        hi = pl.multiple_of(dst + done, SUBLANES)
        v, hb = vm_ref.at[pl.ds(lo, size), :], hbm_ref.at[pl.ds(hi, size), :]
        cp = pltpu.make_async_copy(v, hb, sem) if to_hbm else pltpu.make_async_copy(hb, v, sem)
        out.append((((n8 >> bit) & 1) == 1, cp))
    return out


def _for_each_copy(tab_ref, tile, vm_ref, hbm_ref, sem, to_hbm, action):
    small = PIECES.index(32)

    def body(e, carry):
        copies = _chunk_copies(tab_ref, tile, e, vm_ref, hbm_ref, sem, to_hbm)

        def run(items):
            for cond, cp in items:
                @pl.when(cond)
                def _():
                    action(cp)
        run(copies[small:])

        @pl.when(tab_ref[tab_ref.shape[0] // 3 + tile * N_EXPERTS + e] >= 2 * PIECES[small] // SUBLANES)
        def _():
            run(copies[:small])
        return carry
    lax.fori_loop(0, N_EXPERTS, body, 0)


def _wait_rows(n_rows, vm_ref, hbm_ref, sem, to_hbm):
    n8 = n_rows >> (SUBLANES.bit_length() - 1)
    size = SUBLANES
    while size * 2 <= vm_ref.shape[0]:
        size *= 2
    while size >= SUBLANES:
        v, hb = vm_ref.at[pl.ds(0, size), :], hbm_ref.at[pl.ds(0, size), :]
        cp = pltpu.make_async_copy(v, hb, sem) if to_hbm else pltpu.make_async_copy(hb, v, sem)

        @pl.when(((n8 >> ((size // SUBLANES).bit_length() - 1)) & 1) == 1)
        def _():
            cp.wait()
        size //= 2


def _dispatch_kernel(tab_ref, rows_ref, tail_ref, nb_ref, n2_ref, pos_ref, xb_ref, sorted_ref, zero_ref, sem, zsem):
    i = pl.program_id(0)
    last = pl.num_programs(0) - 1
    slot = i & 1
    cur = sorted_ref.at[slot]
    tm = n2_ref.shape[0]
    rmax = sorted_ref.shape[1]
    ridx = lax.broadcasted_iota(I32, (rmax, tm), 0)
    perm = jnp.where(ridx == pos_ref[0:1, :], 1.0, 0.0)
    for k in range(1, TOP_K):
        perm = perm + jnp.where(ridx == pos_ref[k:k + 1, :], 1.0, 0.0)
    perm = perm.astype(BF16)
    cur[...] = _pack_bf16_pairs(jnp.dot(perm, n2_ref[...], preferred_element_type=F32))
    _for_each_copy(tab_ref, i, cur, xb_ref, sem.at[slot], True, lambda cp: cp.start())

    @pl.when(i > 0)
    def _():
        _wait_rows(rows_ref[i - 1], sorted_ref.at[1 - slot], xb_ref, sem.at[1 - slot], True)

    @pl.when(i == last)
    def _():
        _wait_rows(rows_ref[i], cur, xb_ref, sem.at[slot], True)
        zero_ref[...] = jnp.zeros_like(zero_ref)
        _for_each_copy(tail_ref, 0, zero_ref, xb_ref, zsem, True, lambda cp: cp.start())
        _for_each_copy(tail_ref, 0, zero_ref, xb_ref, zsem, True, lambda cp: cp.wait())

        zr = zero_ref.shape[0]

        def spare(b, carry):
            cp = pltpu.make_async_copy(zero_ref, xb_ref.at[pl.ds(pl.multiple_of(b * zr, zr), zr), :], zsem)
            cp.start()
            cp.wait()
            return carry
        lax.fori_loop(nb_ref[0] * (MOE_BM // zr), xb_ref.shape[0] // zr, spare, 0)


def _expert_kernel(be_ref, first_ref, nb_ref, x_ref, wg_ref, bg_ref, wu_ref, bu_ref, wd_ref, bd_ref, y_ref,
                   wg_s, wu_s, wd_s):
    b = pl.program_id(0)

    @pl.when(b < nb_ref[0])
    def _():
        @pl.when((first_ref[b] & 1) == 1)
        def _():
            wg_s[...] = wg_ref[0, 0].astype(BF16)
            wu_s[...] = wu_ref[0, 0].astype(BF16)
            wd_s[...] = wd_ref[0, 0].astype(BF16)

        def ffn(rows):
            x = _unpack_bf16_pairs(x_ref[rows, :])
            g = jnp.minimum(jnp.dot(x, wg_s[...], preferred_element_type=F32) + bg_ref[0], SWIGLU_LIMIT)
            u = jnp.clip(jnp.dot(x, wu_s[...], preferred_element_type=F32) + bu_ref[0], -SWIGLU_LIMIT, SWIGLU_LIMIT)
            a = (g * _sigmoid(SWIGLU_ALPHA * g) * (u + 1.0)).astype(BF16)
            y = jnp.dot(a, wd_s[...], preferred_element_type=F32) + bd_ref[0]
            y_ref[rows, :] = _pack_bf16_pairs(y.astype(BF16).astype(F32))

        n_sub = x_ref.shape[0] // MOE_SUB
        live = first_ref[b] >> 1

        @pl.when(live == n_sub)
        def _():
            for sb in range(n_sub):
                ffn(slice(sb * MOE_SUB, (sb + 1) * MOE_SUB))

        @pl.when(live < n_sub)
        def _():
            for sb in range(n_sub):
                rows = slice(sb * MOE_SUB, (sb + 1) * MOE_SUB)

                @pl.when(sb < live)
                def _():
                    ffn(rows)

                @pl.when(sb >= live)
                def _():
                    y_ref[rows, :] = jnp.zeros((MOE_SUB, y_ref.shape[1]), y_ref.dtype)

    @pl.when(b >= nb_ref[0])
    def _():
        y_ref[...] = jnp.zeros_like(y_ref)


def _combine_kernel(tab_ref, rows_ref, pos_ref, gate_ref, h_ref, mod_ref, yb_ref, o_ref, ys_ref, sem):
    i = pl.program_id(0)
    slot = i & 1
    d = h_ref.shape[1]
    tm = h_ref.shape[0]
    rmax = ys_ref.shape[1]

    @pl.when(i == 0)
    def _():
        ys_ref[...] = jnp.zeros_like(ys_ref)
        _for_each_copy(tab_ref, 0, ys_ref.at[0], yb_ref, sem.at[0], False, lambda cp: cp.start())

    @pl.when(i + 1 < pl.num_programs(0))
    def _():
        _for_each_copy(tab_ref, i + 1, ys_ref.at[1 - slot], yb_ref, sem.at[1 - slot], False, lambda cp: cp.start())

    cidx = lax.broadcasted_iota(I32, (tm, rmax), 1)
    wt = jnp.where(cidx == pos_ref[:, 0:1], gate_ref[:, 0:1], 0.0)
    for k in range(1, TOP_K):
        wt = wt + jnp.where(cidx == pos_ref[:, k:k + 1], gate_ref[:, k:k + 1], 0.0)
    _wait_rows(rows_ref[i], ys_ref.at[slot], yb_ref, sem.at[slot], False)
    f = jnp.dot(wt.astype(BF16), _unpack_bf16_pairs(ys_ref[slot]), preferred_element_type=F32)
    o_ref[...] = h_ref[...] + mod_ref[0][:, 5 * d:6 * d] * f


def _moe_plan(cnt8, n_blocks_max):
    nt, ne = cnt8.shape
    loc_off = jnp.cumsum(cnt8, axis=1) - cnt8
    used = jnp.sum(cnt8, axis=0)
    rows_e = (used + MOE_BM - 1) // MOE_BM * MOE_BM
    e_end = jnp.cumsum(rows_e)
    e_start = e_end - rows_e
    dst = e_start[None, :] + jnp.cumsum(cnt8, axis=0) - cnt8
    tab = jnp.stack([loc_off, cnt8 // SUBLANES, dst]).reshape(3 * nt * ne).astype(I32)
    tail = jnp.stack([jnp.zeros((ne,), I32), (rows_e - used) // SUBLANES, e_start + used]).reshape(3 * ne).astype(I32)
    nb = (e_end[-1] // MOE_BM).astype(I32)
    blk = jnp.arange(n_blocks_max, dtype=I32)
    blk_c = jnp.minimum(blk, nb - 1)
    blk_e = jnp.sum((e_end[None, :] <= (blk_c * MOE_BM)[:, None]).astype(I32), axis=1)
    blk_e = jnp.minimum(blk_e, ne - 1)
    owner = (blk_e[:, None] == jnp.arange(ne, dtype=I32)[None, :]).astype(I32)
    start_b = jnp.sum(owner * e_start[None, :].astype(I32), axis=1)
    used_b = jnp.sum(owner * used[None, :].astype(I32), axis=1)
    first = (blk_c * MOE_BM == start_b).astype(I32)
    live = jnp.clip(start_b + used_b - blk_c * MOE_BM, 0, MOE_BM)
    first = first + 2 * ((live + MOE_SUB - 1) // MOE_SUB).astype(I32)
    tile_rows = jnp.sum(cnt8, axis=1).astype(I32)
    return tab, tile_rows, tail, blk_e, first, nb.reshape(1)


def moe_layer(h, mod3, layer, g2, p, n_rows, seg):
    d = h.shape[1]
    ne = N_EXPERTS
    nt = n_rows // TM
    n2, pos, gate, cnt = moe_route(h, mod3, layer, g2, p['w_router'], p['b_router'], n_rows, seg)
    max_rows = n_rows * TOP_K + nt * ne * (SUBLANES - 1) + ne * (MOE_BM - 1)
    nbm = -(-max_rows // MOE_BM)
    dp = d // 2
    tab, tile_rows, tail, blk_e, first, nb = _moe_plan(cnt[:, :, 0], nbm)

    xb = pl.pallas_call(
        _dispatch_kernel,
        grid_spec=pltpu.PrefetchScalarGridSpec(
            num_scalar_prefetch=4, grid=(nt,),
            in_specs=[pl.BlockSpec((TM, d), lambda i, *_: (i, 0)),
                      pl.BlockSpec((TOP_K, TM), lambda i, *_: (0, i))],
            out_specs=pl.BlockSpec(memory_space=pl.ANY),
            scratch_shapes=[pltpu.VMEM((2, MOE_RMAX, dp), U32), pltpu.VMEM((MOE_BM, dp), U32),
                            pltpu.SemaphoreType.DMA((2,)), pltpu.SemaphoreType.DMA(())]),
        out_shape=jax.ShapeDtypeStruct((nbm * MOE_BM, dp), U32),
        compiler_params=_cparams(("arbitrary",)),
        name="moe_dispatch",
    )(tab, tile_rows, tail, nb, n2, pos)

    row_map = lambda b, be, fi, nbr: (jnp.maximum(jnp.minimum(b, nbr[0] - 1), 0), 0)
    out_map = lambda b, be, fi, nbr: (b, 0)
    w_map = lambda b, be, fi, nbr: (layer, be[b], 0, 0)
    b_map = lambda b, be, fi, nbr: (layer * ne + be[b], 0, 0)
    yb = pl.pallas_call(
        _expert_kernel,
        grid_spec=pltpu.PrefetchScalarGridSpec(
            num_scalar_prefetch=3, grid=(nbm,),
            in_specs=[pl.BlockSpec((MOE_BM, dp), row_map),
                      pl.BlockSpec((1, 1, d, d), w_map), pl.BlockSpec((1, 1, d), b_map),
                      pl.BlockSpec((1, 1, d, d), w_map), pl.BlockSpec((1, 1, d), b_map),
                      pl.BlockSpec((1, 1, d, d), w_map), pl.BlockSpec((1, 1, d), b_map)],
            out_specs=pl.BlockSpec((MOE_BM, dp), out_map),
            scratch_shapes=[pltpu.VMEM((d, d), BF16)] * 3),
        out_shape=jax.ShapeDtypeStruct((nbm * MOE_BM, dp), U32),
        compiler_params=_cparams(("arbitrary",)),
        name="moe_experts",
    )(blk_e, first, nb, xb, p['w_gate'], p['b_gate'].reshape(-1, 1, d), p['w_up'], p['b_up'].reshape(-1, 1, d),
      p['w_down'], p['b_down'].reshape(-1, 1, d))

    return pl.pallas_call(
        _combine_kernel,
        grid_spec=pltpu.PrefetchScalarGridSpec(
            num_scalar_prefetch=2, grid=(nt,),
            in_specs=[pl.BlockSpec((TM, TOP_K), lambda i, *_: (i, 0)),
                      pl.BlockSpec((TM, TOP_K), lambda i, *_: (i, 0)),
                      pl.BlockSpec((TM, d), lambda i, *_: (i, 0)),
                      pl.BlockSpec((1, 1, 6 * d), lambda i, *_: (layer * 16 + i // (seg // TM), 0, 0)),
                      pl.BlockSpec(memory_space=pl.ANY)],
            out_specs=pl.BlockSpec((TM, d), lambda i, *_: (i, 0)),
            scratch_shapes=[pltpu.VMEM((2, MOE_RMAX, dp), U32), pltpu.SemaphoreType.DMA((2,))]),
        out_shape=jax.ShapeDtypeStruct((n_rows, d), F32),
        compiler_params=_cparams(("arbitrary",)),
        name="moe_combine",
    )(tab, tile_rows, pos.T, gate.T, h, mod3, yb)


def kernel(x, c, ctx, c_ctx, w_ada, b_ada, g_norm1, g_norm2, sg_w_in, sg_g_v, sg_w_s, sg_b_s, sg_w_out,
           ssm_w_in, ssm_lam_re, ssm_lam_im, ssm_log_dt, ssm_b_re, ssm_b_im, ssm_c_re, ssm_c_im, ssm_d, ssm_w_glu,
           mla_w_in, mla_g_q, mla_g_kv, mla_w_uq, mla_w_ukv, mla_g_qn, mla_g_kn, mla_w_out,
           moe_w_router, moe_b_router, moe_w_gate, moe_b_gate, moe_w_up, moe_b_up, moe_w_down, moe_b_down):
    nb, l_lat, d = x.shape
    l_ctx = ctx.shape[1]
    depth = w_ada.shape[0]
    n_lat, n_ctx = nb * l_lat, nb * l_ctx
    seg = l_lat
    assert n_ctx <= seg and seg % TM == 0 and n_ctx % TM == 0
    mod3 = modulation(c, c_ctx, w_ada, b_ada)
    h = jnp.concatenate([x.reshape(n_lat, d), ctx.reshape(n_ctx, d)], axis=0)
    for i in range(depth):
        mixer, slot = i % N_MIXERS, i // N_MIXERS
        ctx_out = i < depth - 1
        n_rows = n_lat + n_ctx if ctx_out else n_lat
        if mixer == 0:
            hm = gmlp_layer(h, mod3, i, g_norm1[i], sg_w_in[slot], sg_g_v[slot], sg_w_s[slot], sg_b_s[slot],
                            sg_w_out[slot], n_rows, seg)
        elif mixer == 1:
            p = dict(w_in=ssm_w_in[slot], lam_re=ssm_lam_re[slot], lam_im=ssm_lam_im[slot], log_dt=ssm_log_dt[slot],
                     b_re=ssm_b_re[slot], b_im=ssm_b_im[slot], c_re=ssm_c_re[slot], c_im=ssm_c_im[slot],
                     d=ssm_d[slot], w_glu=ssm_w_glu[slot])
            hm = s5_layer(h, mod3, i, g_norm1[i], p, n_lat, n_ctx, nb, seg)[:n_rows]
        else:
            p = dict(w_in=mla_w_in[slot], g_q=mla_g_q[slot], g_kv=mla_g_kv[slot], w_uq=mla_w_uq[slot],
                     w_ukv=mla_w_ukv[slot], g_qn=mla_g_qn[slot], g_kn=mla_g_kn[slot], w_out=mla_w_out[slot])
            hm = mla_layer(h, mod3, i, g_norm1[i], p, n_lat, n_ctx, nb, seg)[:n_rows]
        pm = dict(w_router=moe_w_router[i], b_router=moe_b_router[i], w_gate=moe_w_gate, b_gate=moe_b_gate,
                  w_up=moe_w_up, b_up=moe_b_up, w_down=moe_w_down, b_down=moe_b_down)
        h = moe_layer(hm, mod3, i, g_norm2[i], pm, n_rows, seg)
    return h[:n_lat].reshape(nb, l_lat, d)
```

```python
import functools
import math

import jax
import jax.numpy as jnp
import numpy as np
from jax import lax
from jax.experimental import pallas as pl
from jax.experimental.pallas import tpu as pltpu

F32 = jnp.float32
BF16 = jnp.bfloat16
I32 = jnp.int32
U32 = jnp.uint32
EPS = 1e-6

N_MIXERS = 3
GRID_W = 64
CHUNK = 128
SG_GROUPS = 8
S5_GROUP_CH = 16
S5_STATE = 64
MLA_HEADS = 8
MLA_NOPE = 128
MLA_ROPE = 64
MLA_QK = MLA_NOPE + MLA_ROPE
MLA_V = 128
MLA_Q_RANK = 384
MLA_KV_RANK = 256
ROPE_THETA = 10000.0
N_EXPERTS = 32
TOP_K = 4
SWIGLU_LIMIT = 7.0
SWIGLU_ALPHA = 1.702

LANES = 128
SUBLANES = 8
VMEM_LIMIT = 56 * 1024 * 1024

TM = 256
TMK = 1024
SUB = 256
S5_TB = 64
ATTN_TQ = 2048
ATTN_QS = 256
MOE_BM = 1024
MOE_SUB = 256
MOE_RMAX = TM * TOP_K + N_EXPERTS * SUBLANES
PIECES = (512, 256, 128, 64, 32, 16, 8)


def _cparams(sem):
    return pltpu.CompilerParams(dimension_semantics=sem, vmem_limit_bytes=VMEM_LIMIT)


def _rms(x, g):
    ms = jnp.mean(x * x, axis=-1, keepdims=True)
    return x * lax.rsqrt(ms + EPS) * g


def _adaln(x, g, shift, scale):
    return _rms(x, g) * (1.0 + scale) + shift


def _gelu(x):
    return 0.5 * x * (1.0 + lax.erf(x * (1.0 / math.sqrt(2.0))))


def _sigmoid(x):
    return 1.0 / (1.0 + jnp.exp(-x))


def _pack_bf16_pairs(x):
    k = x.shape[1] // 2
    lo = lax.bitcast_convert_type(x[:, :k], U32) >> 16
    hi = lax.bitcast_convert_type(x[:, k:], U32) & jnp.uint32(0xFFFF0000)
    return hi | lo


def _unpack_bf16_pairs(u):
    lo = lax.bitcast_convert_type(u << 16, F32).astype(BF16)
    hi = lax.bitcast_convert_type(u & jnp.uint32(0xFFFF0000), F32).astype(BF16)
    return jnp.concatenate([lo, hi], axis=1)


def _subtiles(rows):
    n = max(1, rows // SUB)
    return [slice(i * (rows // n), (i + 1) * (rows // n)) for i in range(n)]


def _mod_spec(layer, d, tiles_per_seg):
    return pl.BlockSpec((1, 1, 6 * d), lambda i: (layer * 16 + i // tiles_per_seg, 0, 0))


def _mod_kernel(s_ref, w_ref, b_ref, o_ref):
    s = s_ref[...]
    s = s * _sigmoid(s)
    w = w_ref[0]
    s_hi, w_hi = s.astype(BF16), w.astype(BF16)
    s_lo, w_lo = (s - s_hi.astype(F32)).astype(BF16), (w - w_hi.astype(F32)).astype(BF16)
    acc = jnp.dot(s_hi, w_lo, preferred_element_type=F32) + jnp.dot(s_lo, w_hi, preferred_element_type=F32)
    o_ref[0] = jnp.dot(s_hi, w_hi, preferred_element_type=F32) + acc + b_ref[0]


def modulation(c, c_ctx, w_ada, b_ada):
    depth, d, d6 = w_ada.shape
    nb = c.shape[0]
    s = jnp.zeros((16, d), F32).at[:nb].set(c).at[nb].set(c_ctx)
    tn = 1024
    out = pl.pallas_call(
        _mod_kernel,
        grid=(depth, d6 // tn),
        in_specs=[pl.BlockSpec((16, d), lambda l, j: (0, 0)),
                  pl.BlockSpec((1, d, tn), lambda l, j: (l, 0, j)),
                  pl.BlockSpec((1, 1, tn), lambda l, j: (l, 0, j))],
        out_specs=pl.BlockSpec((1, 16, tn), lambda l, j: (l, 0, j)),
        out_shape=jax.ShapeDtypeStruct((depth, 16, d6), F32),
        compiler_params=_cparams(("arbitrary", "arbitrary")),
        name="adaln_modulation",
    )(s, w_ada, b_ada.reshape(depth, 1, d6))
    return out.reshape(depth * 16, 1, d6)


def _gmlp_kernel(h_ref, mod_ref, g1_ref, win_ref, gv_ref, ws_ref, bs_ref, wout_ref, o_ref, gated_ref):
    d = h_ref.shape[1]
    mod = mod_ref[0]
    gd = d // SG_GROUPS
    for sb in range(h_ref.shape[0] // SUB):
        r0 = sb * SUB
        h = h_ref[r0:r0 + SUB, :]
        n = _adaln(h, g1_ref[...], mod[:, 0:d], mod[:, d:2 * d]).astype(BF16)
        z = _gelu(jnp.dot(n, win_ref[...], preferred_element_type=F32))
        u = z[:, :d]
        v = _rms(z[:, d:], gv_ref[...]).astype(BF16)
        for j in range(SUB // CHUNK):
            rows = slice(j * CHUNK, (j + 1) * CHUNK)
            out_rows = slice(r0 + j * CHUNK, r0 + (j + 1) * CHUNK)
            for g in range(SG_GROUPS):
                cols = slice(g * gd, (g + 1) * gd)
                sv = jnp.dot(ws_ref[g], v[rows, cols], preferred_element_type=F32) + bs_ref[g]
                gated_ref[out_rows, cols] = (u[rows, cols] * sv).astype(BF16)
        m = jnp.dot(gated_ref[r0:r0 + SUB, :], wout_ref[...], preferred_element_type=F32)
        o_ref[r0:r0 + SUB, :] = h + mod[:, 2 * d:3 * d] * m


def gmlp_layer(h, mod3, layer, g1, w_in, g_v, w_s, b_s, w_out, n_rows, seg):
    d = h.shape[1]
    gd = d // SG_GROUPS
    tm = min(TMK, seg)
    bs_b = jnp.broadcast_to(b_s[:, :, None], (SG_GROUPS, CHUNK, gd)).astype(F32)
    full = lambda *shape: pl.BlockSpec(shape, lambda i: (0,) * len(shape))
    return pl.pallas_call(
        _gmlp_kernel,
        grid=(n_rows // tm,),
        in_specs=[pl.BlockSpec((tm, d), lambda i: (i, 0)),
                  _mod_spec(layer, d, seg // tm),
                  full(1, d), full(d, 2 * d), full(1, d),
                  full(SG_GROUPS, CHUNK, CHUNK), full(SG_GROUPS, CHUNK, gd), full(d, d)],
        out_specs=pl.BlockSpec((tm, d), lambda i: (i, 0)),
        out_shape=jax.ShapeDtypeStruct((n_rows, d), F32),
        scratch_shapes=[pltpu.VMEM((tm, d), BF16)],
        compiler_params=_cparams(("arbitrary",)),
        name="gmlp_mixer",
    )(h, mod3, g1.reshape(1, d), w_in.astype(BF16), g_v.reshape(1, d), w_s.astype(BF16), bs_b,
      w_out.astype(BF16))


def _ln_proj_kernel(h_ref, mod_ref, g1_ref, w_ref, o_ref):
    d = h_ref.shape[1]
    mod = mod_ref[0]
    for rows in _subtiles(h_ref.shape[0]):
        n = _adaln(h_ref[rows, :], g1_ref[...], mod[:, 0:d], mod[:, d:2 * d]).astype(BF16)
        o_ref[rows, :] = jnp.dot(n, w_ref[...], preferred_element_type=F32)


def _time_major_map(tm, n_lat, l_lat, l_ctx):
    lat_tiles, tpl, tpc = n_lat // tm, l_lat // tm, l_ctx // tm

    def index_map(i):
        j = i - lat_tiles
        return (jnp.where(i < lat_tiles, tpc + i % tpl, j % tpc), jnp.where(i < lat_tiles, i // tpl, j // tpc))
    return index_map


def ln_proj(h, mod3, layer, g1, w, seg, n_lat, nb):
    n_rows, d = h.shape
    dn = w.shape[1]
    l_lat, l_ctx = n_lat // nb, (n_rows - n_lat) // nb
    tm = min(SUB, l_ctx)
    return pl.pallas_call(
        _ln_proj_kernel,
        grid=(n_rows // tm,),
        in_specs=[pl.BlockSpec((tm, d), lambda i: (i, 0)),
                  _mod_spec(layer, d, seg // tm),
                  pl.BlockSpec((1, d), lambda i: (0, 0)),
                  pl.BlockSpec((d, dn), lambda i: (0, 0))],
        out_specs=pl.BlockSpec((tm, dn), _time_major_map(tm, n_lat, l_lat, l_ctx)),
        out_shape=jax.ShapeDtypeStruct((l_ctx + l_lat, nb * dn), F32),
        compiler_params=_cparams(("arbitrary",)),
        name="adaln_in_proj",
    )(h, mod3, g1.reshape(1, d), w.astype(BF16))


def _out_proj_kernel(y_ref, h_ref, mod_ref, w_ref, o_ref, *, glu):
    d = h_ref.shape[1]
    mod = mod_ref[0]
    for rows in _subtiles(h_ref.shape[0]):
        z = jnp.dot(y_ref[rows, :], w_ref[...], preferred_element_type=F32)
        if glu:
            z = z[:, :d] * _sigmoid(z[:, d:])
        o_ref[rows, :] = h_ref[rows, :] + mod[:, 2 * d:3 * d] * z


def out_proj(y, h, mod3, layer, w, seg, glu, time_major=None):
    n_rows, d = h.shape
    dk, dn = w.shape
    if time_major is None:
        tm = min(TMK, seg)
        y_map = lambda i: (i, 0)
    else:
        n_lat, nb = time_major
        l_lat, l_ctx = n_lat // nb, (n_rows - n_lat) // nb
        tm = min(SUB, l_ctx)
        y_map = _time_major_map(tm, n_lat, l_lat, l_ctx)
    return pl.pallas_call(
        functools.partial(_out_proj_kernel, glu=glu),
        grid=(n_rows // tm,),
        in_specs=[pl.BlockSpec((tm, dk), y_map),
                  pl.BlockSpec((tm, d), lambda i: (i, 0)),
                  _mod_spec(layer, d, seg // tm),
                  pl.BlockSpec((dk, dn), lambda i: (0, 0))],
        out_specs=pl.BlockSpec((tm, d), lambda i: (i, 0)),
        out_shape=jax.ShapeDtypeStruct((n_rows, d), F32),
        compiler_params=_cparams(("arbitrary",)),
        name="mixer_out_proj",
    )(y, h, mod3, w.astype(BF16))


def _s5_params(lam_re, lam_im, log_dt, b_re, b_im, c_re, c_im):
    ng, npst = lam_re.shape
    gpb = LANES // S5_GROUP_CH
    nj = ng // gpb
    dt = jnp.exp(log_dt.astype(F32))[:, None]
    mag = jnp.exp(lam_re * dt)
    a_re = mag * jnp.cos(lam_im * dt)
    a_im = mag * jnp.sin(lam_im * dt)
    den = lam_re * lam_re + lam_im * lam_im
    n_re = a_re - 1.0
    f_re = (n_re * lam_re + a_im * lam_im) / den
    f_im = (a_im * lam_re - n_re * lam_im) / den
    bb_re = f_re[..., None] * b_re - f_im[..., None] * b_im
    bb_im = f_re[..., None] * b_im + f_im[..., None] * b_re
    eye = jnp.eye(gpb, dtype=F32)

    def blockdiag_in(bb):
        x = bb.reshape(nj, gpb, npst, S5_GROUP_CH).transpose(0, 1, 3, 2)
        return (x[:, :, :, None, :] * eye[None, :, None, :, None]).reshape(nj, LANES, gpb * npst)

    def blockdiag_out(cc):
        x = cc.reshape(nj, gpb, S5_GROUP_CH, npst).transpose(0, 1, 3, 2)
        return (x[:, :, :, None, :] * eye[None, :, None, :, None]).reshape(nj, gpb * npst, LANES)

    b_in = jnp.concatenate([blockdiag_in(bb_re), blockdiag_in(bb_im)], axis=2)
    c_out = jnp.concatenate([blockdiag_out(c_re), blockdiag_out(-c_im)], axis=1)
    a = jnp.stack([a_re.reshape(nj, 1, gpb * npst), a_im.reshape(nj, 1, gpb * npst)], axis=1)
    return a, b_in.astype(BF16), c_out.astype(BF16)


def _s5_dir_kernel(x_ref, *rest, reverse, final):
    nj = rest[-1].shape[0]
    h_ref = rest[-1]
    u_refs = rest[-1 - nj:-1]
    if final:
        yf_ref, d_ref, b_ref, c_ref, a_ref, o_ref = rest[:-1 - nj]
    else:
        b_ref, c_ref, a_ref, o_ref = rest[:-1 - nj]
    sw = b_ref.shape[2] // 2
    steps = x_ref.shape[0] // SUBLANES

    @pl.when(pl.program_id(0) == 0)
    def _():
        h_ref[...] = jnp.zeros_like(h_ref)

    x = x_ref[...]
    xb = x.astype(BF16)
    for j, u_ref in enumerate(u_refs):
        cols = slice(j * LANES, (j + 1) * LANES)
        u_ref[...] = jnp.dot(xb[:, cols], b_ref[j], preferred_element_type=F32)
        a_re = jnp.broadcast_to(a_ref[j, 0], (SUBLANES, sw))
        a_im = jnp.broadcast_to(a_ref[j, 1], (SUBLANES, sw))
        h_re, h_im = h_ref[j, 0], h_ref[j, 1]
        for s in range(steps):
            t = (steps - 1 - s) if reverse else s
            rows = slice(t * SUBLANES, (t + 1) * SUBLANES)
            h_re, h_im = (a_re * h_re - a_im * h_im + u_ref[rows, :sw],
                          a_re * h_im + a_im * h_re + u_ref[rows, sw:])
            u_ref[rows, :sw] = h_re
            u_ref[rows, sw:] = h_im
        h_ref[j, 0] = h_re
        h_ref[j, 1] = h_im
        y = jnp.dot(u_ref[...].astype(BF16), c_ref[j], preferred_element_type=F32)
        if final:
            skip = d_ref[:, cols] * x[:, cols] + yf_ref[:, cols]
            o_ref[:, cols] = _gelu(jnp.maximum(y, -jnp.inf) + skip).astype(BF16)
        else:
            o_ref[:, cols] = y


def s5_layer(h, mod3, layer, g1, p, n_lat, n_ctx, nb, seg):
    d = h.shape[1]
    assert nb == SUBLANES
    l_lat, l_ctx = n_lat // nb, n_ctx // nb
    x_tb = ln_proj(h, mod3, layer, g1, p['w_in'], seg, n_lat, nb).reshape(-1, d)
    rows = S5_TB * nb
    n_blk, n_cblk = (l_ctx + l_lat) // S5_TB, l_ctx // S5_TB
    fwd = lambda i: (i, 0)
    rev = lambda i: (jnp.where(i < n_cblk, n_cblk - 1 - i, n_blk - 1 + n_cblk - i), 0)
    full = lambda a: pl.BlockSpec(a.shape, lambda i: (0,) * a.ndim)
    y = None
    for k, order in enumerate((fwd, rev)):
        a, b_in, c_out = _s5_params(p['lam_re'][k], p['lam_im'][k], p['log_dt'][k], p['b_re'][k], p['b_im'][k],
                                    p['c_re'][k], p['c_im'][k])
        final = k == 1
        blk = pl.BlockSpec((rows, d), order)
        args = [x_tb] + ([y, p['d'].reshape(1, d)] if final else []) + [b_in, c_out, a]
        specs = [blk] + ([blk, full(args[2])] if final else []) + [full(b_in), full(c_out), full(a)]
        y = pl.pallas_call(
            functools.partial(_s5_dir_kernel, reverse=bool(k), final=final),
            grid=(n_blk,),
            in_specs=specs,
            out_specs=blk,
            out_shape=jax.ShapeDtypeStruct(x_tb.shape, BF16 if final else F32),
            scratch_shapes=[pltpu.VMEM((rows, b_in.shape[2]), F32)] * b_in.shape[0]
            + [pltpu.VMEM((b_in.shape[0], 2, SUBLANES, b_in.shape[2] // 2), F32)],
            compiler_params=_cparams(("arbitrary",)),
            name="s5_scan_reverse" if final else "s5_scan_forward",
        )(*args)
    return out_proj(y.reshape(l_ctx + l_lat, nb * d), h, mod3, layer, p['w_glu'], seg, glu=True, time_major=(n_lat, nb))


def _mla_proj_kernel(h_ref, mod_ref, g1_ref, win_ref, gq_ref, gkv_ref, wuq_ref, wukv_ref,
                     gqn_ref, gkn_ref, cos_ref, sin_ref, q_ref, k_ref, v_ref):
    d = h_ref.shape[1]
    mod = mod_ref[0]
    r0 = MLA_Q_RANK + MLA_KV_RANK
    hw = 2 * LANES
    qscale = 1.0 / math.sqrt(MLA_QK)
    gqn, gkn = gqn_ref[...], gkn_ref[...]
    for sb in range(h_ref.shape[0] // SUB):
        rows = slice(sb * SUB, (sb + 1) * SUB)
        n = _adaln(h_ref[rows, :], g1_ref[...], mod[:, 0:d], mod[:, d:2 * d]).astype(BF16)
        z = jnp.dot(n, win_ref[...], preferred_element_type=F32)
        ql = _rms(z[:, :MLA_Q_RANK], gq_ref[...]).astype(BF16)
        kvl = _rms(z[:, MLA_Q_RANK:r0], gkv_ref[...]).astype(BF16)
        pe, pe_sw = z[:, r0:r0 + LANES], z[:, r0 + LANES:r0 + 2 * LANES]
        qa = jnp.dot(ql, wuq_ref[...], preferred_element_type=F32)
        kv = jnp.dot(kvl, wukv_ref[...], preferred_element_type=F32)
        cos, sin = cos_ref[rows, :], sin_ref[rows, :]
        kr = (pe * gkn[1:2]) * cos + (pe_sw * gkn[2:3]) * sin
        pe_ss = jnp.sum(pe * pe, axis=-1, keepdims=True)
        for hd in range(MLA_HEADS):
            qn = qa[:, hd * hw:hd * hw + LANES]
            qr = qa[:, hd * hw + LANES:(hd + 1) * hw]
            qsw = qa[:, MLA_HEADS * hw + hd * LANES:MLA_HEADS * hw + (hd + 1) * LANES]
            ss = jnp.sum(qn * qn, axis=-1, keepdims=True) + jnp.sum(qr * qr, axis=-1, keepdims=True)
            rq = lax.rsqrt(ss * (1.0 / MLA_QK) + EPS) * qscale
            q_ref[rows, hd * hw:hd * hw + LANES] = (qn * rq * gqn[0:1]).astype(BF16)
            q_ref[rows, hd * hw + LANES:(hd + 1) * hw] = (
                rq * ((qr * gqn[1:2]) * cos + (qsw * gqn[2:3]) * sin)).astype(BF16)
            kn = kv[:, hd * LANES:(hd + 1) * LANES]
            rk = lax.rsqrt((jnp.sum(kn * kn, axis=-1, keepdims=True) + pe_ss) * (1.0 / MLA_QK) + EPS)
            k_ref[rows, hd * hw:hd * hw + LANES] = (kn * rk * gkn[0:1]).astype(BF16)
            k_ref[rows, hd * hw + LANES:(hd + 1) * hw] = (kr * rk).astype(BF16)
        v_ref[rows, :] = kv[:, MLA_HEADS * LANES:].astype(BF16)


def _attn_kernel(q_ref, kc_ref, vc_ref, *rest, with_latent):
    if with_latent:
        kl_ref, vl_ref, o_ref = rest
    else:
        (o_ref,) = rest
    nt = (((1,), (1,)), ((), ()))
    tq = q_ref.shape[0]
    qs = min(ATTN_QS, tq)
    for qi in range(tq // qs):
        rows = slice(qi * qs, (qi + 1) * qs)
        q = q_ref[rows, :]
        s1 = lax.dot_general(q, kc_ref[...], nt, preferred_element_type=F32)
        mx = jnp.max(s1, axis=-1, keepdims=True)
        if with_latent:
            s2 = lax.dot_general(q, kl_ref[...], nt, preferred_element_type=F32)
            mx = jnp.maximum(mx, jnp.max(s2, axis=-1, keepdims=True))
        p1 = jnp.exp(s1 - mx)
        den = jnp.sum(p1, axis=-1, keepdims=True)
        o = jnp.dot(p1.astype(BF16), vc_ref[...], preferred_element_type=F32)
        if with_latent:
            p2 = jnp.exp(s2 - mx)
            den = den + jnp.sum(p2, axis=-1, keepdims=True)
            o = o + jnp.dot(p2.astype(BF16), vl_ref[...], preferred_element_type=F32)
        o_ref[rows, :] = (o / den).astype(BF16)


def _rope_tables(l_lat, n_rows_id):
    rows = l_lat // GRID_W
    row = jnp.repeat(jnp.arange(rows), GRID_W).astype(F32)
    col = jnp.tile(jnp.arange(GRID_W), rows).astype(F32)
    quarter = MLA_ROPE // 4
    inv_freq = ROPE_THETA ** (-jnp.arange(quarter, dtype=F32) / quarter)
    ang_r, ang_c = row[:, None] * inv_freq, col[:, None] * inv_freq
    cos = jnp.concatenate([jnp.cos(ang_r), jnp.cos(ang_r), jnp.cos(ang_c), jnp.cos(ang_c)], axis=1)
    sin = jnp.concatenate([-jnp.sin(ang_r), jnp.sin(ang_r), -jnp.sin(ang_c), jnp.sin(ang_c)], axis=1)
    pad = ((0, n_rows_id), (0, LANES - MLA_ROPE))
    cos = jnp.pad(cos, pad).at[l_lat:, :MLA_ROPE].set(1.0)
    return cos, jnp.pad(sin, pad)


def mla_layer(h, mod3, layer, g1, p, n_lat, n_ctx, nb, seg):
    n_rows, d = h.shape
    hw = 2 * LANES
    nh = MLA_HEADS
    l_lat, l_ctx = n_lat // nb, n_ctx // nb
    r0 = MLA_Q_RANK + MLA_KV_RANK
    swap = np.arange(MLA_ROPE)
    swap = np.where(swap % 32 < 16, swap + 16, swap - 16)
    lane_pad = lambda x: jnp.pad(x, ((0, 0),) * (x.ndim - 1) + ((0, LANES - x.shape[-1]),))
    w_pe = p['w_in'][:, r0:]
    w_in_ext = jnp.concatenate([p['w_in'][:, :r0], lane_pad(w_pe), lane_pad(w_pe[:, swap])], axis=1).astype(BF16)
    wq = p['w_uq'].reshape(MLA_Q_RANK, nh, MLA_QK)
    wq_main = jnp.pad(wq, ((0, 0), (0, 0), (0, hw - MLA_QK))).reshape(MLA_Q_RANK, nh * hw)
    wq_sw = lane_pad(wq[:, :, MLA_NOPE:][:, :, swap]).reshape(MLA_Q_RANK, nh * LANES)
    w_uq_ext = jnp.concatenate([wq_main, wq_sw], axis=1).astype(BF16)
    wkv = p['w_ukv'].reshape(MLA_KV_RANK, nh, 2, LANES).transpose(0, 2, 1, 3).reshape(MLA_KV_RANK, 2 * nh * LANES)
    gains = lambda g: jnp.zeros((SUBLANES, LANES), F32).at[0].set(g[:MLA_NOPE]).at[1, :MLA_ROPE].set(
        g[MLA_NOPE:]).at[2, :MLA_ROPE].set(g[MLA_NOPE:][swap])
    cos, sin = _rope_tables(l_lat, seg)
    tm = min(TMK // 2, seg)
    t_lat = l_lat // tm
    tps = seg // tm
    rope_map = lambda i: (jnp.where(i < (n_lat // tm), i % t_lat, t_lat + i % tps), 0)
    full = lambda *shape: pl.BlockSpec(shape, lambda i: (0,) * len(shape))
    q, k, v = pl.pallas_call(
        _mla_proj_kernel,
        grid=(n_rows // tm,),
        in_specs=[pl.BlockSpec((tm, d), lambda i: (i, 0)), _mod_spec(layer, d, tps), full(1, d),
                  full(d, r0 + 2 * LANES), full(1, MLA_Q_RANK), full(1, MLA_KV_RANK),
                  full(MLA_Q_RANK, nh * (hw + LANES)), full(MLA_KV_RANK, 2 * nh * LANES),
                  full(SUBLANES, LANES), full(SUBLANES, LANES),
                  pl.BlockSpec((tm, LANES), rope_map), pl.BlockSpec((tm, LANES), rope_map)],
        out_specs=[pl.BlockSpec((tm, nh * hw), lambda i: (i, 0)), pl.BlockSpec((tm, nh * hw), lambda i: (i, 0)),
                   pl.BlockSpec((tm, nh * LANES), lambda i: (i, 0))],
        out_shape=[jax.ShapeDtypeStruct((n_rows, nh * hw), BF16), jax.ShapeDtypeStruct((n_rows, nh * hw), BF16),
                   jax.ShapeDtypeStruct((n_rows, nh * LANES), BF16)],
        compiler_params=_cparams(("arbitrary",)),
        name="mla_projection",
    )(h, mod3, g1.reshape(1, d), w_in_ext, p['g_q'].reshape(1, -1), p['g_kv'].reshape(1, -1), w_uq_ext,
      wkv.astype(BF16), gains(p['g_qn']), gains(p['g_kn']), cos, sin)

    tq = min(ATTN_TQ, l_lat)
    cb = n_lat // l_ctx
    o_lat = pl.pallas_call(
        functools.partial(_attn_kernel, with_latent=True),
        grid=(nb, nh, l_lat // tq),
        in_specs=[pl.BlockSpec((tq, hw), lambda b, hd, i: (b * (l_lat // tq) + i, hd)),
                  pl.BlockSpec((l_ctx, hw), lambda b, hd, i: (cb + b, hd)),
                  pl.BlockSpec((l_ctx, LANES), lambda b, hd, i: (cb + b, hd)),
                  pl.BlockSpec((l_lat, hw), lambda b, hd, i: (b, hd)),
                  pl.BlockSpec((l_lat, LANES), lambda b, hd, i: (b, hd))],
        out_specs=pl.BlockSpec((tq, LANES), lambda b, hd, i: (b * (l_lat // tq) + i, hd)),
        out_shape=jax.ShapeDtypeStruct((n_lat, nh * LANES), BF16),
        compiler_params=_cparams(("arbitrary", "arbitrary", "arbitrary")),
        name="mla_attention_latent",
    )(q, k, v, k, v)
    o_ctx = pl.pallas_call(
        functools.partial(_attn_kernel, with_latent=False),
        grid=(nb, nh),
        in_specs=[pl.BlockSpec((l_ctx, hw), lambda b, hd: (cb + b, hd)),
                  pl.BlockSpec((l_ctx, hw), lambda b, hd: (cb + b, hd)),
                  pl.BlockSpec((l_ctx, LANES), lambda b, hd: (cb + b, hd))],
        out_specs=pl.BlockSpec((l_ctx, LANES), lambda b, hd: (b, hd)),
        out_shape=jax.ShapeDtypeStruct((n_ctx, nh * LANES), BF16),
        compiler_params=_cparams(("arbitrary", "arbitrary")),
        name="mla_attention_context",
    )(q, k, v)
    return out_proj(jnp.concatenate([o_lat, o_ctx], axis=0), h, mod3, layer, p['w_out'], seg, glu=False)


def _route_kernel(h_ref, mod_ref, g2_ref, wrt_ref, br_ref, n2_ref, pos_ref, gate_ref, cnt_ref):
    d = h_ref.shape[1]
    tm = h_ref.shape[0]
    ne = wrt_ref.shape[1]
    mod = mod_ref[0]
    n2 = _adaln(h_ref[...], g2_ref[...], mod[:, 3 * d:4 * d], mod[:, 4 * d:5 * d])
    n2_hi = n2.astype(BF16)
    n2_ref[...] = n2_hi
    n2_lo = (n2 - n2_hi.astype(F32)).astype(BF16)
    nt = (((1,), (1,)), ((), ()))
    w_hi, w_lo = wrt_ref[0], wrt_ref[1]
    logits = (lax.dot_general(w_hi, n2_hi, nt, preferred_element_type=F32)
              + (lax.dot_general(w_hi, n2_lo, nt, preferred_element_type=F32)
                 + lax.dot_general(w_lo, n2_hi, nt, preferred_element_type=F32))) + br_ref[...]
    eidx = lax.broadcasted_iota(I32, (ne, tm), 0).astype(F32)
    work = logits
    vals, idxs = [], []
    for _ in range(TOP_K):
        mx = jnp.max(work, axis=0, keepdims=True)
        idx = jnp.min(jnp.where(work == mx, eidx, float(ne)), axis=0, keepdims=True)
        vals.append(mx)
        idxs.append(idx)
        work = jnp.where(eidx == idx, -jnp.inf, work)
    ex = [jnp.exp(v - vals[0]) for v in vals]
    den = ex[0] + ex[1] + ex[2] + ex[3]
    sel = (eidx == idxs[0]) | (eidx == idxs[1]) | (eidx == idxs[2]) | (eidx == idxs[3])
    onehot = jnp.where(sel, 1.0, 0.0).astype(BF16)
    r = lax.broadcasted_iota(I32, (tm, tm), 0)
    c = lax.broadcasted_iota(I32, (tm, tm), 1)
    before = jnp.where(r < c, 1.0, 0.0).astype(BF16)
    rank_all = jnp.dot(onehot, before, preferred_element_type=F32)
    cnt = jnp.dot(onehot, jnp.ones((tm, tm), BF16), preferred_element_type=F32)
    cnt8 = jnp.floor((cnt + (SUBLANES - 1)) * (1.0 / SUBLANES)) * SUBLANES
    er = lax.broadcasted_iota(I32, (ne, LANES), 0)
    ec = lax.broadcasted_iota(I32, (ne, LANES), 1)
    lower = jnp.where(ec < er, 1.0, 0.0).astype(BF16)
    cnt8_p = jnp.concatenate([cnt8, jnp.zeros((LANES - ne, tm), F32)], axis=0).astype(BF16)
    off = jnp.dot(lower, cnt8_p, preferred_element_type=F32)
    base = rank_all + off
    for k in range(TOP_K):
        pos = jnp.sum(jnp.where(eidx == idxs[k], base, 0.0), axis=0, keepdims=True)
        pos_ref[k:k + 1, :] = pos.astype(I32)
        gate_ref[k:k + 1, :] = ex[k] / den
    cnt_ref[0] = cnt8[:, :LANES].astype(I32)


def moe_route(h, mod3, layer, g2, w_router, b_router, n_rows, seg):
    d = h.shape[1]
    ne = w_router.shape[1]
    nt = n_rows // TM
    wt = w_router.T
    wt_hi = wt.astype(BF16)
    wt_split = jnp.stack([wt_hi, (wt - wt_hi.astype(F32)).astype(BF16)])
    return pl.pallas_call(
        _route_kernel,
        grid=(nt,),
        in_specs=[pl.BlockSpec((TM, d), lambda i: (i, 0)), _mod_spec(layer, d, seg // TM),
                  pl.BlockSpec((1, d), lambda i: (0, 0)), pl.BlockSpec((2, ne, d), lambda i: (0, 0, 0)),
                  pl.BlockSpec((ne, 1), lambda i: (0, 0))],
        out_specs=[pl.BlockSpec((TM, d), lambda i: (i, 0)), pl.BlockSpec((TOP_K, TM), lambda i: (0, i)),
                   pl.BlockSpec((TOP_K, TM), lambda i: (0, i)), pl.BlockSpec((1, ne, LANES), lambda i: (i, 0, 0))],
        out_shape=[jax.ShapeDtypeStruct((n_rows, d), BF16), jax.ShapeDtypeStruct((TOP_K, n_rows), I32),
                   jax.ShapeDtypeStruct((TOP_K, n_rows), F32), jax.ShapeDtypeStruct((nt, ne, LANES), I32)],
        compiler_params=_cparams(("arbitrary",)),
        name="moe_route",
    )(h, mod3, g2.reshape(1, d), wt_split, b_router.reshape(ne, 1))


def _chunk_copies(tab_ref, tile, e, vm_ref, hbm_ref, sem, to_hbm, pieces):
    ne = N_EXPERTS
    stride = tab_ref.shape[0] // 3
    off = tab_ref[tile * ne + e]
    n8 = tab_ref[stride + tile * ne + e]
    dst = tab_ref[2 * stride + tile * ne + e]
    out = []
    for size in pieces:
        bit = (size // SUBLANES).bit_length() - 1
        done = ((n8 >> (bit + 1)) << (bit + 1)) * SUBLANES
        lo = pl.multiple_of(off + done, SUBLANES)
        hi = pl.multiple_of(dst + done, SUBLANES)
        v, hb = vm_ref.at[pl.ds(lo, size), :], hbm_ref.at[pl.ds(hi, size), :]
        cp = pltpu.make_async_copy(v, hb, sem) if to_hbm else pltpu.make_async_copy(hb, v, sem)
        out.append((((n8 >> bit) & 1) == 1, cp))
    return out


def _for_each_copy(tab_ref, tile, vm_ref, hbm_ref, sem, to_hbm, action, max_rows=TM):
    pieces = tuple(p for p in PIECES if p <= max_rows)
    small = pieces.index(32)

    def body(e, carry):
        copies = _chunk_copies(tab_ref, tile, e, vm_ref, hbm_ref, sem, to_hbm, pieces)

        def run(items):
            for cond, cp in items:
                @pl.when(cond)
                def _():
                    action(cp)
        run(copies[small:])

        @pl.when(tab_ref[tab_ref.shape[0] // 3 + tile * N_EXPERTS + e] >= 2 * pieces[small] // SUBLANES)
        def _():
            run(copies[:small])
        return carry
    lax.fori_loop(0, N_EXPERTS, body, 0)


def _wait_rows(n_rows, vm_ref, hbm_ref, sem, to_hbm):
    n8 = n_rows >> (SUBLANES.bit_length() - 1)
    size = SUBLANES
    while size * 2 <= vm_ref.shape[0]:
        size *= 2
    while size >= SUBLANES:
        v, hb = vm_ref.at[pl.ds(0, size), :], hbm_ref.at[pl.ds(0, size), :]
        cp = pltpu.make_async_copy(v, hb, sem) if to_hbm else pltpu.make_async_copy(hb, v, sem)

        @pl.when(((n8 >> ((size // SUBLANES).bit_length() - 1)) & 1) == 1)
        def _():
            cp.wait()
        size //= 2


def _dispatch_kernel(tab_ref, rows_ref, tail_ref, nb_ref, n2_ref, pos_ref, xb_ref, sorted_ref, zero_ref, sem, zsem):
    i = pl.program_id(0)
    last = pl.num_programs(0) - 1
    slot = i & 1
    cur = sorted_ref.at[slot]
    tm = n2_ref.shape[0]
    rmax = sorted_ref.shape[1]
    ridx = lax.broadcasted_iota(I32, (rmax, tm), 0)
    perm = jnp.where(ridx == pos_ref[0:1, :], 1.0, 0.0)
    for k in range(1, TOP_K):
        perm = perm + jnp.where(ridx == pos_ref[k:k + 1, :], 1.0, 0.0)
    perm = perm.astype(BF16)
    cur[...] = _pack_bf16_pairs(jnp.dot(perm, n2_ref[...], preferred_element_type=F32))
    _for_each_copy(tab_ref, i, cur, xb_ref, sem.at[slot], True, lambda cp: cp.start())

    @pl.when(i > 0)
    def _():
        _wait_rows(rows_ref[i - 1], sorted_ref.at[1 - slot], xb_ref, sem.at[1 - slot], True)

    @pl.when(i == last)
    def _():
        _wait_rows(rows_ref[i], cur, xb_ref, sem.at[slot], True)
        zero_ref[...] = jnp.zeros_like(zero_ref)
        _for_each_copy(tail_ref, 0, zero_ref, xb_ref, zsem, True, lambda cp: cp.start(), max_rows=MOE_BM)
        _for_each_copy(tail_ref, 0, zero_ref, xb_ref, zsem, True, lambda cp: cp.wait(), max_rows=MOE_BM)

        zr = zero_ref.shape[0]

        def spare(b, carry):
            cp = pltpu.make_async_copy(zero_ref, xb_ref.at[pl.ds(pl.multiple_of(b * zr, zr), zr), :], zsem)
            cp.start()
            cp.wait()
            return carry
        lax.fori_loop(nb_ref[0] * (MOE_BM // zr), xb_ref.shape[0] // zr, spare, 0)


def _expert_kernel(be_ref, first_ref, nb_ref, x_ref, wg_ref, bg_ref, wu_ref, bu_ref, wd_ref, bd_ref, y_ref,
                   wg_s, wu_s, wd_s):
    b = pl.program_id(0)

    @pl.when(b < nb_ref[0])
    def _():
        @pl.when((first_ref[b] & 1) == 1)
        def _():
            wg_s[...] = wg_ref[0, 0].astype(BF16)
            wu_s[...] = wu_ref[0, 0].astype(BF16)
            wd_s[...] = wd_ref[0, 0].astype(BF16)

        def ffn(rows):
            x = _unpack_bf16_pairs(x_ref[rows, :])
            g = jnp.minimum(jnp.dot(x, wg_s[...], preferred_element_type=F32) + bg_ref[0], SWIGLU_LIMIT)
            u = jnp.clip(jnp.dot(x, wu_s[...], preferred_element_type=F32) + bu_ref[0], -SWIGLU_LIMIT, SWIGLU_LIMIT)
            a = (g * _sigmoid(SWIGLU_ALPHA * g) * (u + 1.0)).astype(BF16)
            y = jnp.dot(a, wd_s[...], preferred_element_type=F32) + bd_ref[0]
            y_ref[rows, :] = _pack_bf16_pairs(y.astype(BF16).astype(F32))

        n_sub = x_ref.shape[0] // MOE_SUB
        live = first_ref[b] >> 1

        @pl.when(live == n_sub)
        def _():
            for sb in range(n_sub):
                ffn(slice(sb * MOE_SUB, (sb + 1) * MOE_SUB))

        @pl.when(live < n_sub)
        def _():
            for sb in range(n_sub):
                rows = slice(sb * MOE_SUB, (sb + 1) * MOE_SUB)

                @pl.when(sb < live)
                def _():
                    ffn(rows)

                @pl.when(sb >= live)
                def _():
                    y_ref[rows, :] = jnp.zeros((MOE_SUB, y_ref.shape[1]), y_ref.dtype)

    @pl.when(b >= nb_ref[0])
    def _():
        y_ref[...] = jnp.zeros_like(y_ref)


def _combine_kernel(tab_ref, rows_ref, pos_ref, gate_ref, h_ref, mod_ref, yb_ref, o_ref, ys_ref, sem):
    i = pl.program_id(0)
    slot = i & 1
    d = h_ref.shape[1]
    tm = h_ref.shape[0]
    rmax = ys_ref.shape[1]

    @pl.when(i == 0)
    def _():
        ys_ref[...] = jnp.zeros_like(ys_ref)
        _for_each_copy(tab_ref, 0, ys_ref.at[0], yb_ref, sem.at[0], False, lambda cp: cp.start())

    @pl.when(i + 1 < pl.num_programs(0))
    def _():
        _for_each_copy(tab_ref, i + 1, ys_ref.at[1 - slot], yb_ref, sem.at[1 - slot], False, lambda cp: cp.start())

    cidx = lax.broadcasted_iota(I32, (tm, rmax), 1)
    wt = jnp.where(cidx == pos_ref[:, 0:1], gate_ref[:, 0:1], 0.0)
    for k in range(1, TOP_K):
        wt = wt + jnp.where(cidx == pos_ref[:, k:k + 1], gate_ref[:, k:k + 1], 0.0)
    _wait_rows(rows_ref[i], ys_ref.at[slot], yb_ref, sem.at[slot], False)
    f = jnp.dot(wt.astype(BF16), _unpack_bf16_pairs(ys_ref[slot]), preferred_element_type=F32)
    o_ref[...] = h_ref[...] + mod_ref[0][:, 5 * d:6 * d] * f


def _moe_plan(cnt8, n_blocks_max):
    nt, ne = cnt8.shape
    loc_off = jnp.cumsum(cnt8, axis=1) - cnt8
    used = jnp.sum(cnt8, axis=0)
    rows_e = (used + MOE_BM - 1) // MOE_BM * MOE_BM
    e_end = jnp.cumsum(rows_e)
    e_start = e_end - rows_e
    dst = e_start[None, :] + jnp.cumsum(cnt8, axis=0) - cnt8
    tab = jnp.stack([loc_off, cnt8 // SUBLANES, dst]).reshape(3 * nt * ne).astype(I32)
    tail = jnp.stack([jnp.zeros((ne,), I32), (rows_e - used) // SUBLANES, e_start + used]).reshape(3 * ne).astype(I32)
    nb = (e_end[-1] // MOE_BM).astype(I32)
    blk = jnp.arange(n_blocks_max, dtype=I32)
    blk_c = jnp.minimum(blk, nb - 1)
    blk_e = jnp.sum((e_end[None, :] <= (blk_c * MOE_BM)[:, None]).astype(I32), axis=1)
    blk_e = jnp.minimum(blk_e, ne - 1)
    owner = (blk_e[:, None] == jnp.arange(ne, dtype=I32)[None, :]).astype(I32)
    start_b = jnp.sum(owner * e_start[None, :].astype(I32), axis=1)
    used_b = jnp.sum(owner * used[None, :].astype(I32), axis=1)
    first = (blk_c * MOE_BM == start_b).astype(I32)
    live = jnp.clip(start_b + used_b - blk_c * MOE_BM, 0, MOE_BM)
    first = first + 2 * ((live + MOE_SUB - 1) // MOE_SUB).astype(I32)
    tile_rows = jnp.sum(cnt8, axis=1).astype(I32)
    return tab, tile_rows, tail, blk_e, first, nb.reshape(1)


def moe_layer(h, mod3, layer, g2, p, n_rows, seg):
    d = h.shape[1]
    ne = N_EXPERTS
    nt = n_rows // TM
    n2, pos, gate, cnt = moe_route(h, mod3, layer, g2, p['w_router'], p['b_router'], n_rows, seg)
    max_rows = n_rows * TOP_K + nt * ne * (SUBLANES - 1) + ne * (MOE_BM - 1)
    nbm = -(-max_rows // MOE_BM)
    dp = d // 2
    tab, tile_rows, tail, blk_e, first, nb = _moe_plan(cnt[:, :, 0], nbm)

    xb = pl.pallas_call(
        _dispatch_kernel,
        grid_spec=pltpu.PrefetchScalarGridSpec(
            num_scalar_prefetch=4, grid=(nt,),
            in_specs=[pl.BlockSpec((TM, d), lambda i, *_: (i, 0)),
                      pl.BlockSpec((TOP_K, TM), lambda i, *_: (0, i))],
            out_specs=pl.BlockSpec(memory_space=pl.ANY),
            scratch_shapes=[pltpu.VMEM((2, MOE_RMAX, dp), U32), pltpu.VMEM((MOE_BM, dp), U32),
                            pltpu.SemaphoreType.DMA((2,)), pltpu.SemaphoreType.DMA(())]),
        out_shape=jax.ShapeDtypeStruct((nbm * MOE_BM, dp), U32),
        compiler_params=_cparams(("arbitrary",)),
        name="moe_dispatch",
    )(tab, tile_rows, tail, nb, n2, pos)

    row_map = lambda b, be, fi, nbr: (jnp.maximum(jnp.minimum(b, nbr[0] - 1), 0), 0)
    out_map = lambda b, be, fi, nbr: (b, 0)
    w_map = lambda b, be, fi, nbr: (layer, be[b], 0, 0)
    b_map = lambda b, be, fi, nbr: (layer * ne + be[b], 0, 0)
    yb = pl.pallas_call(
        _expert_kernel,
        grid_spec=pltpu.PrefetchScalarGridSpec(
            num_scalar_prefetch=3, grid=(nbm,),
            in_specs=[pl.BlockSpec((MOE_BM, dp), row_map),
                      pl.BlockSpec((1, 1, d, d), w_map), pl.BlockSpec((1, 1, d), b_map),
                      pl.BlockSpec((1, 1, d, d), w_map), pl.BlockSpec((1, 1, d), b_map),
                      pl.BlockSpec((1, 1, d, d), w_map), pl.BlockSpec((1, 1, d), b_map)],
            out_specs=pl.BlockSpec((MOE_BM, dp), out_map),
            scratch_shapes=[pltpu.VMEM((d, d), BF16)] * 3),
        out_shape=jax.ShapeDtypeStruct((nbm * MOE_BM, dp), U32),
        compiler_params=_cparams(("arbitrary",)),
        name="moe_experts",
    )(blk_e, first, nb, xb, p['w_gate'], p['b_gate'].reshape(-1, 1, d), p['w_up'], p['b_up'].reshape(-1, 1, d),
      p['w_down'], p['b_down'].reshape(-1, 1, d))

    return pl.pallas_call(
        _combine_kernel,
        grid_spec=pltpu.PrefetchScalarGridSpec(
            num_scalar_prefetch=2, grid=(nt,),
            in_specs=[pl.BlockSpec((TM, TOP_K), lambda i, *_: (i, 0)),
                      pl.BlockSpec((TM, TOP_K), lambda i, *_: (i, 0)),
                      pl.BlockSpec((TM, d), lambda i, *_: (i, 0)),
                      pl.BlockSpec((1, 1, 6 * d), lambda i, *_: (layer * 16 + i // (seg // TM), 0, 0)),
                      pl.BlockSpec(memory_space=pl.ANY)],
            out_specs=pl.BlockSpec((TM, d), lambda i, *_: (i, 0)),
            scratch_shapes=[pltpu.VMEM((2, MOE_RMAX, dp), U32), pltpu.SemaphoreType.DMA((2,))]),
        out_shape=jax.ShapeDtypeStruct((n_rows, d), F32),
        compiler_params=_cparams(("arbitrary",)),
        name="moe_combine",
    )(tab, tile_rows, pos.T, gate.T, h, mod3, yb)


def kernel(x, c, ctx, c_ctx, w_ada, b_ada, g_norm1, g_norm2, sg_w_in, sg_g_v, sg_w_s, sg_b_s, sg_w_out,
           ssm_w_in, ssm_lam_re, ssm_lam_im, ssm_log_dt, ssm_b_re, ssm_b_im, ssm_c_re, ssm_c_im, ssm_d, ssm_w_glu,
           mla_w_in, mla_g_q, mla_g_kv, mla_w_uq, mla_w_ukv, mla_g_qn, mla_g_kn, mla_w_out,
           moe_w_router, moe_b_router, moe_w_gate, moe_b_gate, moe_w_up, moe_b_up, moe_w_down, moe_b_down):
    nb, l_lat, d = x.shape
    l_ctx = ctx.shape[1]
    depth = w_ada.shape[0]
    n_lat, n_ctx = nb * l_lat, nb * l_ctx
    seg = l_lat
    assert n_ctx <= seg and seg % TM == 0 and n_ctx % TM == 0
    mod3 = modulation(c, c_ctx, w_ada, b_ada)
    h = jnp.concatenate([x.reshape(n_lat, d), ctx.reshape(n_ctx, d)], axis=0)
    for i in range(depth):
        mixer, slot = i % N_MIXERS, i // N_MIXERS
        ctx_out = i < depth - 1
        n_rows = n_lat + n_ctx if ctx_out else n_lat
        if mixer == 0:
            hm = gmlp_layer(h, mod3, i, g_norm1[i], sg_w_in[slot], sg_g_v[slot], sg_w_s[slot], sg_b_s[slot],
                            sg_w_out[slot], n_rows, seg)
        elif mixer == 1:
            p = dict(w_in=ssm_w_in[slot], lam_re=ssm_lam_re[slot], lam_im=ssm_lam_im[slot], log_dt=ssm_log_dt[slot],
                     b_re=ssm_b_re[slot], b_im=ssm_b_im[slot], c_re=ssm_c_re[slot], c_im=ssm_c_im[slot],
                     d=ssm_d[slot], w_glu=ssm_w_glu[slot])
            hm = s5_layer(h, mod3, i, g_norm1[i], p, n_lat, n_ctx, nb, seg)[:n_rows]
        else:
            p = dict(w_in=mla_w_in[slot], g_q=mla_g_q[slot], g_kv=mla_g_kv[slot], w_uq=mla_w_uq[slot],
                     w_ukv=mla_w_ukv[slot], g_qn=mla_g_qn[slot], g_kn=mla_g_kn[slot], w_out=mla_w_out[slot])
            hm = mla_layer(h, mod3, i, g_norm1[i], p, n_lat, n_ctx, nb, seg)[:n_rows]
        pm = dict(w_router=moe_w_router[i], b_router=moe_b_router[i], w_gate=moe_w_gate, b_gate=moe_b_gate,
                  w_up=moe_w_up, b_up=moe_b_up, w_down=moe_w_down, b_down=moe_b_down)
        h = moe_layer(hm, mod3, i, g_norm2[i], pm, n_rows, seg)
    return h[:n_lat].reshape(nb, l_lat, d)
```

```python
import functools
import math

import jax
import jax.numpy as jnp
import numpy as np
from jax import lax
from jax.experimental import pallas as pl
from jax.experimental.pallas import tpu as pltpu

F32 = jnp.float32
BF16 = jnp.bfloat16
I32 = jnp.int32
U32 = jnp.uint32
EPS = 1e-6

N_MIXERS = 3
GRID_W = 64
CHUNK = 128
SG_GROUPS = 8
S5_GROUP_CH = 16
S5_STATE = 64
MLA_HEADS = 8
MLA_NOPE = 128
MLA_ROPE = 64
MLA_QK = MLA_NOPE + MLA_ROPE
MLA_V = 128
MLA_Q_RANK = 384
MLA_KV_RANK = 256
ROPE_THETA = 10000.0
N_EXPERTS = 32
TOP_K = 4
SWIGLU_LIMIT = 7.0
SWIGLU_ALPHA = 1.702

LANES = 128
SUBLANES = 8
VMEM_LIMIT = 56 * 1024 * 1024

TM = 256
TMK = 1024
SUB = 256
S5_TB = 64
ATTN_TQ = 2048
ATTN_QS = 256
MOE_BM = 1024
MOE_SUB = 256
MOE_RMAX = TM * TOP_K + N_EXPERTS * SUBLANES
PIECES = (512, 256, 128, 64, 32, 16, 8)


def _cparams(sem):
    return pltpu.CompilerParams(dimension_semantics=sem, vmem_limit_bytes=VMEM_LIMIT)


def _rms(x, g):
    ms = jnp.mean(x * x, axis=-1, keepdims=True)
    return x * lax.rsqrt(ms + EPS) * g


def _adaln(x, g, shift, scale):
    return _rms(x, g) * (1.0 + scale) + shift


def _gelu(x):
    return 0.5 * x * (1.0 + lax.erf(x * (1.0 / math.sqrt(2.0))))


def _sigmoid(x):
    return 1.0 / (1.0 + jnp.exp(-x))


def _pack_bf16_pairs(x):
    k = x.shape[1] // 2
    lo = lax.bitcast_convert_type(x[:, :k], U32) >> 16
    hi = lax.bitcast_convert_type(x[:, k:], U32) & jnp.uint32(0xFFFF0000)
    return hi | lo


def _unpack_bf16_pairs(u):
    lo = lax.bitcast_convert_type(u << 16, F32).astype(BF16)
    hi = lax.bitcast_convert_type(u & jnp.uint32(0xFFFF0000), F32).astype(BF16)
    return jnp.concatenate([lo, hi], axis=1)


def _subtiles(rows):
    n = max(1, rows // SUB)
    return [slice(i * (rows // n), (i + 1) * (rows // n)) for i in range(n)]


def _mod_spec(layer, d, tiles_per_seg):
    return pl.BlockSpec((1, 1, 6 * d), lambda i: (layer * 16 + i // tiles_per_seg, 0, 0))


def _mod_kernel(s_ref, w_ref, b_ref, o_ref):
    s = s_ref[...]
    s = s * _sigmoid(s)
    w = w_ref[0]
    s_hi, w_hi = s.astype(BF16), w.astype(BF16)
    s_lo, w_lo = (s - s_hi.astype(F32)).astype(BF16), (w - w_hi.astype(F32)).astype(BF16)
    acc = jnp.dot(s_hi, w_lo, preferred_element_type=F32) + jnp.dot(s_lo, w_hi, preferred_element_type=F32)
    o_ref[0] = jnp.dot(s_hi, w_hi, preferred_element_type=F32) + acc + b_ref[0]


def modulation(c, c_ctx, w_ada, b_ada):
    depth, d, d6 = w_ada.shape
    nb = c.shape[0]
    s = jnp.zeros((16, d), F32).at[:nb].set(c).at[nb].set(c_ctx)
    tn = 1024
    out = pl.pallas_call(
        _mod_kernel,
        grid=(depth, d6 // tn),
        in_specs=[pl.BlockSpec((16, d), lambda l, j: (0, 0)),
                  pl.BlockSpec((1, d, tn), lambda l, j: (l, 0, j)),
                  pl.BlockSpec((1, 1, tn), lambda l, j: (l, 0, j))],
        out_specs=pl.BlockSpec((1, 16, tn), lambda l, j: (l, 0, j)),
        out_shape=jax.ShapeDtypeStruct((depth, 16, d6), F32),
        compiler_params=_cparams(("arbitrary", "arbitrary")),
        name="adaln_modulation",
    )(s, w_ada, b_ada.reshape(depth, 1, d6))
    return out.reshape(depth * 16, 1, d6)


def _gmlp_kernel(h_ref, mod_ref, g1_ref, win_ref, gv_ref, ws_ref, bs_ref, wout_ref, o_ref, gated_ref):
    d = h_ref.shape[1]
    mod = mod_ref[0]
    gd = d // SG_GROUPS
    for sb in range(h_ref.shape[0] // SUB):
        r0 = sb * SUB
        h = h_ref[r0:r0 + SUB, :]
        n = _adaln(h, g1_ref[...], mod[:, 0:d], mod[:, d:2 * d]).astype(BF16)
        z = _gelu(jnp.dot(n, win_ref[...], preferred_element_type=F32))
        u = z[:, :d]
        v = _rms(z[:, d:], gv_ref[...]).astype(BF16)
        for j in range(SUB // CHUNK):
            rows = slice(j * CHUNK, (j + 1) * CHUNK)
            out_rows = slice(r0 + j * CHUNK, r0 + (j + 1) * CHUNK)
            for g in range(SG_GROUPS):
                cols = slice(g * gd, (g + 1) * gd)
                sv = jnp.dot(ws_ref[g], v[rows, cols], preferred_element_type=F32) + bs_ref[g]
                gated_ref[out_rows, cols] = (u[rows, cols] * sv).astype(BF16)
        m = jnp.dot(gated_ref[r0:r0 + SUB, :], wout_ref[...], preferred_element_type=F32)
        o_ref[r0:r0 + SUB, :] = h + mod[:, 2 * d:3 * d] * m


def gmlp_layer(h, mod3, layer, g1, w_in, g_v, w_s, b_s, w_out, n_rows, seg):
    d = h.shape[1]
    gd = d // SG_GROUPS
    tm = min(TMK, seg)
    bs_b = jnp.broadcast_to(b_s[:, :, None], (SG_GROUPS, CHUNK, gd)).astype(F32)
    full = lambda *shape: pl.BlockSpec(shape, lambda i: (0,) * len(shape))
    return pl.pallas_call(
        _gmlp_kernel,
        grid=(n_rows // tm,),
        in_specs=[pl.BlockSpec((tm, d), lambda i: (i, 0)),
                  _mod_spec(layer, d, seg // tm),
                  full(1, d), full(d, 2 * d), full(1, d),
                  full(SG_GROUPS, CHUNK, CHUNK), full(SG_GROUPS, CHUNK, gd), full(d, d)],
        out_specs=pl.BlockSpec((tm, d), lambda i: (i, 0)),
        out_shape=jax.ShapeDtypeStruct((n_rows, d), F32),
        scratch_shapes=[pltpu.VMEM((tm, d), BF16)],
        compiler_params=_cparams(("arbitrary",)),
        name="gmlp_mixer",
    )(h, mod3, g1.reshape(1, d), w_in.astype(BF16), g_v.reshape(1, d), w_s.astype(BF16), bs_b,
      w_out.astype(BF16))


def _ln_proj_kernel(h_ref, mod_ref, g1_ref, w_ref, o_ref):
    d = h_ref.shape[1]
    mod = mod_ref[0]
    for rows in _subtiles(h_ref.shape[0]):
        n = _adaln(h_ref[rows, :], g1_ref[...], mod[:, 0:d], mod[:, d:2 * d]).astype(BF16)
        o_ref[rows, :] = jnp.dot(n, w_ref[...], preferred_element_type=F32)


def _time_major_map(tm, n_lat, l_lat, l_ctx):
    lat_tiles, tpl, tpc = n_lat // tm, l_lat // tm, l_ctx // tm

    def index_map(i):
        j = i - lat_tiles
        return (jnp.where(i < lat_tiles, tpc + i % tpl, j % tpc), jnp.where(i < lat_tiles, i // tpl, j // tpc))
    return index_map


def ln_proj(h, mod3, layer, g1, w, seg, n_lat, nb):
    n_rows, d = h.shape
    dn = w.shape[1]
    l_lat, l_ctx = n_lat // nb, (n_rows - n_lat) // nb
    tm = min(SUB, l_ctx)
    return pl.pallas_call(
        _ln_proj_kernel,
        grid=(n_rows // tm,),
        in_specs=[pl.BlockSpec((tm, d), lambda i: (i, 0)),
                  _mod_spec(layer, d, seg // tm),
                  pl.BlockSpec((1, d), lambda i: (0, 0)),
                  pl.BlockSpec((d, dn), lambda i: (0, 0))],
        out_specs=pl.BlockSpec((tm, dn), _time_major_map(tm, n_lat, l_lat, l_ctx)),
        out_shape=jax.ShapeDtypeStruct((l_ctx + l_lat, nb * dn), F32),
        compiler_params=_cparams(("arbitrary",)),
        name="adaln_in_proj",
    )(h, mod3, g1.reshape(1, d), w.astype(BF16))


def _out_proj_kernel(y_ref, h_ref, mod_ref, w_ref, o_ref, *, glu):
    d = h_ref.shape[1]
    mod = mod_ref[0]
    for rows in _subtiles(h_ref.shape[0]):
        z = jnp.dot(y_ref[rows, :], w_ref[...], preferred_element_type=F32)
        if glu:
            z = z[:, :d] * _sigmoid(z[:, d:])
        o_ref[rows, :] = h_ref[rows, :] + mod[:, 2 * d:3 * d] * z


def out_proj(y, h, mod3, layer, w, seg, glu, time_major=None):
    n_rows, d = h.shape
    dk, dn = w.shape
    if time_major is None:
        tm = min(TMK, seg)
        y_map = lambda i: (i, 0)
    else:
        n_lat, nb = time_major
        l_lat, l_ctx = n_lat // nb, (n_rows - n_lat) // nb
        tm = min(SUB, l_ctx)
        y_map = _time_major_map(tm, n_lat, l_lat, l_ctx)
    return pl.pallas_call(
        functools.partial(_out_proj_kernel, glu=glu),
        grid=(n_rows // tm,),
        in_specs=[pl.BlockSpec((tm, dk), y_map),
                  pl.BlockSpec((tm, d), lambda i: (i, 0)),
                  _mod_spec(layer, d, seg // tm),
                  pl.BlockSpec((dk, dn), lambda i: (0, 0))],
        out_specs=pl.BlockSpec((tm, d), lambda i: (i, 0)),
        out_shape=jax.ShapeDtypeStruct((n_rows, d), F32),
        compiler_params=_cparams(("arbitrary",)),
        name="mixer_out_proj",
    )(y, h, mod3, w.astype(BF16))


def _s5_params(lam_re, lam_im, log_dt, b_re, b_im, c_re, c_im):
    ng, npst = lam_re.shape
    gpb = LANES // S5_GROUP_CH
    nj = ng // gpb
    dt = jnp.exp(log_dt.astype(F32))[:, None]
    mag = jnp.exp(lam_re * dt)
    a_re = mag * jnp.cos(lam_im * dt)
    a_im = mag * jnp.sin(lam_im * dt)
    den = lam_re * lam_re + lam_im * lam_im
    n_re = a_re - 1.0
    f_re = (n_re * lam_re + a_im * lam_im) / den
    f_im = (a_im * lam_re - n_re * lam_im) / den
    bb_re = f_re[..., None] * b_re - f_im[..., None] * b_im
    bb_im = f_re[..., None] * b_im + f_im[..., None] * b_re
    eye = jnp.eye(gpb, dtype=F32)

    def blockdiag_in(bb):
        x = bb.reshape(nj, gpb, npst, S5_GROUP_CH).transpose(0, 1, 3, 2)
        return (x[:, :, :, None, :] * eye[None, :, None, :, None]).reshape(nj, LANES, gpb * npst)

    def blockdiag_out(cc):
        x = cc.reshape(nj, gpb, S5_GROUP_CH, npst).transpose(0, 1, 3, 2)
        return (x[:, :, :, None, :] * eye[None, :, None, :, None]).reshape(nj, gpb * npst, LANES)

    b_in = jnp.concatenate([blockdiag_in(bb_re), blockdiag_in(bb_im)], axis=2)
    c_out = jnp.concatenate([blockdiag_out(c_re), blockdiag_out(-c_im)], axis=1)
    a = jnp.stack([a_re.reshape(nj, 1, gpb * npst), a_im.reshape(nj, 1, gpb * npst)], axis=1)
    return a, b_in.astype(BF16), c_out.astype(BF16)


def _s5_dir_kernel(x_ref, *rest, reverse, final):
    nj = rest[-1].shape[0]
    h_ref = rest[-1]
    u_refs = rest[-1 - nj:-1]
    if final:
        yf_ref, d_ref, b_ref, c_ref, a_ref, o_ref = rest[:-1 - nj]
    else:
        b_ref, c_ref, a_ref, o_ref = rest[:-1 - nj]
    sw = b_ref.shape[2] // 2
    steps = x_ref.shape[0] // SUBLANES

    @pl.when(pl.program_id(0) == 0)
    def _():
        h_ref[...] = jnp.zeros_like(h_ref)

    x = x_ref[...]
    xb = x.astype(BF16)
    for j, u_ref in enumerate(u_refs):
        cols = slice(j * LANES, (j + 1) * LANES)
        u_ref[...] = jnp.dot(xb[:, cols], b_ref[j], preferred_element_type=F32)
        a_re = jnp.broadcast_to(a_ref[j, 0], (SUBLANES, sw))
        a_im = jnp.broadcast_to(a_ref[j, 1], (SUBLANES, sw))
        h_re, h_im = h_ref[j, 0], h_ref[j, 1]
        for s in range(steps):
            t = (steps - 1 - s) if reverse else s
            rows = slice(t * SUBLANES, (t + 1) * SUBLANES)
            h_re, h_im = (a_re * h_re - a_im * h_im + u_ref[rows, :sw],
                          a_re * h_im + a_im * h_re + u_ref[rows, sw:])
            u_ref[rows, :sw] = h_re
            u_ref[rows, sw:] = h_im
        h_ref[j, 0] = h_re
        h_ref[j, 1] = h_im
        y = jnp.dot(u_ref[...].astype(BF16), c_ref[j], preferred_element_type=F32)
        if final:
            skip = d_ref[:, cols] * x[:, cols] + yf_ref[:, cols]
            o_ref[:, cols] = _gelu(jnp.maximum(y, -jnp.inf) + skip).astype(BF16)
        else:
            o_ref[:, cols] = y


def s5_layer(h, mod3, layer, g1, p, n_lat, n_ctx, nb, seg):
    d = h.shape[1]
    assert nb == SUBLANES
    l_lat, l_ctx = n_lat // nb, n_ctx // nb
    x_tb = ln_proj(h, mod3, layer, g1, p['w_in'], seg, n_lat, nb).reshape(-1, d)
    rows = S5_TB * nb
    n_blk, n_cblk = (l_ctx + l_lat) // S5_TB, l_ctx // S5_TB
    fwd = lambda i: (i, 0)
    rev = lambda i: (jnp.where(i < n_cblk, n_cblk - 1 - i, n_blk - 1 + n_cblk - i), 0)
    full = lambda a: pl.BlockSpec(a.shape, lambda i: (0,) * a.ndim)
    y = None
    for k, order in enumerate((fwd, rev)):
        a, b_in, c_out = _s5_params(p['lam_re'][k], p['lam_im'][k], p['log_dt'][k], p['b_re'][k], p['b_im'][k],
                                    p['c_re'][k], p['c_im'][k])
        final = k == 1
        blk = pl.BlockSpec((rows, d), order)
        args = [x_tb] + ([y, p['d'].reshape(1, d)] if final else []) + [b_in, c_out, a]
        specs = [blk] + ([blk, full(args[2])] if final else []) + [full(b_in), full(c_out), full(a)]
        y = pl.pallas_call(
            functools.partial(_s5_dir_kernel, reverse=bool(k), final=final),
            grid=(n_blk,),
            in_specs=specs,
            out_specs=blk,
            out_shape=jax.ShapeDtypeStruct(x_tb.shape, BF16 if final else F32),
            scratch_shapes=[pltpu.VMEM((rows, b_in.shape[2]), F32)] * b_in.shape[0]
            + [pltpu.VMEM((b_in.shape[0], 2, SUBLANES, b_in.shape[2] // 2), F32)],
            compiler_params=_cparams(("arbitrary",)),
            name="s5_scan_reverse" if final else "s5_scan_forward",
        )(*args)
    return out_proj(y.reshape(l_ctx + l_lat, nb * d), h, mod3, layer, p['w_glu'], seg, glu=True, time_major=(n_lat, nb))


def _mla_proj_kernel(h_ref, mod_ref, g1_ref, win_ref, gq_ref, gkv_ref, wuq_ref, wukv_ref,
                     gqn_ref, gkn_ref, cos_ref, sin_ref, q_ref, k_ref, v_ref):
    d = h_ref.shape[1]
    mod = mod_ref[0]
    r0 = MLA_Q_RANK + MLA_KV_RANK
    hw = 2 * LANES
    qscale = 1.0 / math.sqrt(MLA_QK)
    gqn, gkn = gqn_ref[...], gkn_ref[...]
    for sb in range(h_ref.shape[0] // SUB):
        rows = slice(sb * SUB, (sb + 1) * SUB)
        n = _adaln(h_ref[rows, :], g1_ref[...], mod[:, 0:d], mod[:, d:2 * d]).astype(BF16)
        z = jnp.dot(n, win_ref[...], preferred_element_type=F32)
        ql = _rms(z[:, :MLA_Q_RANK], gq_ref[...]).astype(BF16)
        kvl = _rms(z[:, MLA_Q_RANK:r0], gkv_ref[...]).astype(BF16)
        pe, pe_sw = z[:, r0:r0 + LANES], z[:, r0 + LANES:r0 + 2 * LANES]
        qa = jnp.dot(ql, wuq_ref[...], preferred_element_type=F32)
        kv = jnp.dot(kvl, wukv_ref[...], preferred_element_type=F32)
        cos, sin = cos_ref[rows, :], sin_ref[rows, :]
        kr = (pe * gkn[1:2]) * cos + (pe_sw * gkn[2:3]) * sin
        pe_ss = jnp.sum(pe * pe, axis=-1, keepdims=True)
        for hd in range(MLA_HEADS):
            qn = qa[:, hd * hw:hd * hw + LANES]
            qr = qa[:, hd * hw + LANES:(hd + 1) * hw]
            qsw = qa[:, MLA_HEADS * hw + hd * LANES:MLA_HEADS * hw + (hd + 1) * LANES]
            ss = jnp.sum(qn * qn, axis=-1, keepdims=True) + jnp.sum(qr * qr, axis=-1, keepdims=True)
            rq = lax.rsqrt(ss * (1.0 / MLA_QK) + EPS) * qscale
            q_ref[rows, hd * hw:hd * hw + LANES] = (qn * rq * gqn[0:1]).astype(BF16)
            q_ref[rows, hd * hw + LANES:(hd + 1) * hw] = (
                rq * ((qr * gqn[1:2]) * cos + (qsw * gqn[2:3]) * sin)).astype(BF16)
            kn = kv[:, hd * LANES:(hd + 1) * LANES]
            rk = lax.rsqrt((jnp.sum(kn * kn, axis=-1, keepdims=True) + pe_ss) * (1.0 / MLA_QK) + EPS)
            k_ref[rows, hd * hw:hd * hw + LANES] = (kn * rk * gkn[0:1]).astype(BF16)
            k_ref[rows, hd * hw + LANES:(hd + 1) * hw] = (kr * rk).astype(BF16)
        v_ref[rows, :] = kv[:, MLA_HEADS * LANES:].astype(BF16)


def _attn_kernel(q_ref, kc_ref, vc_ref, *rest, with_latent):
    if with_latent:
        kl_ref, vl_ref, o_ref = rest
    else:
        (o_ref,) = rest
    nt = (((1,), (1,)), ((), ()))
    tq = q_ref.shape[0]
    qs = min(ATTN_QS, tq)
    for qi in range(tq // qs):
        rows = slice(qi * qs, (qi + 1) * qs)
        q = q_ref[rows, :]
        s1 = lax.dot_general(q, kc_ref[...], nt, preferred_element_type=F32)
        mx = jnp.max(s1, axis=-1, keepdims=True)
        if with_latent:
            s2 = lax.dot_general(q, kl_ref[...], nt, preferred_element_type=F32)
            mx = jnp.maximum(mx, jnp.max(s2, axis=-1, keepdims=True))
        p1 = jnp.exp(s1 - mx)
        den = jnp.sum(p1, axis=-1, keepdims=True)
        o = jnp.dot(p1.astype(BF16), vc_ref[...], preferred_element_type=F32)
        if with_latent:
            p2 = jnp.exp(s2 - mx)
            den = den + jnp.sum(p2, axis=-1, keepdims=True)
            o = o + jnp.dot(p2.astype(BF16), vl_ref[...], preferred_element_type=F32)
        o_ref[rows, :] = (o / den).astype(BF16)


def _rope_tables(l_lat, n_rows_id):
    rows = l_lat // GRID_W
    row = jnp.repeat(jnp.arange(rows), GRID_W).astype(F32)
    col = jnp.tile(jnp.arange(GRID_W), rows).astype(F32)
    quarter = MLA_ROPE // 4
    inv_freq = ROPE_THETA ** (-jnp.arange(quarter, dtype=F32) / quarter)
    ang_r, ang_c = row[:, None] * inv_freq, col[:, None] * inv_freq
    cos = jnp.concatenate([jnp.cos(ang_r), jnp.cos(ang_r), jnp.cos(ang_c), jnp.cos(ang_c)], axis=1)
    sin = jnp.concatenate([-jnp.sin(ang_r), jnp.sin(ang_r), -jnp.sin(ang_c), jnp.sin(ang_c)], axis=1)
    pad = ((0, n_rows_id), (0, LANES - MLA_ROPE))
    cos = jnp.pad(cos, pad).at[l_lat:, :MLA_ROPE].set(1.0)
    return cos, jnp.pad(sin, pad)


def mla_layer(h, mod3, layer, g1, p, n_lat, n_ctx, nb, seg):
    n_rows, d = h.shape
    hw = 2 * LANES
    nh = MLA_HEADS
    l_lat, l_ctx = n_lat // nb, n_ctx // nb
    r0 = MLA_Q_RANK + MLA_KV_RANK
    swap = np.arange(MLA_ROPE)
    swap = np.where(swap % 32 < 16, swap + 16, swap - 16)
    lane_pad = lambda x: jnp.pad(x, ((0, 0),) * (x.ndim - 1) + ((0, LANES - x.shape[-1]),))
    w_pe = p['w_in'][:, r0:]
    w_in_ext = jnp.concatenate([p['w_in'][:, :r0], lane_pad(w_pe), lane_pad(w_pe[:, swap])], axis=1).astype(BF16)
    wq = p['w_uq'].reshape(MLA_Q_RANK, nh, MLA_QK)
    wq_main = jnp.pad(wq, ((0, 0), (0, 0), (0, hw - MLA_QK))).reshape(MLA_Q_RANK, nh * hw)
    wq_sw = lane_pad(wq[:, :, MLA_NOPE:][:, :, swap]).reshape(MLA_Q_RANK, nh * LANES)
    w_uq_ext = jnp.concatenate([wq_main, wq_sw], axis=1).astype(BF16)
    wkv = p['w_ukv'].reshape(MLA_KV_RANK, nh, 2, LANES).transpose(0, 2, 1, 3).reshape(MLA_KV_RANK, 2 * nh * LANES)
    gains = lambda g: jnp.zeros((SUBLANES, LANES), F32).at[0].set(g[:MLA_NOPE]).at[1, :MLA_ROPE].set(
        g[MLA_NOPE:]).at[2, :MLA_ROPE].set(g[MLA_NOPE:][swap])
    cos, sin = _rope_tables(l_lat, seg)
    tm = min(TMK // 2, seg)
    t_lat = l_lat // tm
    tps = seg // tm
    rope_map = lambda i: (jnp.where(i < (n_lat // tm), i % t_lat, t_lat + i % tps), 0)
    full = lambda *shape: pl.BlockSpec(shape, lambda i: (0,) * len(shape))
    q, k, v = pl.pallas_call(
        _mla_proj_kernel,
        grid=(n_rows // tm,),
        in_specs=[pl.BlockSpec((tm, d), lambda i: (i, 0)), _mod_spec(layer, d, tps), full(1, d),
                  full(d, r0 + 2 * LANES), full(1, MLA_Q_RANK), full(1, MLA_KV_RANK),
                  full(MLA_Q_RANK, nh * (hw + LANES)), full(MLA_KV_RANK, 2 * nh * LANES),
                  full(SUBLANES, LANES), full(SUBLANES, LANES),
                  pl.BlockSpec((tm, LANES), rope_map), pl.BlockSpec((tm, LANES), rope_map)],
        out_specs=[pl.BlockSpec((tm, nh * hw), lambda i: (i, 0)), pl.BlockSpec((tm, nh * hw), lambda i: (i, 0)),
                   pl.BlockSpec((tm, nh * LANES), lambda i: (i, 0))],
        out_shape=[jax.ShapeDtypeStruct((n_rows, nh * hw), BF16), jax.ShapeDtypeStruct((n_rows, nh * hw), BF16),
                   jax.ShapeDtypeStruct((n_rows, nh * LANES), BF16)],
        compiler_params=_cparams(("arbitrary",)),
        name="mla_projection",
    )(h, mod3, g1.reshape(1, d), w_in_ext, p['g_q'].reshape(1, -1), p['g_kv'].reshape(1, -1), w_uq_ext,
      wkv.astype(BF16), gains(p['g_qn']), gains(p['g_kn']), cos, sin)

    tq = min(ATTN_TQ, l_lat)
    cb = n_lat // l_ctx
    o_lat = pl.pallas_call(
        functools.partial(_attn_kernel, with_latent=True),
        grid=(nb, nh, l_lat // tq),
        in_specs=[pl.BlockSpec((tq, hw), lambda b, hd, i: (b * (l_lat // tq) + i, hd)),
                  pl.BlockSpec((l_ctx, hw), lambda b, hd, i: (cb + b, hd)),
                  pl.BlockSpec((l_ctx, LANES), lambda b, hd, i: (cb + b, hd)),
                  pl.BlockSpec((l_lat, hw), lambda b, hd, i: (b, hd)),
                  pl.BlockSpec((l_lat, LANES), lambda b, hd, i: (b, hd))],
        out_specs=pl.BlockSpec((tq, LANES), lambda b, hd, i: (b * (l_lat // tq) + i, hd)),
        out_shape=jax.ShapeDtypeStruct((n_lat, nh * LANES), BF16),
        compiler_params=_cparams(("arbitrary", "arbitrary", "arbitrary")),
        name="mla_attention_latent",
    )(q, k, v, k, v)
    o_ctx = pl.pallas_call(
        functools.partial(_attn_kernel, with_latent=False),
        grid=(nb, nh),
        in_specs=[pl.BlockSpec((l_ctx, hw), lambda b, hd: (cb + b, hd)),
                  pl.BlockSpec((l_ctx, hw), lambda b, hd: (cb + b, hd)),
                  pl.BlockSpec((l_ctx, LANES), lambda b, hd: (cb + b, hd))],
        out_specs=pl.BlockSpec((l_ctx, LANES), lambda b, hd: (b, hd)),
        out_shape=jax.ShapeDtypeStruct((n_ctx, nh * LANES), BF16),
        compiler_params=_cparams(("arbitrary", "arbitrary")),
        name="mla_attention_context",
    )(q, k, v)
    return out_proj(jnp.concatenate([o_lat, o_ctx], axis=0), h, mod3, layer, p['w_out'], seg, glu=False)


def _route_kernel(h_ref, mod_ref, g2_ref, wrt_ref, br_ref, n2_ref, pos_ref, gate_ref, cnt_ref):
    for sb, rows in enumerate(_subtiles(h_ref.shape[0])):
        _route_tile(h_ref[rows, :], mod_ref[0], g2_ref, wrt_ref, br_ref, n2_ref, pos_ref, gate_ref, cnt_ref, sb, rows)


def _route_tile(h, mod, g2_ref, wrt_ref, br_ref, n2_ref, pos_ref, gate_ref, cnt_ref, sb, rows):
    tm, d = h.shape
    ne = wrt_ref.shape[1]
    n2 = _adaln(h, g2_ref[...], mod[:, 3 * d:4 * d], mod[:, 4 * d:5 * d])
    n2_hi = n2.astype(BF16)
    n2_ref[rows, :] = n2_hi
    n2_lo = (n2 - n2_hi.astype(F32)).astype(BF16)
    nt = (((1,), (1,)), ((), ()))
    w_hi, w_lo = wrt_ref[0], wrt_ref[1]
    logits = (lax.dot_general(w_hi, n2_hi, nt, preferred_element_type=F32)
              + (lax.dot_general(w_hi, n2_lo, nt, preferred_element_type=F32)
                 + lax.dot_general(w_lo, n2_hi, nt, preferred_element_type=F32))) + br_ref[...]
    eidx = lax.broadcasted_iota(I32, (ne, tm), 0).astype(F32)
    work = logits
    vals, idxs = [], []
    for _ in range(TOP_K):
        mx = jnp.max(work, axis=0, keepdims=True)
        idx = jnp.min(jnp.where(work == mx, eidx, float(ne)), axis=0, keepdims=True)
        vals.append(mx)
        idxs.append(idx)
        work = jnp.where(eidx == idx, -jnp.inf, work)
    ex = [jnp.exp(v - vals[0]) for v in vals]
    den = ex[0] + ex[1] + ex[2] + ex[3]
    sel = (eidx == idxs[0]) | (eidx == idxs[1]) | (eidx == idxs[2]) | (eidx == idxs[3])
    onehot = jnp.where(sel, 1.0, 0.0).astype(BF16)
    r = lax.broadcasted_iota(I32, (tm, tm), 0)
    c = lax.broadcasted_iota(I32, (tm, tm), 1)
    before = jnp.where(r < c, 1.0, 0.0).astype(BF16)
    rank_all = jnp.dot(onehot, before, preferred_element_type=F32)
    cnt = jnp.dot(onehot, jnp.ones((tm, tm), BF16), preferred_element_type=F32)
    cnt8 = jnp.floor((cnt + (SUBLANES - 1)) * (1.0 / SUBLANES)) * SUBLANES
    er = lax.broadcasted_iota(I32, (ne, LANES), 0)
    ec = lax.broadcasted_iota(I32, (ne, LANES), 1)
    lower = jnp.where(ec < er, 1.0, 0.0).astype(BF16)
    cnt8_p = jnp.concatenate([cnt8, jnp.zeros((LANES - ne, tm), F32)], axis=0).astype(BF16)
    off = jnp.dot(lower, cnt8_p, preferred_element_type=F32)
    base = rank_all + off
    for k in range(TOP_K):
        pos = jnp.sum(jnp.where(eidx == idxs[k], base, 0.0), axis=0, keepdims=True)
        pos_ref[k:k + 1, rows] = pos.astype(I32)
        gate_ref[k:k + 1, rows] = ex[k] / den
    cnt_ref[sb] = cnt8[:, :LANES].astype(I32)


def moe_route(h, mod3, layer, g2, w_router, b_router, n_rows, seg):
    d = h.shape[1]
    ne = w_router.shape[1]
    nt = n_rows // TM
    wt = w_router.T
    wt_hi = wt.astype(BF16)
    wt_split = jnp.stack([wt_hi, (wt - wt_hi.astype(F32)).astype(BF16)])
    tr = min(TMK, seg)
    return pl.pallas_call(
        _route_kernel,
        grid=(n_rows // tr,),
        in_specs=[pl.BlockSpec((tr, d), lambda i: (i, 0)), _mod_spec(layer, d, seg // tr),
                  pl.BlockSpec((1, d), lambda i: (0, 0)), pl.BlockSpec((2, ne, d), lambda i: (0, 0, 0)),
                  pl.BlockSpec((ne, 1), lambda i: (0, 0))],
        out_specs=[pl.BlockSpec((tr, d), lambda i: (i, 0)), pl.BlockSpec((TOP_K, tr), lambda i: (0, i)),
                   pl.BlockSpec((TOP_K, tr), lambda i: (0, i)),
                   pl.BlockSpec((tr // TM, ne, LANES), lambda i: (i, 0, 0))],
        out_shape=[jax.ShapeDtypeStruct((n_rows, d), BF16), jax.ShapeDtypeStruct((TOP_K, n_rows), I32),
                   jax.ShapeDtypeStruct((TOP_K, n_rows), F32), jax.ShapeDtypeStruct((nt, ne, LANES), I32)],
        compiler_params=_cparams(("arbitrary",)),
        name="moe_route",
    )(h, mod3, g2.reshape(1, d), wt_split, b_router.reshape(ne, 1))


def _chunk_copies(tab_ref, tile, e, vm_ref, hbm_ref, sem, to_hbm, pieces):
    ne = N_EXPERTS
    stride = tab_ref.shape[0] // 3
    off = tab_ref[tile * ne + e]
    n8 = tab_ref[stride + tile * ne + e]
    dst = tab_ref[2 * stride + tile * ne + e]
    out = []
    for size in pieces:
        bit = (size // SUBLANES).bit_length() - 1
        done = ((n8 >> (bit + 1)) << (bit + 1)) * SUBLANES
        lo = pl.multiple_of(off + done, SUBLANES)
        hi = pl.multiple_of(dst + done, SUBLANES)
        v, hb = vm_ref.at[pl.ds(lo, size), :], hbm_ref.at[pl.ds(hi, size), :]
        cp = pltpu.make_async_copy(v, hb, sem) if to_hbm else pltpu.make_async_copy(hb, v, sem)
        out.append((((n8 >> bit) & 1) == 1, cp))
    return out


def _for_each_copy(tab_ref, tile, vm_ref, hbm_ref, sem, to_hbm, action, max_rows=TM):
    pieces = tuple(p for p in PIECES if p <= max_rows)
    small = pieces.index(32)

    def body(e, carry):
        copies = _chunk_copies(tab_ref, tile, e, vm_ref, hbm_ref, sem, to_hbm, pieces)

        def run(items):
            for cond, cp in items:
                @pl.when(cond)
                def _():
                    action(cp)
        run(copies[small:])

        @pl.when(tab_ref[tab_ref.shape[0] // 3 + tile * N_EXPERTS + e] >= 2 * pieces[small] // SUBLANES)
        def _():
            run(copies[:small])
        return carry
    lax.fori_loop(0, N_EXPERTS, body, 0)


def _wait_rows(n_rows, vm_ref, hbm_ref, sem, to_hbm):
    n8 = n_rows >> (SUBLANES.bit_length() - 1)
    size = SUBLANES
    while size * 2 <= vm_ref.shape[0]:
        size *= 2
    while size >= SUBLANES:
        v, hb = vm_ref.at[pl.ds(0, size), :], hbm_ref.at[pl.ds(0, size), :]
        cp = pltpu.make_async_copy(v, hb, sem) if to_hbm else pltpu.make_async_copy(hb, v, sem)

        @pl.when(((n8 >> ((size // SUBLANES).bit_length() - 1)) & 1) == 1)
        def _():
            cp.wait()
        size //= 2


def _dispatch_kernel(tab_ref, rows_ref, tail_ref, nb_ref, n2_ref, pos_ref, xb_ref, sorted_ref, zero_ref, sem, zsem):
    i = pl.program_id(0)
    last = pl.num_programs(0) - 1
    slot = i & 1
    cur = sorted_ref.at[slot]
    tm = n2_ref.shape[0]
    rmax = sorted_ref.shape[1]
    rb = 256
    ridx = lax.broadcasted_iota(I32, (rb, tm), 0).astype(F32).astype(BF16)
    blocks = []
    for r0 in range(0, rmax, rb):
        local = [(pos_ref[k:k + 1, :] - r0).astype(F32).astype(BF16) for k in range(TOP_K)]
        one, zero = jnp.ones((), BF16), jnp.zeros((), BF16)
        blk = jnp.where(ridx == local[0], one, zero)
        for k in range(1, TOP_K):
            blk = blk + jnp.where(ridx == local[k], one, zero)
        blocks.append(blk)
    perm = jnp.concatenate(blocks, axis=0)
    cur[...] = _pack_bf16_pairs(jnp.dot(perm, n2_ref[...], preferred_element_type=F32))
    _for_each_copy(tab_ref, i, cur, xb_ref, sem.at[slot], True, lambda cp: cp.start())

    @pl.when(i > 0)
    def _():
        _wait_rows(rows_ref[i - 1], sorted_ref.at[1 - slot], xb_ref, sem.at[1 - slot], True)

    @pl.when(i == last)
    def _():
        _wait_rows(rows_ref[i], cur, xb_ref, sem.at[slot], True)
        zero_ref[...] = jnp.zeros_like(zero_ref)
        _for_each_copy(tail_ref, 0, zero_ref, xb_ref, zsem, True, lambda cp: cp.start(), max_rows=MOE_BM)
        _for_each_copy(tail_ref, 0, zero_ref, xb_ref, zsem, True, lambda cp: cp.wait(), max_rows=MOE_BM)

        zr = zero_ref.shape[0]

        def spare(b, carry):
            cp = pltpu.make_async_copy(zero_ref, xb_ref.at[pl.ds(pl.multiple_of(b * zr, zr), zr), :], zsem)
            cp.start()
            cp.wait()
            return carry
        lax.fori_loop(nb_ref[0] * (MOE_BM // zr), xb_ref.shape[0] // zr, spare, 0)


def _expert_kernel(be_ref, first_ref, nb_ref, x_ref, wg_ref, bg_ref, wu_ref, bu_ref, wd_ref, bd_ref, y_ref,
                   wg_s, wu_s, wd_s):
    b = pl.program_id(0)

    @pl.when(b < nb_ref[0])
    def _():
        @pl.when((first_ref[b] & 1) == 1)
        def _():
            wg_s[...] = wg_ref[0, 0].astype(BF16)
            wu_s[...] = wu_ref[0, 0].astype(BF16)
            wd_s[...] = wd_ref[0, 0].astype(BF16)

        def ffn(rows):
            x = _unpack_bf16_pairs(x_ref[rows, :])
            g = jnp.minimum(jnp.dot(x, wg_s[...], preferred_element_type=F32) + bg_ref[0], SWIGLU_LIMIT)
            u = jnp.clip(jnp.dot(x, wu_s[...], preferred_element_type=F32) + bu_ref[0], -SWIGLU_LIMIT, SWIGLU_LIMIT)
            a = (g * _sigmoid(SWIGLU_ALPHA * g) * (u + 1.0)).astype(BF16)
            y = jnp.dot(a, wd_s[...], preferred_element_type=F32) + bd_ref[0]
            y_ref[rows, :] = _pack_bf16_pairs(y.astype(BF16).astype(F32))

        n_sub = x_ref.shape[0] // MOE_SUB
        live = first_ref[b] >> 1

        @pl.when(live == n_sub)
        def _():
            for sb in range(n_sub):
                ffn(slice(sb * MOE_SUB, (sb + 1) * MOE_SUB))

        @pl.when(live < n_sub)
        def _():
            for sb in range(n_sub):
                rows = slice(sb * MOE_SUB, (sb + 1) * MOE_SUB)

                @pl.when(sb < live)
                def _():
                    ffn(rows)

                @pl.when(sb >= live)
                def _():
                    y_ref[rows, :] = jnp.zeros((MOE_SUB, y_ref.shape[1]), y_ref.dtype)

    @pl.when(b >= nb_ref[0])
    def _():
        y_ref[...] = jnp.zeros_like(y_ref)


def _combine_kernel(tab_ref, rows_ref, pos_ref, gate_ref, h_ref, mod_ref, yb_ref, o_ref, ys_ref, sem):
    i = pl.program_id(0)
    slot = i & 1
    d = h_ref.shape[1]
    tm = h_ref.shape[0]
    rmax = ys_ref.shape[1]

    @pl.when(i == 0)
    def _():
        ys_ref[...] = jnp.zeros_like(ys_ref)
        _for_each_copy(tab_ref, 0, ys_ref.at[0], yb_ref, sem.at[0], False, lambda cp: cp.start())

    @pl.when(i + 1 < pl.num_programs(0))
    def _():
        _for_each_copy(tab_ref, i + 1, ys_ref.at[1 - slot], yb_ref, sem.at[1 - slot], False, lambda cp: cp.start())

    cb = 256
    cidx = lax.broadcasted_iota(I32, (tm, cb), 1).astype(F32).astype(BF16)
    gates = [gate_ref[:, k:k + 1].astype(BF16) for k in range(TOP_K)]
    zero = jnp.zeros((), BF16)
    blocks = []
    for c0 in range(0, rmax, cb):
        local = [(pos_ref[:, k:k + 1] - c0).astype(F32).astype(BF16) for k in range(TOP_K)]
        blk = jnp.where(cidx == local[0], gates[0], zero)
        for k in range(1, TOP_K):
            blk = blk + jnp.where(cidx == local[k], gates[k], zero)
        blocks.append(blk)
    wt = jnp.concatenate(blocks, axis=1)
    _wait_rows(rows_ref[i], ys_ref.at[slot], yb_ref, sem.at[slot], False)
    f = jnp.dot(wt, _unpack_bf16_pairs(ys_ref[slot]), preferred_element_type=F32)
    o_ref[...] = h_ref[...] + mod_ref[0][:, 5 * d:6 * d] * f


def _moe_plan(cnt8, n_blocks_max):
    nt, ne = cnt8.shape
    loc_off = jnp.cumsum(cnt8, axis=1) - cnt8
    used = jnp.sum(cnt8, axis=0)
    rows_e = (used + MOE_BM - 1) // MOE_BM * MOE_BM
    e_end = jnp.cumsum(rows_e)
    e_start = e_end - rows_e
    dst = e_start[None, :] + jnp.cumsum(cnt8, axis=0) - cnt8
    tab = jnp.stack([loc_off, cnt8 // SUBLANES, dst]).reshape(3 * nt * ne).astype(I32)
    tail = jnp.stack([jnp.zeros((ne,), I32), (rows_e - used) // SUBLANES, e_start + used]).reshape(3 * ne).astype(I32)
    nb = (e_end[-1] // MOE_BM).astype(I32)
    blk = jnp.arange(n_blocks_max, dtype=I32)
    blk_c = jnp.minimum(blk, nb - 1)
    blk_e = jnp.sum((e_end[None, :] <= (blk_c * MOE_BM)[:, None]).astype(I32), axis=1)
    blk_e = jnp.minimum(blk_e, ne - 1)
    owner = (blk_e[:, None] == jnp.arange(ne, dtype=I32)[None, :]).astype(I32)
    start_b = jnp.sum(owner * e_start[None, :].astype(I32), axis=1)
    used_b = jnp.sum(owner * used[None, :].astype(I32), axis=1)
    first = (blk_c * MOE_BM == start_b).astype(I32)
    live = jnp.clip(start_b + used_b - blk_c * MOE_BM, 0, MOE_BM)
    first = first + 2 * ((live + MOE_SUB - 1) // MOE_SUB).astype(I32)
    tile_rows = jnp.sum(cnt8, axis=1).astype(I32)
    return tab, tile_rows, tail, blk_e, first, nb.reshape(1)


def moe_layer(h, mod3, layer, g2, p, n_rows, seg):
    d = h.shape[1]
    ne = N_EXPERTS
    nt = n_rows // TM
    n2, pos, gate, cnt = moe_route(h, mod3, layer, g2, p['w_router'], p['b_router'], n_rows, seg)
    max_rows = n_rows * TOP_K + nt * ne * (SUBLANES - 1) + ne * (MOE_BM - 1)
    nbm = -(-max_rows // MOE_BM)
    dp = d // 2
    tab, tile_rows, tail, blk_e, first, nb = _moe_plan(cnt[:, :, 0], nbm)

    xb = pl.pallas_call(
        _dispatch_kernel,
        grid_spec=pltpu.PrefetchScalarGridSpec(
            num_scalar_prefetch=4, grid=(nt,),
            in_specs=[pl.BlockSpec((TM, d), lambda i, *_: (i, 0)),
                      pl.BlockSpec((TOP_K, TM), lambda i, *_: (0, i))],
            out_specs=pl.BlockSpec(memory_space=pl.ANY),
            scratch_shapes=[pltpu.VMEM((2, MOE_RMAX, dp), U32), pltpu.VMEM((MOE_BM, dp), U32),
                            pltpu.SemaphoreType.DMA((2,)), pltpu.SemaphoreType.DMA(())]),
        out_shape=jax.ShapeDtypeStruct((nbm * MOE_BM, dp), U32),
        compiler_params=_cparams(("arbitrary",)),
        name="moe_dispatch",
    )(tab, tile_rows, tail, nb, n2, pos)

    row_map = lambda b, be, fi, nbr: (jnp.maximum(jnp.minimum(b, nbr[0] - 1), 0), 0)
    out_map = lambda b, be, fi, nbr: (b, 0)
    w_map = lambda b, be, fi, nbr: (layer, be[b], 0, 0)
    b_map = lambda b, be, fi, nbr: (layer * ne + be[b], 0, 0)
    yb = pl.pallas_call(
        _expert_kernel,
        grid_spec=pltpu.PrefetchScalarGridSpec(
            num_scalar_prefetch=3, grid=(nbm,),
            in_specs=[pl.BlockSpec((MOE_BM, dp), row_map),
                      pl.BlockSpec((1, 1, d, d), w_map), pl.BlockSpec((1, 1, d), b_map),
                      pl.BlockSpec((1, 1, d, d), w_map), pl.BlockSpec((1, 1, d), b_map),
                      pl.BlockSpec((1, 1, d, d), w_map), pl.BlockSpec((1, 1, d), b_map)],
            out_specs=pl.BlockSpec((MOE_BM, dp), out_map),
            scratch_shapes=[pltpu.VMEM((d, d), BF16)] * 3),
        out_shape=jax.ShapeDtypeStruct((nbm * MOE_BM, dp), U32),
        compiler_params=_cparams(("arbitrary",)),
        name="moe_experts",
    )(blk_e, first, nb, xb, p['w_gate'], p['b_gate'].reshape(-1, 1, d), p['w_up'], p['b_up'].reshape(-1, 1, d),
      p['w_down'], p['b_down'].reshape(-1, 1, d))

    return pl.pallas_call(
        _combine_kernel,
        grid_spec=pltpu.PrefetchScalarGridSpec(
            num_scalar_prefetch=2, grid=(nt,),
            in_specs=[pl.BlockSpec((TM, TOP_K), lambda i, *_: (i, 0)),
                      pl.BlockSpec((TM, TOP_K), lambda i, *_: (i, 0)),
                      pl.BlockSpec((TM, d), lambda i, *_: (i, 0)),
                      pl.BlockSpec((1, 1, 6 * d), lambda i, *_: (layer * 16 + i // (seg // TM), 0, 0)),
                      pl.BlockSpec(memory_space=pl.ANY)],
            out_specs=pl.BlockSpec((TM, d), lambda i, *_: (i, 0)),
            scratch_shapes=[pltpu.VMEM((2, MOE_RMAX, dp), U32), pltpu.SemaphoreType.DMA((2,))]),
        out_shape=jax.ShapeDtypeStruct((n_rows, d), F32),
        compiler_params=_cparams(("arbitrary",)),
        name="moe_combine",
    )(tab, tile_rows, pos.T, gate.T, h, mod3, yb)


def kernel(x, c, ctx, c_ctx, w_ada, b_ada, g_norm1, g_norm2, sg_w_in, sg_g_v, sg_w_s, sg_b_s, sg_w_out,
           ssm_w_in, ssm_lam_re, ssm_lam_im, ssm_log_dt, ssm_b_re, ssm_b_im, ssm_c_re, ssm_c_im, ssm_d, ssm_w_glu,
           mla_w_in, mla_g_q, mla_g_kv, mla_w_uq, mla_w_ukv, mla_g_qn, mla_g_kn, mla_w_out,
           moe_w_router, moe_b_router, moe_w_gate, moe_b_gate, moe_w_up, moe_b_up, moe_w_down, moe_b_down):
    nb, l_lat, d = x.shape
    l_ctx = ctx.shape[1]
    depth = w_ada.shape[0]
    n_lat, n_ctx = nb * l_lat, nb * l_ctx
    seg = l_lat
    assert n_ctx <= seg and seg % TM == 0 and n_ctx % TM == 0
    mod3 = modulation(c, c_ctx, w_ada, b_ada)
    h = jnp.concatenate([x.reshape(n_lat, d), ctx.reshape(n_ctx, d)], axis=0)
    for i in range(depth):
        mixer, slot = i % N_MIXERS, i // N_MIXERS
        ctx_out = i < depth - 1
        n_rows = n_lat + n_ctx if ctx_out else n_lat
        if mixer == 0:
            hm = gmlp_layer(h, mod3, i, g_norm1[i], sg_w_in[slot], sg_g_v[slot], sg_w_s[slot], sg_b_s[slot],
                            sg_w_out[slot], n_rows, seg)
        elif mixer == 1:
            p = dict(w_in=ssm_w_in[slot], lam_re=ssm_lam_re[slot], lam_im=ssm_lam_im[slot], log_dt=ssm_log_dt[slot],
                     b_re=ssm_b_re[slot], b_im=ssm_b_im[slot], c_re=ssm_c_re[slot], c_im=ssm_c_im[slot],
                     d=ssm_d[slot], w_glu=ssm_w_glu[slot])
            hm = s5_layer(h, mod3, i, g_norm1[i], p, n_lat, n_ctx, nb, seg)[:n_rows]
        else:
            p = dict(w_in=mla_w_in[slot], g_q=mla_g_q[slot], g_kv=mla_g_kv[slot], w_uq=mla_w_uq[slot],
                     w_ukv=mla_w_ukv[slot], g_qn=mla_g_qn[slot], g_kn=mla_g_kn[slot], w_out=mla_w_out[slot])
            hm = mla_layer(h, mod3, i, g_norm1[i], p, n_lat, n_ctx, nb, seg)[:n_rows]
        pm = dict(w_router=moe_w_router[i], b_router=moe_b_router[i], w_gate=moe_w_gate, b_gate=moe_b_gate,
                  w_up=moe_w_up, b_up=moe_b_up, w_down=moe_w_down, b_down=moe_b_down)
        h = moe_layer(hm, mod3, i, g_norm2[i], pm, n_rows, seg)
    return h[:n_lat].reshape(nb, l_lat, d)
```
